```python
import jax
import jax.numpy as jnp
from jax import lax
import numpy as np

D_MODEL = 1024
BATCH = 32
SEQ = 256
DEPTH = 1
DEC_BATCH = 2
DEC_SEQ = 1024
PAST_LEN = 512

GRID_W = 64
HEAD_DIM = 64
N_Q_HEADS = 8
N_KV_HEADS = 2
Q_GROUP = N_Q_HEADS // N_KV_HEADS
ATTN_Q_W = N_Q_HEADS * HEAD_DIM
ATTN_KV_W = N_KV_HEADS * HEAD_DIM
Q_BLOCK = 128
ROPE_THETA = 10000.0
DN_HEADS = 8
DN_DK = 64
DN_DV = 64
DN_QK_W = DN_HEADS * DN_DK
DN_V_W = DN_HEADS * DN_DV
DN_CONV = 5
DN_CHUNK = 64
D_FF = -(-8 * D_MODEL // (3 * 256)) * 256
EPS = 1e-6
IN_WIDTHS = (ATTN_Q_W, ATTN_KV_W, ATTN_KV_W, DN_QK_W, DN_QK_W, DN_V_W, DN_V_W,
             2 * DN_HEADS, 2 * DN_HEADS, D_MODEL, D_MODEL)
IN_W = sum(IN_WIDTHS)

kernel_name = 'hybrid_gqa_gdn_prefix_diffusion_step'


def rms_norm(x, g):
    xf = x.astype(jnp.float32)
    y = xf * lax.rsqrt(jnp.mean(xf * xf, axis=-1, keepdims=True) + EPS)
    return (y * g.astype(jnp.float32)).astype(x.dtype)


def l2_norm(x):
    xf = x.astype(jnp.float32)
    return xf * lax.rsqrt(jnp.sum(xf * xf, axis=-1, keepdims=True) + EPS)


def rope_1d(x, pos):
    half = x.shape[-1] // 2
    inv = ROPE_THETA ** (-jnp.arange(half, dtype=jnp.float32) / half)
    ang = pos[:, None] * inv[None, :]
    cos = jnp.cos(ang)[None, :, None, :]
    sin = jnp.sin(ang)[None, :, None, :]
    x1, x2 = x[..., :half], x[..., half:]
    return jnp.concatenate([x1 * cos - x2 * sin, x2 * cos + x1 * sin], axis=-1)


def axial_rope(x):
    n_tokens = x.shape[1]
    rows = n_tokens // GRID_W
    row_pos = jnp.repeat(jnp.arange(rows, dtype=jnp.float32), GRID_W)
    col_pos = jnp.tile(jnp.arange(GRID_W, dtype=jnp.float32), rows)
    xf = x.astype(jnp.float32)
    ax = HEAD_DIM // 2
    out = jnp.concatenate([rope_1d(xf[..., :ax], row_pos), rope_1d(xf[..., ax:], col_pos)], axis=-1)
    return out.astype(x.dtype)


def block_attention(q, k, v):
    b, sq = q.shape[0], q.shape[1]
    nb = sq // Q_BLOCK
    qb = q.reshape(b, nb, Q_BLOCK, N_KV_HEADS, Q_GROUP, HEAD_DIM).transpose(1, 0, 2, 3, 4, 5)
    scale = HEAD_DIM ** -0.5

    def one_block(q_blk):
        s = jnp.einsum('bqhgd,bkhd->bhgqk', q_blk, k, preferred_element_type=jnp.float32) * scale
        p = jax.nn.softmax(s, axis=-1)
        return jnp.einsum('bhgqk,bkhd->bqhgd', p.astype(v.dtype), v)

    o = lax.map(one_block, qb)
    return o.transpose(1, 0, 2, 3, 4, 5).reshape(b, sq, ATTN_Q_W)


def short_conv(x, w):
    ch = x.shape[-1]
    pad = (DN_CONV - 1) // 2
    y = lax.conv_general_dilated(x, w[:, None, :].astype(x.dtype), window_strides=(1,),
                                 padding=[(pad, pad)], dimension_numbers=('NWC', 'WIO', 'NWC'),
                                 feature_group_count=ch)
    return jax.nn.silu(y)


def chunk_gated_delta(q, k, v, g, beta, s0):
    b, s, h, dk = q.shape
    dv = v.shape[-1]
    n = s // DN_CHUNK
    c = DN_CHUNK

    def chunks(a):
        return a.reshape(b, n, c, *a.shape[2:]).swapaxes(2, 3)

    q = chunks(q) * (dk ** -0.5)
    k = chunks(k)
    v = chunks(v)
    g = jnp.cumsum(chunks(g), axis=-1)
    beta = chunks(beta)
    k_beta = k * beta[..., None]
    v_beta = v * beta[..., None]
    tril_incl = jnp.tril(jnp.ones((c, c), dtype=bool))
    tril_strict = jnp.tril(jnp.ones((c, c), dtype=bool), -1)
    decay = jnp.exp(jnp.where(tril_incl, g[..., :, None] - g[..., None, :], -jnp.inf))
    lmat = jnp.where(tril_strict, jnp.einsum('bnhid,bnhjd->bnhij', k_beta, k) * decay, 0.0)
    eye = jnp.eye(c, dtype=jnp.float32)
    t_inv = lax.linalg.triangular_solve(eye + lmat, jnp.broadcast_to(eye, lmat.shape),
                                        left_side=True, lower=True, unit_diagonal=True)
    u = jnp.einsum('bnhij,bnhjd->bnhid', t_inv, v_beta)
    w = jnp.einsum('bnhij,bnhjd->bnhid', t_inv, k_beta * jnp.exp(g)[..., None])
    a_intra = jnp.where(tril_incl, jnp.einsum('bnhid,bnhjd->bnhij', q, k) * decay, 0.0)

    def step(state, xs):
        q_i, k_i, u_i, w_i, g_i, a_i = xs
        v_new = u_i - jnp.einsum('bhcd,bhde->bhce', w_i, state)
        o_i = (jnp.einsum('bhcd,bhde->bhce', q_i * jnp.exp(g_i)[..., None], state)
               + jnp.einsum('bhij,bhje->bhie', a_i, v_new))
        g_last = g_i[..., -1]
        state = (state * jnp.exp(g_last)[..., None, None]
                 + jnp.einsum('bhcd,bhce->bhde', k_i * jnp.exp(g_last[..., None] - g_i)[..., None], v_new))
        return state, o_i

    xs = tuple(a.swapaxes(0, 1) for a in (q, k, u, w, g, a_intra))
    s_final, o = lax.scan(step, s0.astype(jnp.float32), xs)
    o = o.transpose(1, 0, 3, 2, 4).reshape(b, s, h, dv)
    return o, s_final


def delta_branch(dq, dk_, dv_, dz, dbeta, dalpha, conv_w, a_log, dt_bias, dn_norm_g, s_f, s_b):
    b, s = dq.shape[0], dq.shape[1]
    qkv = short_conv(jnp.concatenate([dq, dk_, dv_], axis=-1), conv_w)
    cq, ck, cv = jnp.split(qkv, [DN_QK_W, 2 * DN_QK_W], axis=-1)
    cq = l2_norm(cq.reshape(b, s, DN_HEADS, DN_DK))
    ck = l2_norm(ck.reshape(b, s, DN_HEADS, DN_DK))
    cv = cv.reshape(b, s, DN_HEADS, DN_DV).astype(jnp.float32)
    beta = jax.nn.sigmoid(dbeta.astype(jnp.float32)).reshape(b, s, 2, DN_HEADS)
    g = -jnp.exp(a_log.astype(jnp.float32)) * jax.nn.softplus(
        dalpha.astype(jnp.float32).reshape(b, s, 2, DN_HEADS) + dt_bias.astype(jnp.float32))
    o_f, sf = chunk_gated_delta(cq, ck, cv, g[:, :, 0], beta[:, :, 0], s_f)
    flip = lambda a: a[:, ::-1]
    o_b, sb = chunk_gated_delta(flip(cq), flip(ck), flip(cv), flip(g[:, :, 1]), flip(beta[:, :, 1]), s_b)
    o = rms_norm(o_f + flip(o_b), dn_norm_g) * jax.nn.silu(dz.astype(jnp.float32).reshape(b, s, DN_HEADS, DN_DV))
    return o.reshape(b, s, DN_V_W).astype(dq.dtype), sf, sb


def mixer_block(h, lw, ctx):
    b, s = h.shape[0], h.shape[1]
    split_idx = np.cumsum(IN_WIDTHS)[:-1].tolist()
    aq, ak, av, dq, dk_, dv_, dz, dbeta, dalpha, ga, gb = jnp.split(h @ lw['w_in'], split_idx, axis=-1)
    q = rms_norm(aq.reshape(b, s, N_Q_HEADS, HEAD_DIM), lw['q_norm_g'])
    k = rms_norm(ak.reshape(b, s, N_KV_HEADS, HEAD_DIM), lw['k_norm_g'])
    v = av.reshape(b, s, N_KV_HEADS, HEAD_DIM)
    if ctx is None:
        o_a = block_attention(q, k, v)
        zero = jnp.zeros((b, DN_HEADS, DN_DK, DN_DV), jnp.float32)
        s_f0, s_b0 = zero, zero
    else:
        ctx_k, ctx_v, s_f0, s_b0 = ctx
        keys = jnp.concatenate([ctx_k.astype(k.dtype), axial_rope(k)], axis=1)
        vals = jnp.concatenate([ctx_v.astype(v.dtype), v], axis=1)
        o_a = block_attention(axial_rope(q), keys, vals)
    o_d, sf, sb = delta_branch(dq, dk_, dv_, dz, dbeta, dalpha, lw['conv_w'], lw['a_log'], lw['dt_bias'],
                               lw['dn_norm_g'], s_f0, s_b0)
    merged = jax.nn.sigmoid(ga) * (o_a @ lw['w_pa']) + jax.nn.sigmoid(gb) * (o_d @ lw['w_pb'])
    return merged @ lw['w_o'], (k, v, sf, sb)


def layer_forward(x, mod, lw, ctx):
    sh1, sc1, g1, sh2, sc2, g2 = jnp.split(mod.astype(x.dtype), 6, axis=-1)
    h = rms_norm(x, lw['norm1_g']) * (1.0 + sc1) + sh1
    mix, ctx_out = mixer_block(h, lw, ctx)
    x = x + g1 * mix
    h2 = rms_norm(x, lw['norm2_g']) * (1.0 + sc2) + sh2
    ffn = (jax.nn.silu(h2 @ lw['w_gate']) * (h2 @ lw['w_up'])) @ lw['w_down']
    return x + g2 * ffn, ctx_out


def setup_inputs(seed: int = 0) -> dict:
    key = jax.random.key(seed)
    ks = jax.random.split(key, 32)
    nrm = lambda k, shape, s: jax.random.normal(k, shape, jnp.float32) * s
    dt = jax.random.uniform(ks[13], (DEPTH, 2, DN_HEADS), jnp.float32, 0.001, 0.1)
    return {
        'x_prompt': nrm(ks[0], (BATCH, SEQ, D_MODEL), 1.0),
        'x_sample': nrm(ks[1], (DEC_BATCH, DEC_SEQ, D_MODEL), 1.0),
        'cache_k': nrm(ks[2], (DEC_BATCH, DEPTH, PAST_LEN, N_KV_HEADS, HEAD_DIM), 1.0),
        'cache_v': nrm(ks[3], (DEC_BATCH, DEPTH, PAST_LEN, N_KV_HEADS, HEAD_DIM), 1.0),
        'state_fwd': nrm(ks[4], (DEC_BATCH, DEPTH, DN_HEADS, DN_DK, DN_DV), 0.1),
        'state_bwd': nrm(ks[5], (DEC_BATCH, DEPTH, DN_HEADS, DN_DK, DN_DV), 0.1),
        'c': nrm(ks[6], (DEC_BATCH, D_MODEL), 1.0),
        'c_ctx': nrm(ks[7], (D_MODEL,), 1.0),
        'w_ada': nrm(ks[8], (DEPTH, D_MODEL, 6 * D_MODEL), 0.5 * D_MODEL ** -0.5),
        'b_ada': nrm(ks[9], (DEPTH, 6 * D_MODEL), 0.02),
        'norm1_g': 1.0 + nrm(ks[10], (DEPTH, D_MODEL), 0.1),
        'w_in': nrm(ks[11], (DEPTH, D_MODEL, IN_W), D_MODEL ** -0.5),
        'q_norm_g': 1.0 + nrm(ks[12], (DEPTH, HEAD_DIM), 0.1),
        'k_norm_g': 1.0 + nrm(ks[14], (DEPTH, HEAD_DIM), 0.1),
        'conv_w': nrm(ks[15], (DEPTH, DN_CONV, 2 * DN_QK_W + DN_V_W), DN_CONV ** -0.5),
        'a_log': jnp.log(jax.random.uniform(ks[16], (DEPTH, 2, DN_HEADS), jnp.float32, 1.0, 16.0)),
        'dt_bias': jnp.log(jnp.expm1(dt)),
        'dn_norm_g': 1.0 + nrm(ks[17], (DEPTH, DN_DV), 0.1),
        'w_pa': nrm(ks[18], (DEPTH, ATTN_Q_W, D_MODEL), ATTN_Q_W ** -0.5),
        'w_pb': nrm(ks[19], (DEPTH, DN_V_W, D_MODEL), DN_V_W ** -0.5),
        'w_o': nrm(ks[20], (DEPTH, D_MODEL, D_MODEL), D_MODEL ** -0.5),
        'norm2_g': 1.0 + nrm(ks[21], (DEPTH, D_MODEL), 0.1),
        'w_gate': nrm(ks[22], (DEPTH, D_MODEL, D_FF), D_MODEL ** -0.5),
        'w_up': nrm(ks[23], (DEPTH, D_MODEL, D_FF), D_MODEL ** -0.5),
        'w_down': nrm(ks[24], (DEPTH, D_FF, D_MODEL), D_FF ** -0.5),
    }


def reference(x_prompt, x_sample, cache_k, cache_v, state_fwd, state_bwd, c, c_ctx,
              w_ada, b_ada, norm1_g, w_in, q_norm_g, k_norm_g, conv_w, a_log, dt_bias, dn_norm_g,
              w_pa, w_pb, w_o, norm2_g, w_gate, w_up, w_down):
    y_prompt = x_prompt
    y_sample = x_sample
    ks_out, vs_out, sf_out, sb_out = [], [], [], []
    for l in range(DEPTH):
        lw = dict(w_in=w_in[l], q_norm_g=q_norm_g[l], k_norm_g=k_norm_g[l], conv_w=conv_w[l],
                  a_log=a_log[l], dt_bias=dt_bias[l], dn_norm_g=dn_norm_g[l], w_pa=w_pa[l], w_pb=w_pb[l],
                  w_o=w_o[l], norm1_g=norm1_g[l], norm2_g=norm2_g[l], w_gate=w_gate[l], w_up=w_up[l],
                  w_down=w_down[l])
        mod_ctx = (jax.nn.silu(c_ctx) @ w_ada[l] + b_ada[l]).reshape(1, 1, 6 * D_MODEL)
        y_prompt, (k_c, v_c, sf_c, sb_c) = layer_forward(y_prompt, mod_ctx, lw, None)
        ks_out.append(k_c)
        vs_out.append(v_c)
        sf_out.append(sf_c)
        sb_out.append(sb_c)
        mod_lat = (jax.nn.silu(c) @ w_ada[l] + b_ada[l])[:, None, :]
        ctx = (cache_k[:, l], cache_v[:, l], state_fwd[:, l], state_bwd[:, l])
        y_sample, _ = layer_forward(y_sample, mod_lat, lw, ctx)
    new_cache_k = jnp.stack(ks_out, axis=1)
    new_cache_v = jnp.stack(vs_out, axis=1)
    new_state_fwd = jnp.stack(sf_out, axis=1)
    new_state_bwd = jnp.stack(sb_out, axis=1)
    return (y_prompt, y_sample, new_cache_k, new_cache_v, new_state_fwd, new_state_bwd)
```

```python
import functools
import math

import jax
import jax.numpy as jnp
from jax import lax
from jax.experimental import pallas as pl
from jax.experimental.pallas import tpu as pltpu

D_MODEL = 1024
GRID_W = 64
HEAD_DIM = 64
N_Q_HEADS = 8
N_KV_HEADS = 2
Q_GROUP = N_Q_HEADS // N_KV_HEADS
ATTN_Q_W = N_Q_HEADS * HEAD_DIM
ATTN_KV_W = N_KV_HEADS * HEAD_DIM
ROPE_THETA = 10000.0
DN_HEADS = 8
DN_DK = 64
DN_DV = 64
DN_W = DN_HEADS * DN_DK
DN_CONV = 5
CHUNK = 64
EPS = 1e-6

LANES = 128
MXU_DIM = 256
VMEM_LIMIT_BYTES = 56 * 1024 * 1024

HEADS_PER_GROUP = MXU_DIM // DN_DK
N_GROUPS = DN_HEADS // HEADS_PER_GROUP
GW = HEADS_PER_GROUP * DN_DK

F32 = jnp.float32
BF16 = jnp.bfloat16


def _dot(a, b):
    return jnp.dot(a, b, preferred_element_type=F32)


def _dot_nt(a, b):
    return lax.dot_general(a, b, (((1,), (1,)), ((), ())), preferred_element_type=F32)


def _dot_tn(a, b):
    return lax.dot_general(a, b, (((0,), (0,)), ((), ())), preferred_element_type=F32)


def _split2(x):
    hi = x.astype(BF16)
    lo = (x - hi.astype(F32)).astype(BF16)
    return hi, lo


def _split3(x):
    hi = x.astype(BF16)
    r = x - hi.astype(F32)
    mid = r.astype(BF16)
    lo = (r - mid.astype(F32)).astype(BF16)
    return hi, mid, lo


def _dot_x3(a, b):
    m = a.shape[0]
    ah, al = _split2(a)
    bh, bl = _split2(b)
    r = _dot(jnp.concatenate([ah, al], axis=0), bh)
    return r[:m] + r[m:] + _dot(ah, bl)


def _dot_exact_rhs(a, b01):
    m = a.shape[0]
    a1, a2, a3 = _split3(a)
    r = _dot(jnp.concatenate([a1, a2, a3], axis=0), b01)
    return r[:m] + r[m:2 * m] + r[2 * m:]


def _dot_exact_lhs(a01, b):
    n = b.shape[1]
    b1, b2, b3 = _split3(b)
    r = _dot(a01, jnp.concatenate([b1, b2, b3], axis=1))
    return r[:, :n] + r[:, n:2 * n] + r[:, 2 * n:]


def _iota(shape, dim):
    return lax.broadcasted_iota(jnp.int32, shape, dim)


def _same_block(shape, width):
    return (_iota(shape, 0) // width) == (_iota(shape, 1) // width)


def _head_sumsq(x, width):
    n = x.shape[1]
    sel = _same_block((n, n), width).astype(BF16)
    sq = x * x
    hi, lo = _split2(sq)
    m = x.shape[0]
    r = _dot(jnp.concatenate([hi, lo], axis=0), sel)
    return r[:m] + r[m:]


def _sigmoid(x):
    return 1.0 / (1.0 + jnp.exp(-x))


def _silu(x):
    return x * _sigmoid(x)


def _softplus(x):
    return jnp.maximum(x, 0.0) + jnp.log(1.0 + jnp.exp(-jnp.abs(x)))


def _resident(shape):
    nd = len(shape)
    return pl.BlockSpec(shape, lambda *_: (0,) * nd, pipeline_mode=pl.Buffered(1))


def _params(n_axes=1):
    return pltpu.CompilerParams(dimension_semantics=("arbitrary",) * n_axes,
                                vmem_limit_bytes=VMEM_LIMIT_BYTES)


def _mod_kernel(c_ref, w_ref, b_ref, o_ref):
    c = c_ref[...]
    o_ref[...] = _dot_x3(_silu(c), w_ref[...]) + b_ref[...]


def _modulation(cvecs, w_ada, b_ada):
    n = w_ada.shape[1]
    tn = 1024
    return pl.pallas_call(
        _mod_kernel,
        grid=(n // tn,),
        in_specs=[pl.BlockSpec((8, D_MODEL), lambda j: (0, 0)),
                  pl.BlockSpec((D_MODEL, tn), lambda j: (0, j)),
                  pl.BlockSpec((1, tn), lambda j: (0, j))],
        out_specs=pl.BlockSpec((8, tn), lambda j: (0, j)),
        out_shape=jax.ShapeDtypeStruct((8, n), F32),
        compiler_params=_params(),
        name="adaln_modulation",
    )(cvecs, w_ada, b_ada.reshape(1, n))


def _mod_parts(mod_ref, first):
    m = mod_ref[...]
    base = 0 if first else 3 * D_MODEL
    return (m[:, base:base + D_MODEL], m[:, base + D_MODEL:base + 2 * D_MODEL],
            m[:, base + 2 * D_MODEL:base + 3 * D_MODEL])


def _rms_mod(x, g, shift, scale):
    ms = jnp.mean(x * x, axis=-1, keepdims=True)
    return (x * lax.rsqrt(ms + EPS) * g) * (1.0 + scale) + shift


def _rope(x, cos, sin_signed):
    outs = []
    half = HEAD_DIM // 4
    first_half = (_iota((1, LANES), 1) % (2 * half)) < half
    for s in range(x.shape[1] // LANES):
        xs = x[:, s * LANES:(s + 1) * LANES]
        partner = jnp.where(first_half, pltpu.roll(xs, LANES - half, axis=1), pltpu.roll(xs, half, axis=1))
        outs.append(xs * cos + partner * sin_signed)
    return outs[0] if len(outs) == 1 else jnp.concatenate(outs, axis=1)


def _in_proj_kernel(*refs, rope):
    if rope:
        (x_ref, mod_ref, g_ref, wqkv_ref, wdn_ref, wba_ref, wg_ref, gq_ref, gk_ref, cos_ref, sin_ref,
         q_ref, k_ref, v_ref, dqkv_ref, dz_ref, ba_ref, gab_ref) = refs
    else:
        (x_ref, mod_ref, g_ref, wqkv_ref, wdn_ref, wba_ref, wg_ref, gq_ref, gk_ref,
         q_ref, k_ref, v_ref, dqkv_ref, dz_ref, ba_ref, gab_ref) = refs
    shift, scale, _ = _mod_parts(mod_ref, True)
    h = _rms_mod(x_ref[...], g_ref[...], shift, scale).astype(BF16)

    qkv = _dot(h, wqkv_ref[...])
    aq = qkv[:, :ATTN_Q_W]
    ak = qkv[:, ATTN_Q_W:ATTN_Q_W + ATTN_KV_W]
    qn = aq * lax.rsqrt(_head_sumsq(aq, HEAD_DIM) * (1.0 / HEAD_DIM) + EPS) * gq_ref[...]
    kn = ak * lax.rsqrt(_head_sumsq(ak, HEAD_DIM) * (1.0 / HEAD_DIM) + EPS) * gk_ref[...]
    if rope:
        cos, sin = cos_ref[...], sin_ref[...]
        qn = _rope(qn, cos, sin)
        kn = _rope(kn, cos, sin)
    q_ref[...] = (qn * (HEAD_DIM ** -0.5)).astype(BF16)
    k_ref[...] = kn
    v_ref[...] = qkv[:, ATTN_Q_W + ATTN_KV_W:]

    dn = _dot(h, wdn_ref[...])
    dqkv_ref[...] = dn[:, :3 * DN_W]
    dz_ref[...] = dn[:, 3 * DN_W:]
    ba_ref[...] = _dot(h, wba_ref[...])
    gab_ref[...] = _dot(h, wg_ref[...])


def _in_proj(x, mod, group_of_tile, lw, tm, rope_tables=None):
    t = x.shape[0]
    rope = rope_tables is not None
    row = lambda w: pl.BlockSpec((tm, w), lambda i: (i, 0))
    in_specs = [row(D_MODEL),
                pl.BlockSpec((None, 1, 6 * D_MODEL), lambda i: (group_of_tile(i), 0, 0)),
                _resident((1, D_MODEL)),
                _resident(lw["w_qkv"].shape), _resident(lw["w_dn"].shape),
                _resident(lw["w_ba"].shape), _resident(lw["w_g"].shape),
                _resident((1, ATTN_Q_W)), _resident((1, ATTN_KV_W))]
    args = [x, mod, lw["norm1_g"], lw["w_qkv"], lw["w_dn"], lw["w_ba"], lw["w_g"], lw["gq"], lw["gk"]]
    if rope:
        cos, sin = rope_tables
        tiles_per_seq = cos.shape[0] // tm
        in_specs += [pl.BlockSpec((tm, LANES), lambda i: (i % tiles_per_seq, 0))] * 2
        args += [cos, sin]
    widths = (ATTN_Q_W, ATTN_KV_W, ATTN_KV_W, 3 * DN_W, DN_W, LANES, 2 * D_MODEL)
    dtypes = (BF16, F32, F32, F32, F32, F32, F32)
    return pl.pallas_call(
        functools.partial(_in_proj_kernel, rope=rope),
        grid=(t // tm,),
        in_specs=in_specs,
        out_specs=[row(w) for w in widths],
        out_shape=[jax.ShapeDtypeStruct((t, w), dt) for w, dt in zip(widths, dtypes)],
        compiler_params=_params(),
        name="in_proj_rope" if rope else "in_proj",
    )(*args)


def _attn_kernel(*refs, n_parts):
    q_ref = refs[0]
    kv_refs = refs[1:1 + 2 * n_parts]
    o_ref = refs[1 + 2 * n_parts]
    q = q_ref[...]
    tq = q.shape[0]
    outs = [None] * N_Q_HEADS
    for kvh in range(N_KV_HEADS):
        ks = [kv_refs[2 * p][:, kvh * HEAD_DIM:(kvh + 1) * HEAD_DIM].astype(BF16) for p in range(n_parts)]
        vs = [kv_refs[2 * p + 1][:, kvh * HEAD_DIM:(kvh + 1) * HEAD_DIM].astype(BF16) for p in range(n_parts)]
        heads = range(kvh * Q_GROUP, (kvh + 1) * Q_GROUP)
        qs = jnp.concatenate([q[:, j * HEAD_DIM:(j + 1) * HEAD_DIM] for j in heads], axis=0)
        ss = [_dot_nt(qs, k) for k in ks]
        m = functools.reduce(jnp.maximum, [jnp.max(s, axis=-1, keepdims=True) for s in ss])
        ps = [jnp.exp(s - m) for s in ss]
        den = functools.reduce(jnp.add, [jnp.sum(p, axis=-1, keepdims=True) for p in ps])
        acc = functools.reduce(jnp.add, [_dot(p.astype(BF16), v) for p, v in zip(ps, vs)])
        o = acc / den
        for g, j in enumerate(heads):
            outs[j] = o[g * tq:(g + 1) * tq]
    o_ref[...] = jnp.concatenate(outs, axis=1).astype(BF16)


def _attention(q, parts, seq_q, tq):
    t = q.shape[0]
    nq = seq_q // tq
    in_specs = [pl.BlockSpec((tq, ATTN_Q_W), lambda b, i: (b * nq + i, 0))]
    args = [q]
    for k, v, seq_k in parts:
        in_specs += [pl.BlockSpec((seq_k, ATTN_KV_W), lambda b, i: (b, 0))] * 2
        args += [k, v]
    return pl.pallas_call(
        functools.partial(_attn_kernel, n_parts=len(parts)),
        grid=(t // seq_q, nq),
        in_specs=in_specs,
        out_specs=pl.BlockSpec((tq, ATTN_Q_W), lambda b, i: (b * nq + i, 0)),
        out_shape=jax.ShapeDtypeStruct((t, ATTN_Q_W), BF16),
        compiler_params=_params(2),
        name="attention_%dparts" % len(parts),
    )(*args)


def _block_diag(x, mask):
    return jnp.where(mask, jnp.concatenate([x] * HEADS_PER_GROUP, axis=0), jnp.zeros((), x.dtype))


def _packed_mm(x, y, mask):
    m = x.shape[0]
    xh, xl = _split2(x)
    yh, yl = _split2(y)
    r = _dot(jnp.concatenate([xh, xl], axis=0), _block_diag(yh, mask))
    return r[:m] + r[m:] + _dot(xh, _block_diag(yl, mask))


def _packed_unit_inverse(lm, eye, mask):
    n = -lm
    r = eye + n
    p = _packed_mm(n, n, mask)
    steps = int(math.log2(CHUNK)) - 1
    for s in range(steps):
        last = s == steps - 1
        lhs = r if last else jnp.concatenate([r, p], axis=0)
        prod = _packed_mm(lhs, p, mask)
        r = r + prod[:CHUNK]
        if not last:
            p = prod[CHUNK:]
    return r


def _delta_kernel(*refs, seq, has_init):
    if has_init:
        (x_ref, z_ref, ba_ref, cw_ref, alog_ref, dtb_ref, gn_ref, s0f_ref, s0b_ref,
         o_ref, sf_ref, sb_ref, q_s, k_s, v_s, gate_s, o_s, st_s) = refs
    else:
        (x_ref, z_ref, ba_ref, cw_ref, alog_ref, dtb_ref, gn_ref,
         o_ref, sf_ref, sb_ref, q_s, k_s, v_s, gate_s, o_s, st_s) = refs
    n_chunks = seq // CHUNK
    rb = min(seq, 256)
    pad = 8
    half = (DN_CONV - 1) // 2

    expand = ((_iota((LANES, 4 * DN_W), 1) // DN_DK) == _iota((LANES, 4 * DN_W), 0)).astype(BF16)
    lane = _iota((1, LANES), 1)
    for blk in range(seq // rb):
        r0 = blk * rb
        for part, dst in enumerate((q_s, k_s, v_s)):
            cols = slice(part * DN_W, (part + 1) * DN_W)
            xb = x_ref[r0:r0 + rb, cols]
            prev = x_ref[r0 - pad:r0, cols] if r0 > 0 else jnp.zeros((pad, DN_W), F32)
            nxt = x_ref[r0 + rb:r0 + rb + pad, cols] if r0 + rb < seq else jnp.zeros((pad, DN_W), F32)
            xe = jnp.concatenate([prev, xb, nxt], axis=0)
            ne = rb + 2 * pad
            y = jnp.zeros((rb, DN_W), F32)
            for tap in range(DN_CONV):
                d = tap - half
                sh = xe if d == 0 else pltpu.roll(xe, (ne - d) % ne, axis=0)
                y = y + sh[pad:pad + rb] * cw_ref[tap:tap + 1, cols]
            y = _silu(y)
            if part == 0:
                y = y * lax.rsqrt(_head_sumsq(y, DN_DK) + EPS) * (DN_DK ** -0.5)
            elif part == 1:
                y = y * lax.rsqrt(_head_sumsq(y, DN_DK) + EPS)
            dst[r0:r0 + rb, :] = y
        ba = ba_ref[r0:r0 + rb, :]
        decay = -jnp.exp(alog_ref[...]) * _softplus(ba + dtb_ref[...])
        vals = jnp.where(lane < 2 * DN_HEADS, _sigmoid(ba), jnp.where(lane < 4 * DN_HEADS, decay, 0.0))
        gate_s[r0:r0 + rb, :] = _dot_exact_rhs(vals, expand)

    bd_mask = _same_block((GW, GW), DN_DK)
    row = _iota((CHUNK, GW), 0)
    col = _iota((CHUNK, GW), 1) % CHUNK
    eye = (row == col).astype(F32)
    tri_r = _iota((CHUNK, CHUNK), 0)
    tri_c = _iota((CHUNK, CHUNK), 1)
    dirs = (
        (col <= row, col >= row, col < row, (tri_c <= tri_r).astype(BF16), CHUNK - 1),
        (col >= row, col <= row, col > row, (tri_c >= tri_r).astype(BF16), 0),
    )
    for d in range(2):
        for g in range(N_GROUPS):
            lanes = slice(g * GW, (g + 1) * GW)
            if has_init:
                s0 = (s0f_ref, s0b_ref)[d][:, lanes]
                st_s[2 * d + g] = _block_diag(s0, bd_mask)
            else:
                st_s[2 * d + g] = jnp.zeros((GW, GW), F32)

    def chunk_step(n, carry):
        for d in range(2):
            incl, incl_t, strict, cum_mat, last_row = dirs[d]
            c = n if d == 0 else n_chunks - 1 - n
            r0 = pl.multiple_of(c * CHUNK, CHUNK)
            rows = pl.ds(r0, CHUNK)
            for g in range(N_GROUPS):
                lanes = slice(g * GW, (g + 1) * GW)
                beta = gate_s[rows, pl.ds(d * DN_W + g * GW, GW)]
                gr = gate_s[rows, pl.ds(2 * DN_W + d * DN_W + g * GW, GW)]
                gc = _dot_exact_lhs(cum_mat, gr)
                gc_col = jnp.sum(jnp.where(incl_t, gr, 0.0), axis=0, keepdims=True)
                g_last = gc[last_row:last_row + 1, :]
                dec = jnp.where(incl, jnp.exp(jnp.minimum(gc - gc_col, 0.0)), 0.0)
                eg = jnp.exp(gc)
                q = q_s[rows, lanes]
                k = k_s[rows, lanes]
                v = v_s[rows, lanes]
                kb = k * beta
                vb = v * beta
                kbg = kb * eg
                qg = q * eg
                k_bd = _block_diag(k.astype(BF16), bd_mask)
                gram = _dot_nt(jnp.concatenate([kb, q], axis=0).astype(BF16), k_bd)
                lm = jnp.where(strict, gram[:CHUNK] * dec, 0.0)
                a = gram[CHUNK:] * dec
                t = _packed_unit_inverse(lm, eye, bd_mask).astype(BF16)
                u = _dot(t, _block_diag(vb.astype(BF16), bd_mask))
                w = _dot(t, _block_diag(kbg.astype(BF16), bd_mask))
                state = st_s[2 * d + g]
                ws_qs = _dot(jnp.concatenate([w, qg], axis=0).astype(BF16), state.astype(BF16))
                v_new = u - ws_qs[:CHUNK]
                o = ws_qs[CHUNK:] + _dot(a.astype(BF16), _block_diag(v_new.astype(BF16), bd_mask))
                k_dec = k * jnp.exp(g_last - gc)
                upd = _dot_tn(k_dec.astype(BF16), v_new.astype(BF16))
                st_s[2 * d + g] = state * jnp.exp(g_last) + jnp.where(bd_mask, upd, 0.0)
                if d == 0:
                    o_s[0, rows, lanes] = o
                else:
                    o_s[1, rows, lanes] = o
        return carry

    lax.fori_loop(0, n_chunks, chunk_step, 0)

    for blk in range(seq // rb):
        rows = slice(blk * rb, (blk + 1) * rb)
        o = o_s[0, rows, :] + o_s[1, rows, :]
        o = o * lax.rsqrt(_head_sumsq(o, DN_DV) * (1.0 / DN_DV) + EPS) * gn_ref[...]
        o_ref[rows, :] = (o * _silu(z_ref[rows, :])).astype(BF16)
    for d, dst in enumerate((sf_ref, sb_ref)):
        for g in range(N_GROUPS):
            st = st_s[2 * d + g]
            packed = st[0:DN_DK]
            for hb in range(1, HEADS_PER_GROUP):
                packed = packed + st[hb * DN_DK:(hb + 1) * DN_DK]
            dst[:, g * GW:(g + 1) * GW] = packed


def _delta(dqkv, dz, ba, lw, seq, init=None):
    t = dqkv.shape[0]
    nb = t // seq
    has_init = init is not None
    seq_block = lambda w: pl.BlockSpec((seq, w), lambda b: (b, 0))
    state_block = pl.BlockSpec((None, DN_DK, DN_W), lambda b: (b, 0, 0))
    in_specs = [seq_block(3 * DN_W), seq_block(DN_W), seq_block(LANES),
                _resident((DN_CONV, 3 * DN_W)), _resident((1, LANES)), _resident((1, LANES)),
                _resident((1, DN_W))]
    args = [dqkv, dz, ba, lw["conv_w"], lw["a_log"], lw["dt_bias"], lw["gn"]]
    if has_init:
        in_specs += [state_block, state_block]
        args += list(init)
    return pl.pallas_call(
        functools.partial(_delta_kernel, seq=seq, has_init=has_init),
        grid=(nb,),
        in_specs=in_specs,
        out_specs=[seq_block(DN_W), state_block, state_block],
        out_shape=[jax.ShapeDtypeStruct((t, DN_W), BF16),
                   jax.ShapeDtypeStruct((nb, DN_DK, DN_W), F32),
                   jax.ShapeDtypeStruct((nb, DN_DK, DN_W), F32)],
        scratch_shapes=[pltpu.VMEM((seq, DN_W), F32), pltpu.VMEM((seq, DN_W), F32),
                        pltpu.VMEM((seq, DN_W), F32), pltpu.VMEM((seq, 4 * DN_W), F32),
                        pltpu.VMEM((2, seq, DN_W), F32),
                        pltpu.VMEM((2 * N_GROUPS, GW, GW), F32)],
        compiler_params=_params(),
        name="delta_rule_init" if has_init else "delta_rule",
    )(*args)


def _post_kernel(x_ref, oa_ref, od_ref, gab_ref, mod_ref, g2_ref, wpa_ref, wpb_ref, wo_ref,
                 wg_ref, wu_ref, wd_ref, y_ref):
    _, _, gate1 = _mod_parts(mod_ref, True)
    shift2, scale2, gate2 = _mod_parts(mod_ref, False)
    gab = gab_ref[...]
    merged = (_sigmoid(gab[:, :D_MODEL]) * _dot(oa_ref[...], wpa_ref[...])
              + _sigmoid(gab[:, D_MODEL:]) * _dot(od_ref[...], wpb_ref[...]))
    x1 = x_ref[...] + gate1 * _dot(merged.astype(BF16), wo_ref[...])
    h2 = _rms_mod(x1, g2_ref[...], shift2, scale2).astype(BF16)
    act = _silu(_dot(h2, wg_ref[...])) * _dot(h2, wu_ref[...])
    y_ref[...] = x1 + gate2 * _dot(act.astype(BF16), wd_ref[...])


def _post(x, oa, od, gab, mod, group_of_tile, lw, tm):
    t = x.shape[0]
    row = lambda w: pl.BlockSpec((tm, w), lambda i: (i, 0))
    weights = [lw[k] for k in ("w_pa", "w_pb", "w_o", "w_gate", "w_up", "w_down")]
    return pl.pallas_call(
        _post_kernel,
        grid=(t // tm,),
        in_specs=[row(D_MODEL), row(ATTN_Q_W), row(DN_W), row(2 * D_MODEL),
                  pl.BlockSpec((None, 1, 6 * D_MODEL), lambda i: (group_of_tile(i), 0, 0)),
                  _resident((1, D_MODEL))] + [_resident(w.shape) for w in weights],
        out_specs=row(D_MODEL),
        out_shape=jax.ShapeDtypeStruct((t, D_MODEL), F32),
        compiler_params=_params(),
        name="post_block",
    )(x, oa, od, gab, mod, lw["norm2_g"], *weights)


def _rope_tables(n_tokens):
    quarter = HEAD_DIM // 4
    lane = jnp.arange(LANES)
    d = lane % HEAD_DIM
    inv = ROPE_THETA ** (-(d % quarter).astype(F32) / quarter)
    t = jnp.arange(n_tokens)
    pos = jnp.where(d[None, :] < HEAD_DIM // 2, (t // GRID_W)[:, None], (t % GRID_W)[:, None]).astype(F32)
    ang = pos * inv[None, :]
    sign = jnp.where((d % (2 * quarter)) < quarter, -1.0, 1.0)
    return jnp.cos(ang), jnp.sin(ang) * sign[None, :]


def _pack_states(s):
    b = s.shape[0]
    return s.transpose(0, 2, 1, 3).reshape(b, DN_DK, DN_W)


def _unpack_states(s):
    b = s.shape[0]
    return s.reshape(b, DN_DK, DN_HEADS, DN_DV).transpose(0, 2, 1, 3)


def _layer_weights(l, w_in, norm1_g, q_norm_g, k_norm_g, conv_w, a_log, dt_bias, dn_norm_g,
                   w_pa, w_pb, w_o, norm2_g, w_gate, w_up, w_down):
    wi = w_in[l]
    o_dn = ATTN_Q_W + 2 * ATTN_KV_W
    o_ba = o_dn + 4 * DN_W
    o_g = o_ba + 4 * DN_HEADS
    pad_small = lambda a: jnp.pad(a.reshape(1, -1), ((0, 0), (0, LANES - a.size)))
    return dict(
        w_qkv=wi[:, :o_dn].astype(BF16),
        w_dn=wi[:, o_dn:o_ba].astype(BF16),
        w_ba=jnp.pad(wi[:, o_ba:o_g], ((0, 0), (0, LANES - 4 * DN_HEADS))).astype(BF16),
        w_g=wi[:, o_g:].astype(BF16),
        norm1_g=norm1_g[l].reshape(1, D_MODEL),
        norm2_g=norm2_g[l].reshape(1, D_MODEL),
        gq=jnp.tile(q_norm_g[l], N_Q_HEADS).reshape(1, ATTN_Q_W),
        gk=jnp.tile(k_norm_g[l], N_KV_HEADS).reshape(1, ATTN_KV_W),
        conv_w=conv_w[l],
        a_log=pad_small(jnp.concatenate([jnp.zeros((2 * DN_HEADS,), F32), a_log[l].reshape(-1)])),
        dt_bias=pad_small(jnp.concatenate([jnp.zeros((2 * DN_HEADS,), F32), dt_bias[l].reshape(-1)])),
        gn=jnp.tile(dn_norm_g[l], DN_HEADS).reshape(1, DN_W),
        w_pa=w_pa[l].astype(BF16), w_pb=w_pb[l].astype(BF16), w_o=w_o[l].astype(BF16),
        w_gate=w_gate[l].astype(BF16), w_up=w_up[l].astype(BF16), w_down=w_down[l].astype(BF16),
    )


def kernel(x_prompt, x_sample, cache_k, cache_v, state_fwd, state_bwd, c, c_ctx, w_ada, b_ada, norm1_g, w_in,
           q_norm_g, k_norm_g, conv_w, a_log, dt_bias, dn_norm_g, w_pa, w_pb, w_o, norm2_g, w_gate, w_up, w_down):
    batch, seq, _ = x_prompt.shape
    dec_batch, dec_seq, _ = x_sample.shape
    depth = w_in.shape[0]
    past = cache_k.shape[2]
    assert dec_batch + 1 <= 8 and seq % CHUNK == 0 and dec_seq % CHUNK == 0

    cvecs = jnp.zeros((8, D_MODEL), F32).at[0].set(c_ctx).at[1:1 + dec_batch].set(c)
    rope_tables = _rope_tables(dec_seq)
    tm_ctx = 256
    tm_lat = 256
    lat_tiles_per_seq = dec_seq // tm_lat

    yp = x_prompt.reshape(batch * seq, D_MODEL)
    ys = x_sample.reshape(dec_batch * dec_seq, D_MODEL)
    ks_out, vs_out, sf_out, sb_out = [], [], [], []
    for l in range(depth):
        lw = _layer_weights(l, w_in, norm1_g, q_norm_g, k_norm_g, conv_w, a_log, dt_bias, dn_norm_g,
                            w_pa, w_pb, w_o, norm2_g, w_gate, w_up, w_down)
        mod = _modulation(cvecs, w_ada[l], b_ada[l])[:1 + dec_batch].reshape(1 + dec_batch, 1, 6 * D_MODEL)

        ctx_group = lambda i: 0
        q, kn, v, dqkv, dz, ba, gab = _in_proj(yp, mod, ctx_group, lw, tm_ctx)
        oa = _attention(q, [(kn, v, seq)], seq, seq)
        od, sf, sb = _delta(dqkv, dz, ba, lw, seq)
        yp = _post(yp, oa, od, gab, mod, ctx_group, lw, tm_ctx)
        ks_out.append(kn.reshape(batch, seq, N_KV_HEADS, HEAD_DIM))
        vs_out.append(v.reshape(batch, seq, N_KV_HEADS, HEAD_DIM))
        sf_out.append(_unpack_states(sf))
        sb_out.append(_unpack_states(sb))

        lat_group = lambda i: 1 + i // lat_tiles_per_seq
        q, kr, v, dqkv, dz, ba, gab = _in_proj(ys, mod, lat_group, lw, tm_lat, rope_tables)
        ck = cache_k[:, l].reshape(dec_batch * past, ATTN_KV_W)
        cv = cache_v[:, l].reshape(dec_batch * past, ATTN_KV_W)
        oa = _attention(q, [(ck, cv, past), (kr, v, dec_seq)], dec_seq, 256)
        init = (_pack_states(state_fwd[:, l]), _pack_states(state_bwd[:, l]))
        od, _, _ = _delta(dqkv, dz, ba, lw, dec_seq, init)
        ys = _post(ys, oa, od, gab, mod, lat_group, lw, tm_lat)

    return (yp.reshape(batch, seq, D_MODEL), ys.reshape(dec_batch, dec_seq, D_MODEL),
            jnp.stack(ks_out, axis=1), jnp.stack(vs_out, axis=1),
            jnp.stack(sf_out, axis=1), jnp.stack(sb_out, axis=1))
```

```python
import functools
import math

import jax
import jax.numpy as jnp
from jax import lax
from jax.experimental import pallas as pl
from jax.experimental.pallas import tpu as pltpu

D_MODEL = 1024
GRID_W = 64
HEAD_DIM = 64
N_Q_HEADS = 8
N_KV_HEADS = 2
Q_GROUP = N_Q_HEADS // N_KV_HEADS
ATTN_Q_W = N_Q_HEADS * HEAD_DIM
ATTN_KV_W = N_KV_HEADS * HEAD_DIM
ROPE_THETA = 10000.0
DN_HEADS = 8
DN_DK = 64
DN_DV = 64
DN_W = DN_HEADS * DN_DK
DN_CONV = 5
CHUNK = 64
EPS = 1e-6

LANES = 128
MXU_DIM = 256
VMEM_LIMIT_BYTES = 56 * 1024 * 1024

HEADS_PER_GROUP = MXU_DIM // DN_DK
N_GROUPS = DN_HEADS // HEADS_PER_GROUP
GW = HEADS_PER_GROUP * DN_DK

F32 = jnp.float32
BF16 = jnp.bfloat16


def _dot(a, b):
    return jnp.dot(a, b, preferred_element_type=F32)


def _dot_nt(a, b):
    return lax.dot_general(a, b, (((1,), (1,)), ((), ())), preferred_element_type=F32)


def _dot_tn(a, b):
    return lax.dot_general(a, b, (((0,), (0,)), ((), ())), preferred_element_type=F32)


def _split2(x):
    hi = x.astype(BF16)
    lo = (x - hi.astype(F32)).astype(BF16)
    return hi, lo


def _split3(x):
    hi = x.astype(BF16)
    r = x - hi.astype(F32)
    mid = r.astype(BF16)
    lo = (r - mid.astype(F32)).astype(BF16)
    return hi, mid, lo


def _dot_x3(a, b):
    m = a.shape[0]
    ah, al = _split2(a)
    bh, bl = _split2(b)
    r = _dot(jnp.concatenate([ah, al], axis=0), bh)
    return r[:m] + r[m:] + _dot(ah, bl)


def _dot_exact_rhs(a, b01):
    m = a.shape[0]
    a1, a2, a3 = _split3(a)
    r = _dot(jnp.concatenate([a1, a2, a3], axis=0), b01)
    return r[:m] + r[m:2 * m] + r[2 * m:]


def _dot_exact_lhs(a01, b):
    n = b.shape[1]
    b1, b2, b3 = _split3(b)
    r = _dot(a01, jnp.concatenate([b1, b2, b3], axis=1))
    return r[:, :n] + r[:, n:2 * n] + r[:, 2 * n:]


def _iota(shape, dim):
    return lax.broadcasted_iota(jnp.int32, shape, dim)


def _same_block(shape, width):
    return (_iota(shape, 0) // width) == (_iota(shape, 1) // width)


def _head_sumsq(x, width):
    n = x.shape[1]
    sel = _same_block((n, n), width).astype(BF16)
    sq = x * x
    hi, lo = _split2(sq)
    m = x.shape[0]
    r = _dot(jnp.concatenate([hi, lo], axis=0), sel)
    return r[:m] + r[m:]


def _sigmoid(x):
    return 1.0 / (1.0 + jnp.exp(-x))


def _silu(x):
    return x * _sigmoid(x)


def _softplus(x):
    return jnp.maximum(x, 0.0) + jnp.log(1.0 + jnp.exp(-jnp.abs(x)))


def _resident(shape):
    nd = len(shape)
    return pl.BlockSpec(shape, lambda *_: (0,) * nd, pipeline_mode=pl.Buffered(1))


def _params(n_axes=1):
    return pltpu.CompilerParams(dimension_semantics=("arbitrary",) * n_axes,
                                vmem_limit_bytes=VMEM_LIMIT_BYTES)


def _mod_kernel(c_ref, w_ref, b_ref, o_ref):
    c = c_ref[...]
    o_ref[...] = _dot_x3(_silu(c), w_ref[...]) + b_ref[...]


def _modulation(cvecs, w_ada, b_ada):
    n = w_ada.shape[1]
    tn = 1024
    return pl.pallas_call(
        _mod_kernel,
        grid=(n // tn,),
        in_specs=[pl.BlockSpec((8, D_MODEL), lambda j: (0, 0)),
                  pl.BlockSpec((D_MODEL, tn), lambda j: (0, j)),
                  pl.BlockSpec((1, tn), lambda j: (0, j))],
        out_specs=pl.BlockSpec((8, tn), lambda j: (0, j)),
        out_shape=jax.ShapeDtypeStruct((8, n), F32),
        compiler_params=_params(),
        name="adaln_modulation",
    )(cvecs, w_ada, b_ada.reshape(1, n))


def _mod_parts(mod_ref, first):
    m = mod_ref[...]
    base = 0 if first else 3 * D_MODEL
    return (m[:, base:base + D_MODEL], m[:, base + D_MODEL:base + 2 * D_MODEL],
            m[:, base + 2 * D_MODEL:base + 3 * D_MODEL])


def _rms_mod(x, g, shift, scale):
    ms = jnp.mean(x * x, axis=-1, keepdims=True)
    return (x * lax.rsqrt(ms + EPS) * g) * (1.0 + scale) + shift


def _rope(x, cos, sin_signed):
    outs = []
    half = HEAD_DIM // 4
    first_half = (_iota((1, LANES), 1) % (2 * half)) < half
    for s in range(x.shape[1] // LANES):
        xs = x[:, s * LANES:(s + 1) * LANES]
        partner = jnp.where(first_half, pltpu.roll(xs, LANES - half, axis=1), pltpu.roll(xs, half, axis=1))
        outs.append(xs * cos + partner * sin_signed)
    return outs[0] if len(outs) == 1 else jnp.concatenate(outs, axis=1)


def _in_proj_kernel(*refs, rope):
    if rope:
        (x_ref, mod_ref, g_ref, wqkv_ref, wdn_ref, wba_ref, wg_ref, gq_ref, gk_ref, cos_ref, sin_ref,
         q_ref, k_ref, v_ref, dqkv_ref, dz_ref, ba_ref, gab_ref) = refs
    else:
        (x_ref, mod_ref, g_ref, wqkv_ref, wdn_ref, wba_ref, wg_ref, gq_ref, gk_ref,
         q_ref, k_ref, v_ref, dqkv_ref, dz_ref, ba_ref, gab_ref) = refs
    shift, scale, _ = _mod_parts(mod_ref, True)
    h = _rms_mod(x_ref[...], g_ref[...], shift, scale).astype(BF16)

    qkv = _dot(h, wqkv_ref[...])
    aq = qkv[:, :ATTN_Q_W]
    ak = qkv[:, ATTN_Q_W:ATTN_Q_W + ATTN_KV_W]
    qn = aq * lax.rsqrt(_head_sumsq(aq, HEAD_DIM) * (1.0 / HEAD_DIM) + EPS) * gq_ref[...]
    kn = ak * lax.rsqrt(_head_sumsq(ak, HEAD_DIM) * (1.0 / HEAD_DIM) + EPS) * gk_ref[...]
    if rope:
        cos, sin = cos_ref[...], sin_ref[...]
        qn = _rope(qn, cos, sin)
        kn = _rope(kn, cos, sin)
    q_ref[...] = (qn * (HEAD_DIM ** -0.5)).astype(BF16)
    k_ref[...] = kn
    v_ref[...] = qkv[:, ATTN_Q_W + ATTN_KV_W:]

    dn = _dot(h, wdn_ref[...])
    dqkv_ref[...] = dn[:, :3 * DN_W]
    dz_ref[...] = dn[:, 3 * DN_W:]
    ba_ref[...] = _dot(h, wba_ref[...])
    gab_ref[...] = _dot(h, wg_ref[...])


def _in_proj(x, mod, group_of_tile, lw, tm, rope_tables=None):
    t = x.shape[0]
    rope = rope_tables is not None
    row = lambda w: pl.BlockSpec((tm, w), lambda i: (i, 0))
    in_specs = [row(D_MODEL),
                pl.BlockSpec((None, 1, 6 * D_MODEL), lambda i: (group_of_tile(i), 0, 0)),
                _resident((1, D_MODEL)),
                _resident(lw["w_qkv"].shape), _resident(lw["w_dn"].shape),
                _resident(lw["w_ba"].shape), _resident(lw["w_g"].shape),
                _resident((1, ATTN_Q_W)), _resident((1, ATTN_KV_W))]
    args = [x, mod, lw["norm1_g"], lw["w_qkv"], lw["w_dn"], lw["w_ba"], lw["w_g"], lw["gq"], lw["gk"]]
    if rope:
        cos, sin = rope_tables
        tiles_per_seq = cos.shape[0] // tm
        in_specs += [pl.BlockSpec((tm, LANES), lambda i: (i % tiles_per_seq, 0))] * 2
        args += [cos, sin]
    widths = (ATTN_Q_W, ATTN_KV_W, ATTN_KV_W, 3 * DN_W, DN_W, LANES, 2 * D_MODEL)
    dtypes = (BF16, F32, F32, F32, F32, F32, F32)
    return pl.pallas_call(
        functools.partial(_in_proj_kernel, rope=rope),
        grid=(t // tm,),
        in_specs=in_specs,
        out_specs=[row(w) for w in widths],
        out_shape=[jax.ShapeDtypeStruct((t, w), dt) for w, dt in zip(widths, dtypes)],
        compiler_params=_params(),
        name="in_proj_rope" if rope else "in_proj",
    )(*args)


def _attn_kernel(*refs, n_parts):
    q_ref = refs[0]
    kv_refs = refs[1:1 + 2 * n_parts]
    o_ref = refs[1 + 2 * n_parts]
    q = q_ref[...]
    tq = q.shape[0]
    outs = [None] * N_Q_HEADS
    for kvh in range(N_KV_HEADS):
        ks = [kv_refs[2 * p][:, kvh * HEAD_DIM:(kvh + 1) * HEAD_DIM].astype(BF16) for p in range(n_parts)]
        vs = [kv_refs[2 * p + 1][:, kvh * HEAD_DIM:(kvh + 1) * HEAD_DIM].astype(BF16) for p in range(n_parts)]
        heads = range(kvh * Q_GROUP, (kvh + 1) * Q_GROUP)
        qs = jnp.concatenate([q[:, j * HEAD_DIM:(j + 1) * HEAD_DIM] for j in heads], axis=0)
        ss = [_dot_nt(qs, k) for k in ks]
        m = functools.reduce(jnp.maximum, [jnp.max(s, axis=-1, keepdims=True) for s in ss])
        ps = [jnp.exp(s - m) for s in ss]
        den = functools.reduce(jnp.add, [jnp.sum(p, axis=-1, keepdims=True) for p in ps])
        acc = functools.reduce(jnp.add, [_dot(p.astype(BF16), v) for p, v in zip(ps, vs)])
        o = acc / den
        for g, j in enumerate(heads):
            outs[j] = o[g * tq:(g + 1) * tq]
    o_ref[...] = jnp.concatenate(outs, axis=1).astype(BF16)


def _attention(q, parts, seq_q, tq):
    t = q.shape[0]
    nq = seq_q // tq
    in_specs = [pl.BlockSpec((tq, ATTN_Q_W), lambda b, i: (b * nq + i, 0))]
    args = [q]
    for k, v, seq_k in parts:
        in_specs += [pl.BlockSpec((seq_k, ATTN_KV_W), lambda b, i: (b, 0))] * 2
        args += [k, v]
    return pl.pallas_call(
        functools.partial(_attn_kernel, n_parts=len(parts)),
        grid=(t // seq_q, nq),
        in_specs=in_specs,
        out_specs=pl.BlockSpec((tq, ATTN_Q_W), lambda b, i: (b * nq + i, 0)),
        out_shape=jax.ShapeDtypeStruct((t, ATTN_Q_W), BF16),
        compiler_params=_params(2),
        name="attention_%dparts" % len(parts),
    )(*args)


def _block_diag(x, mask):
    return jnp.where(mask, jnp.concatenate([x] * HEADS_PER_GROUP, axis=0), jnp.zeros((), x.dtype))


def _packed_mm(x, y, mask):
    m = x.shape[0]
    xh, xl = _split2(x)
    yh, yl = _split2(y)
    r = _dot(jnp.concatenate([xh, xl], axis=0), _block_diag(yh, mask))
    return r[:m] + r[m:] + _dot(xh, _block_diag(yl, mask))


def _packed_unit_inverse(lms, eye, mask):
    ns = [-lm for lm in lms]
    rs = [eye + n for n in ns]
    ps = [_packed_mm(n, n, mask) for n in ns]
    steps = int(math.log2(CHUNK)) - 1
    for s in range(steps):
        last = s == steps - 1
        prods = [_packed_mm(r if last else jnp.concatenate([r, p], axis=0), p, mask) for r, p in zip(rs, ps)]
        rs = [r + prod[:CHUNK] for r, prod in zip(rs, prods)]
        if not last:
            ps = [prod[CHUNK:] for prod in prods]
    return rs


def _delta_kernel(*refs, seq, has_init):
    if has_init:
        (x_ref, z_ref, ba_ref, cw_ref, alog_ref, dtb_ref, gn_ref, s0f_ref, s0b_ref,
         o_ref, sf_ref, sb_ref, q_s, k_s, v_s, gate_s, o_s, st_s) = refs
    else:
        (x_ref, z_ref, ba_ref, cw_ref, alog_ref, dtb_ref, gn_ref,
         o_ref, sf_ref, sb_ref, q_s, k_s, v_s, gate_s, o_s, st_s) = refs
    n_chunks = seq // CHUNK
    rb = min(seq, 256)
    pad = 8
    half = (DN_CONV - 1) // 2

    expand = ((_iota((LANES, 4 * DN_W), 1) // DN_DK) == _iota((LANES, 4 * DN_W), 0)).astype(BF16)
    lane = _iota((1, LANES), 1)
    blk_r = _iota((rb, rb), 0)
    blk_c = _iota((rb, rb), 1)
    same_chunk = (blk_r // CHUNK) == (blk_c // CHUNK)
    cum_f = (same_chunk & (blk_c <= blk_r)).astype(BF16)
    cum_b = (same_chunk & (blk_c >= blk_r)).astype(BF16)
    for blk in range(seq // rb):
        r0 = blk * rb
        for part, dst in enumerate((q_s, k_s, v_s)):
            cols = slice(part * DN_W, (part + 1) * DN_W)
            xb = x_ref[r0:r0 + rb, cols]
            prev = x_ref[r0 - pad:r0, cols] if r0 > 0 else jnp.zeros((pad, DN_W), F32)
            nxt = x_ref[r0 + rb:r0 + rb + pad, cols] if r0 + rb < seq else jnp.zeros((pad, DN_W), F32)
            xe = jnp.concatenate([prev, xb, nxt], axis=0)
            ne = rb + 2 * pad
            y = jnp.zeros((rb, DN_W), F32)
            for tap in range(DN_CONV):
                d = tap - half
                sh = xe if d == 0 else pltpu.roll(xe, (ne - d) % ne, axis=0)
                y = y + sh[pad:pad + rb] * cw_ref[tap:tap + 1, cols]
            y = _silu(y)
            if part == 0:
                y = y * lax.rsqrt(_head_sumsq(y, DN_DK) + EPS) * (DN_DK ** -0.5)
            elif part == 1:
                y = y * lax.rsqrt(_head_sumsq(y, DN_DK) + EPS)
            dst[r0:r0 + rb, :] = y
        ba = ba_ref[r0:r0 + rb, :]
        decay = -jnp.exp(alog_ref[...]) * _softplus(ba + dtb_ref[...])
        vals = jnp.where(lane < 2 * DN_HEADS, _sigmoid(ba), jnp.where(lane < 4 * DN_HEADS, decay, 0.0))
        ex = _dot_exact_rhs(vals, expand)
        gate_s[r0:r0 + rb, 0:2 * DN_W] = ex[:, :2 * DN_W]
        gate_s[r0:r0 + rb, 2 * DN_W:3 * DN_W] = _dot_exact_lhs(cum_f, ex[:, 2 * DN_W:3 * DN_W])
        gate_s[r0:r0 + rb, 3 * DN_W:4 * DN_W] = _dot_exact_lhs(cum_b, ex[:, 3 * DN_W:])

    bd_mask = _same_block((GW, GW), DN_DK)
    row = _iota((CHUNK, GW), 0)
    col = _iota((CHUNK, GW), 1) % CHUNK
    diag = row == col
    eye = diag.astype(F32)
    dirs = ((col <= row, col < row, CHUNK - 1), (col >= row, col > row, 0))
    chains = [(d, g) for d in range(2) for g in range(N_GROUPS)]
    for d, g in chains:
        if has_init:
            s0 = (s0f_ref, s0b_ref)[d][:, g * GW:(g + 1) * GW]
            st_s[2 * d + g] = _block_diag(s0, bd_mask)
        else:
            st_s[2 * d + g] = jnp.zeros((GW, GW), F32)

    def bd(x):
        return _block_diag(x.astype(BF16), bd_mask)

    def chunk_step(n, carry):
        ins = []
        for d, g in chains:
            incl, strict, last_row = dirs[d]
            c = n if d == 0 else n_chunks - 1 - n
            rows = pl.ds(pl.multiple_of(c * CHUNK, CHUNK), CHUNK)
            lanes = slice(g * GW, (g + 1) * GW)
            beta = gate_s[rows, pl.ds(d * DN_W + g * GW, GW)]
            gc = gate_s[rows, pl.ds(2 * DN_W + d * DN_W + g * GW, GW)]
            gc_col = jnp.sum(jnp.where(diag, gc, 0.0), axis=0, keepdims=True)
            g_last = gc[last_row:last_row + 1, :]
            dec = jnp.where(incl, jnp.exp(jnp.minimum(gc - gc_col, 0.0)), 0.0)
            eg = jnp.exp(gc)
            q = q_s[rows, lanes]
            k = k_s[rows, lanes]
            kb = k * beta
            ins.append(dict(rows=rows, lanes=lanes, strict=strict, dec=dec, g_last=g_last, k=k, kb=kb, q=q,
                            vb=v_s[rows, lanes] * beta, kbg=kb * eg, qg=q * eg,
                            k_dec=k * jnp.exp(g_last - gc)))
        grams = [_dot_nt(jnp.concatenate([i["kb"], i["q"]], axis=0).astype(BF16), bd(i["k"])) for i in ins]
        lms = [jnp.where(i["strict"], gm[:CHUNK] * i["dec"], 0.0) for i, gm in zip(ins, grams)]
        attn = [gm[CHUNK:] * i["dec"] for i, gm in zip(ins, grams)]
        ts = [t.astype(BF16) for t in _packed_unit_inverse(lms, eye, bd_mask)]
        us = [_dot(t, bd(i["vb"])) for i, t in zip(ins, ts)]
        ws = [_dot(t, bd(i["kbg"])) for i, t in zip(ins, ts)]
        states = [st_s[2 * d + g] for d, g in chains]
        ws_qs = [_dot(jnp.concatenate([w, i["qg"]], axis=0).astype(BF16), s.astype(BF16))
                 for i, w, s in zip(ins, ws, states)]
        v_news = [u - x[:CHUNK] for u, x in zip(us, ws_qs)]
        outs = [x[CHUNK:] + _dot(a.astype(BF16), bd(vn)) for x, a, vn in zip(ws_qs, attn, v_news)]
        upds = [_dot_tn(i["k_dec"].astype(BF16), vn.astype(BF16)) for i, vn in zip(ins, v_news)]
        for (d, g), i, s, upd, o in zip(chains, ins, states, upds, outs):
            st_s[2 * d + g] = s * jnp.exp(i["g_last"]) + jnp.where(bd_mask, upd, 0.0)
            o_s[d, i["rows"], i["lanes"]] = o
        return carry

    lax.fori_loop(0, n_chunks, chunk_step, 0)

    for blk in range(seq // rb):
        rows = slice(blk * rb, (blk + 1) * rb)
        o = o_s[0, rows, :] + o_s[1, rows, :]
        o = o * lax.rsqrt(_head_sumsq(o, DN_DV) * (1.0 / DN_DV) + EPS) * gn_ref[...]
        o_ref[rows, :] = (o * _silu(z_ref[rows, :])).astype(BF16)
    for d, dst in enumerate((sf_ref, sb_ref)):
        for g in range(N_GROUPS):
            st = st_s[2 * d + g]
            packed = st[0:DN_DK]
            for hb in range(1, HEADS_PER_GROUP):
                packed = packed + st[hb * DN_DK:(hb + 1) * DN_DK]
            dst[:, g * GW:(g + 1) * GW] = packed


def _delta(dqkv, dz, ba, lw, seq, init=None):
    t = dqkv.shape[0]
    nb = t // seq
    has_init = init is not None
    seq_block = lambda w: pl.BlockSpec((seq, w), lambda b: (b, 0))
    state_block = pl.BlockSpec((None, DN_DK, DN_W), lambda b: (b, 0, 0))
    in_specs = [seq_block(3 * DN_W), seq_block(DN_W), seq_block(LANES),
                _resident((DN_CONV, 3 * DN_W)), _resident((1, LANES)), _resident((1, LANES)),
                _resident((1, DN_W))]
    args = [dqkv, dz, ba, lw["conv_w"], lw["a_log"], lw["dt_bias"], lw["gn"]]
    if has_init:
        in_specs += [state_block, state_block]
        args += list(init)
    return pl.pallas_call(
        functools.partial(_delta_kernel, seq=seq, has_init=has_init),
        grid=(nb,),
        in_specs=in_specs,
        out_specs=[seq_block(DN_W), state_block, state_block],
        out_shape=[jax.ShapeDtypeStruct((t, DN_W), BF16),
                   jax.ShapeDtypeStruct((nb, DN_DK, DN_W), F32),
                   jax.ShapeDtypeStruct((nb, DN_DK, DN_W), F32)],
        scratch_shapes=[pltpu.VMEM((seq, DN_W), F32), pltpu.VMEM((seq, DN_W), F32),
                        pltpu.VMEM((seq, DN_W), F32), pltpu.VMEM((seq, 4 * DN_W), F32),
                        pltpu.VMEM((2, seq, DN_W), F32),
                        pltpu.VMEM((2 * N_GROUPS, GW, GW), F32)],
        compiler_params=_params(),
        name="delta_rule_init" if has_init else "delta_rule",
    )(*args)


def _post_kernel(x_ref, oa_ref, od_ref, gab_ref, mod_ref, g2_ref, wpa_ref, wpb_ref, wo_ref,
                 wg_ref, wu_ref, wd_ref, y_ref):
    _, _, gate1 = _mod_parts(mod_ref, True)
    shift2, scale2, gate2 = _mod_parts(mod_ref, False)
    gab = gab_ref[...]
    merged = (_sigmoid(gab[:, :D_MODEL]) * _dot(oa_ref[...], wpa_ref[...])
              + _sigmoid(gab[:, D_MODEL:]) * _dot(od_ref[...], wpb_ref[...]))
    x1 = x_ref[...] + gate1 * _dot(merged.astype(BF16), wo_ref[...])
    h2 = _rms_mod(x1, g2_ref[...], shift2, scale2).astype(BF16)
    act = _silu(_dot(h2, wg_ref[...])) * _dot(h2, wu_ref[...])
    y_ref[...] = x1 + gate2 * _dot(act.astype(BF16), wd_ref[...])


def _post(x, oa, od, gab, mod, group_of_tile, lw, tm):
    t = x.shape[0]
    row = lambda w: pl.BlockSpec((tm, w), lambda i: (i, 0))
    weights = [lw[k] for k in ("w_pa", "w_pb", "w_o", "w_gate", "w_up", "w_down")]
    return pl.pallas_call(
        _post_kernel,
        grid=(t // tm,),
        in_specs=[row(D_MODEL), row(ATTN_Q_W), row(DN_W), row(2 * D_MODEL),
                  pl.BlockSpec((None, 1, 6 * D_MODEL), lambda i: (group_of_tile(i), 0, 0)),
                  _resident((1, D_MODEL))] + [_resident(w.shape) for w in weights],
        out_specs=row(D_MODEL),
        out_shape=jax.ShapeDtypeStruct((t, D_MODEL), F32),
        compiler_params=_params(),
        name="post_block",
    )(x, oa, od, gab, mod, lw["norm2_g"], *weights)


def _rope_tables(n_tokens):
    quarter = HEAD_DIM // 4
    lane = jnp.arange(LANES)
    d = lane % HEAD_DIM
    inv = ROPE_THETA ** (-(d % quarter).astype(F32) / quarter)
    t = jnp.arange(n_tokens)
    pos = jnp.where(d[None, :] < HEAD_DIM // 2, (t // GRID_W)[:, None], (t % GRID_W)[:, None]).astype(F32)
    ang = pos * inv[None, :]
    sign = jnp.where((d % (2 * quarter)) < quarter, -1.0, 1.0)
    return jnp.cos(ang), jnp.sin(ang) * sign[None, :]


def _pack_states(s):
    b = s.shape[0]
    return s.transpose(0, 2, 1, 3).reshape(b, DN_DK, DN_W)


def _unpack_states(s):
    b = s.shape[0]
    return s.reshape(b, DN_DK, DN_HEADS, DN_DV).transpose(0, 2, 1, 3)


def _layer_weights(l, w_in, norm1_g, q_norm_g, k_norm_g, conv_w, a_log, dt_bias, dn_norm_g,
                   w_pa, w_pb, w_o, norm2_g, w_gate, w_up, w_down):
    wi = w_in[l]
    o_dn = ATTN_Q_W + 2 * ATTN_KV_W
    o_ba = o_dn + 4 * DN_W
    o_g = o_ba + 4 * DN_HEADS
    pad_small = lambda a: jnp.pad(a.reshape(1, -1), ((0, 0), (0, LANES - a.size)))
    return dict(
        w_qkv=wi[:, :o_dn].astype(BF16),
        w_dn=wi[:, o_dn:o_ba].astype(BF16),
        w_ba=jnp.pad(wi[:, o_ba:o_g], ((0, 0), (0, LANES - 4 * DN_HEADS))).astype(BF16),
        w_g=wi[:, o_g:].astype(BF16),
        norm1_g=norm1_g[l].reshape(1, D_MODEL),
        norm2_g=norm2_g[l].reshape(1, D_MODEL),
        gq=jnp.tile(q_norm_g[l], N_Q_HEADS).reshape(1, ATTN_Q_W),
        gk=jnp.tile(k_norm_g[l], N_KV_HEADS).reshape(1, ATTN_KV_W),
        conv_w=conv_w[l],
        a_log=pad_small(jnp.concatenate([jnp.zeros((2 * DN_HEADS,), F32), a_log[l].reshape(-1)])),
        dt_bias=pad_small(jnp.concatenate([jnp.zeros((2 * DN_HEADS,), F32), dt_bias[l].reshape(-1)])),
        gn=jnp.tile(dn_norm_g[l], DN_HEADS).reshape(1, DN_W),
        w_pa=w_pa[l].astype(BF16), w_pb=w_pb[l].astype(BF16), w_o=w_o[l].astype(BF16),
        w_gate=w_gate[l].astype(BF16), w_up=w_up[l].astype(BF16), w_down=w_down[l].astype(BF16),
    )


def kernel(x_prompt, x_sample, cache_k, cache_v, state_fwd, state_bwd, c, c_ctx, w_ada, b_ada, norm1_g, w_in,
           q_norm_g, k_norm_g, conv_w, a_log, dt_bias, dn_norm_g, w_pa, w_pb, w_o, norm2_g, w_gate, w_up, w_down):
    batch, seq, _ = x_prompt.shape
    dec_batch, dec_seq, _ = x_sample.shape
    depth = w_in.shape[0]
    past = cache_k.shape[2]
    assert dec_batch + 1 <= 8 and seq % CHUNK == 0 and dec_seq % CHUNK == 0

    cvecs = jnp.zeros((8, D_MODEL), F32).at[0].set(c_ctx).at[1:1 + dec_batch].set(c)
    rope_tables = _rope_tables(dec_seq)
    tm_ctx = 256
    tm_lat = 256
    lat_tiles_per_seq = dec_seq // tm_lat

    yp = x_prompt.reshape(batch * seq, D_MODEL)
    ys = x_sample.reshape(dec_batch * dec_seq, D_MODEL)
    ks_out, vs_out, sf_out, sb_out = [], [], [], []
    for l in range(depth):
        lw = _layer_weights(l, w_in, norm1_g, q_norm_g, k_norm_g, conv_w, a_log, dt_bias, dn_norm_g,
                            w_pa, w_pb, w_o, norm2_g, w_gate, w_up, w_down)
        mod = _modulation(cvecs, w_ada[l], b_ada[l])[:1 + dec_batch].reshape(1 + dec_batch, 1, 6 * D_MODEL)

        ctx_group = lambda i: 0
        q, kn, v, dqkv, dz, ba, gab = _in_proj(yp, mod, ctx_group, lw, tm_ctx)
        oa = _attention(q, [(kn, v, seq)], seq, seq)
        od, sf, sb = _delta(dqkv, dz, ba, lw, seq)
        yp = _post(yp, oa, od, gab, mod, ctx_group, lw, tm_ctx)
        ks_out.append(kn.reshape(batch, seq, N_KV_HEADS, HEAD_DIM))
        vs_out.append(v.reshape(batch, seq, N_KV_HEADS, HEAD_DIM))
        sf_out.append(_unpack_states(sf))
        sb_out.append(_unpack_states(sb))

        lat_group = lambda i: 1 + i // lat_tiles_per_seq
        q, kr, v, dqkv, dz, ba, gab = _in_proj(ys, mod, lat_group, lw, tm_lat, rope_tables)
        ck = cache_k[:, l].reshape(dec_batch * past, ATTN_KV_W)
        cv = cache_v[:, l].reshape(dec_batch * past, ATTN_KV_W)
        oa = _attention(q, [(ck, cv, past), (kr, v, dec_seq)], dec_seq, 256)
        init = (_pack_states(state_fwd[:, l]), _pack_states(state_bwd[:, l]))
        od, _, _ = _delta(dqkv, dz, ba, lw, dec_seq, init)
        ys = _post(ys, oa, od, gab, mod, lat_group, lw, tm_lat)

    return (yp.reshape(batch, seq, D_MODEL), ys.reshape(dec_batch, dec_seq, D_MODEL),
            jnp.stack(ks_out, axis=1), jnp.stack(vs_out, axis=1),
            jnp.stack(sf_out, axis=1), jnp.stack(sb_out, axis=1))
```

```python
import functools
import math

import jax
import jax.numpy as jnp
from jax import lax
from jax.experimental import pallas as pl
from jax.experimental.pallas import tpu as pltpu

D_MODEL = 1024
GRID_W = 64
HEAD_DIM = 64
N_Q_HEADS = 8
N_KV_HEADS = 2
Q_GROUP = N_Q_HEADS // N_KV_HEADS
ATTN_Q_W = N_Q_HEADS * HEAD_DIM
ATTN_KV_W = N_KV_HEADS * HEAD_DIM
ROPE_THETA = 10000.0
DN_HEADS = 8
DN_DK = 64
DN_DV = 64
DN_W = DN_HEADS * DN_DK
DN_CONV = 5
CHUNK = 64
EPS = 1e-6

LANES = 128
MXU_DIM = 256
VMEM_LIMIT_BYTES = 56 * 1024 * 1024

HEADS_PER_GROUP = MXU_DIM // DN_DK
N_GROUPS = DN_HEADS // HEADS_PER_GROUP
GW = HEADS_PER_GROUP * DN_DK

F32 = jnp.float32
BF16 = jnp.bfloat16


def _dot(a, b):
    return jnp.dot(a, b, preferred_element_type=F32)


def _dot_nt(a, b):
    return lax.dot_general(a, b, (((1,), (1,)), ((), ())), preferred_element_type=F32)


def _dot_tn(a, b):
    return lax.dot_general(a, b, (((0,), (0,)), ((), ())), preferred_element_type=F32)


def _split2(x):
    hi = x.astype(BF16)
    lo = (x - hi.astype(F32)).astype(BF16)
    return hi, lo


def _split3(x):
    hi = x.astype(BF16)
    r = x - hi.astype(F32)
    mid = r.astype(BF16)
    lo = (r - mid.astype(F32)).astype(BF16)
    return hi, mid, lo


def _dot_x3(a, b):
    m = a.shape[0]
    ah, al = _split2(a)
    bh, bl = _split2(b)
    r = _dot(jnp.concatenate([ah, al], axis=0), bh)
    return r[:m] + r[m:] + _dot(ah, bl)


def _dot_exact_rhs(a, b01):
    m = a.shape[0]
    a1, a2, a3 = _split3(a)
    r = _dot(jnp.concatenate([a1, a2, a3], axis=0), b01)
    return r[:m] + r[m:2 * m] + r[2 * m:]


def _dot_exact_lhs(a01, b):
    n = b.shape[1]
    b1, b2, b3 = _split3(b)
    r = _dot(a01, jnp.concatenate([b1, b2, b3], axis=1))
    return r[:, :n] + r[:, n:2 * n] + r[:, 2 * n:]


def _iota(shape, dim):
    return lax.broadcasted_iota(jnp.int32, shape, dim)


def _same_block(shape, width):
    return (_iota(shape, 0) // width) == (_iota(shape, 1) // width)


def _head_sumsq(x, width):
    m, n = x.shape
    slab = min(n, MXU_DIM)
    sel = _same_block((slab, slab), width).astype(BF16)
    outs = []
    for s in range(n // slab):
        xs = x[:, s * slab:(s + 1) * slab]
        outs.append(_dot((xs * xs).astype(BF16), sel))
    return outs[0] if len(outs) == 1 else jnp.concatenate(outs, axis=1)


def _sigmoid(x):
    return 1.0 / (1.0 + jnp.exp(-x))


def _silu(x):
    return x * _sigmoid(x)


def _softplus(x):
    return jnp.maximum(x, 0.0) + jnp.log(1.0 + jnp.exp(-jnp.abs(x)))


def _resident(shape):
    nd = len(shape)
    return pl.BlockSpec(shape, lambda *_: (0,) * nd, pipeline_mode=pl.Buffered(1))


def _params(n_axes=1):
    return pltpu.CompilerParams(dimension_semantics=("arbitrary",) * n_axes,
                                vmem_limit_bytes=VMEM_LIMIT_BYTES)


def _mod_kernel(c_ref, w_ref, b_ref, o_ref):
    c = c_ref[...]
    o_ref[...] = _dot_x3(_silu(c), w_ref[...]) + b_ref[...]


def _modulation(cvecs, w_ada, b_ada):
    n = w_ada.shape[1]
    tn = 1024
    return pl.pallas_call(
        _mod_kernel,
        grid=(n // tn,),
        in_specs=[pl.BlockSpec((8, D_MODEL), lambda j: (0, 0)),
                  pl.BlockSpec((D_MODEL, tn), lambda j: (0, j)),
                  pl.BlockSpec((1, tn), lambda j: (0, j))],
        out_specs=pl.BlockSpec((8, tn), lambda j: (0, j)),
        out_shape=jax.ShapeDtypeStruct((8, n), F32),
        compiler_params=_params(),
        name="adaln_modulation",
    )(cvecs, w_ada, b_ada.reshape(1, n))


def _mod_parts(mod_ref, first):
    m = mod_ref[...]
    base = 0 if first else 3 * D_MODEL
    return (m[:, base:base + D_MODEL], m[:, base + D_MODEL:base + 2 * D_MODEL],
            m[:, base + 2 * D_MODEL:base + 3 * D_MODEL])


def _rms_mod(x, g, shift, scale):
    ms = jnp.mean(x * x, axis=-1, keepdims=True)
    return (x * lax.rsqrt(ms + EPS) * g) * (1.0 + scale) + shift


def _rope(x, cos, sin_signed):
    outs = []
    half = HEAD_DIM // 4
    first_half = (_iota((1, LANES), 1) % (2 * half)) < half
    for s in range(x.shape[1] // LANES):
        xs = x[:, s * LANES:(s + 1) * LANES]
        partner = jnp.where(first_half, pltpu.roll(xs, LANES - half, axis=1), pltpu.roll(xs, half, axis=1))
        outs.append(xs * cos + partner * sin_signed)
    return outs[0] if len(outs) == 1 else jnp.concatenate(outs, axis=1)


def _in_proj_kernel(*refs, rope):
    if rope:
        (x_ref, mod_ref, g_ref, wqkv_ref, wdn_ref, wba_ref, wg_ref, gq_ref, gk_ref, cos_ref, sin_ref,
         q_ref, k_ref, v_ref, dqkv_ref, dz_ref, ba_ref, gab_ref) = refs
    else:
        (x_ref, mod_ref, g_ref, wqkv_ref, wdn_ref, wba_ref, wg_ref, gq_ref, gk_ref,
         q_ref, k_ref, v_ref, dqkv_ref, dz_ref, ba_ref, gab_ref) = refs
    shift, scale, _ = _mod_parts(mod_ref, True)
    h = _rms_mod(x_ref[...], g_ref[...], shift, scale).astype(BF16)

    qkv = _dot(h, wqkv_ref[...])
    aq = qkv[:, :ATTN_Q_W]
    ak = qkv[:, ATTN_Q_W:ATTN_Q_W + ATTN_KV_W]
    qn = aq * lax.rsqrt(_head_sumsq(aq, HEAD_DIM) * (1.0 / HEAD_DIM) + EPS) * gq_ref[...]
    kn = ak * lax.rsqrt(_head_sumsq(ak, HEAD_DIM) * (1.0 / HEAD_DIM) + EPS) * gk_ref[...]
    if rope:
        cos, sin = cos_ref[...], sin_ref[...]
        qn = _rope(qn, cos, sin)
        kn = _rope(kn, cos, sin)
    q_ref[...] = (qn * (HEAD_DIM ** -0.5)).astype(BF16)
    k_ref[...] = kn
    v_ref[...] = qkv[:, ATTN_Q_W + ATTN_KV_W:]

    dn = _dot(h, wdn_ref[...])
    dqkv_ref[...] = dn[:, :3 * DN_W]
    dz_ref[...] = dn[:, 3 * DN_W:]
    ba_ref[...] = _dot(h, wba_ref[...])
    gab_ref[...] = _dot(h, wg_ref[...])


def _in_proj(x, mod, group_of_tile, lw, tm, rope_tables=None):
    t = x.shape[0]
    rope = rope_tables is not None
    row = lambda w: pl.BlockSpec((tm, w), lambda i: (i, 0))
    in_specs = [row(D_MODEL),
                pl.BlockSpec((None, 1, 6 * D_MODEL), lambda i: (group_of_tile(i), 0, 0)),
                _resident((1, D_MODEL)),
                _resident(lw["w_qkv"].shape), _resident(lw["w_dn"].shape),
                _resident(lw["w_ba"].shape), _resident(lw["w_g"].shape),
                _resident((1, ATTN_Q_W)), _resident((1, ATTN_KV_W))]
    args = [x, mod, lw["norm1_g"], lw["w_qkv"], lw["w_dn"], lw["w_ba"], lw["w_g"], lw["gq"], lw["gk"]]
    if rope:
        cos, sin = rope_tables
        tiles_per_seq = cos.shape[0] // tm
        in_specs += [pl.BlockSpec((tm, LANES), lambda i: (i % tiles_per_seq, 0))] * 2
        args += [cos, sin]
    widths = (ATTN_Q_W, ATTN_KV_W, ATTN_KV_W, 3 * DN_W, DN_W, LANES, 2 * D_MODEL)
    dtypes = (BF16, F32, F32, F32, F32, F32, F32)
    return pl.pallas_call(
        functools.partial(_in_proj_kernel, rope=rope),
        grid=(t // tm,),
        in_specs=in_specs,
        out_specs=[row(w) for w in widths],
        out_shape=[jax.ShapeDtypeStruct((t, w), dt) for w, dt in zip(widths, dtypes)],
        compiler_params=_params(),
        name="in_proj_rope" if rope else "in_proj",
    )(*args)


def _attn_kernel(*refs, n_parts):
    q_ref = refs[0]
    kv_refs = refs[1:1 + 2 * n_parts]
    o_ref = refs[1 + 2 * n_parts]
    q = q_ref[...]
    tq = q.shape[0]
    outs = [None] * N_Q_HEADS
    for kvh in range(N_KV_HEADS):
        ks = [kv_refs[2 * p][:, kvh * HEAD_DIM:(kvh + 1) * HEAD_DIM].astype(BF16) for p in range(n_parts)]
        vs = [kv_refs[2 * p + 1][:, kvh * HEAD_DIM:(kvh + 1) * HEAD_DIM].astype(BF16) for p in range(n_parts)]
        heads = range(kvh * Q_GROUP, (kvh + 1) * Q_GROUP)
        qs = jnp.concatenate([q[:, j * HEAD_DIM:(j + 1) * HEAD_DIM] for j in heads], axis=0)
        ss = [_dot_nt(qs, k) for k in ks]
        m = functools.reduce(jnp.maximum, [jnp.max(s, axis=-1, keepdims=True) for s in ss])
        ps = [jnp.exp(s - m) for s in ss]
        den = functools.reduce(jnp.add, [jnp.sum(p, axis=-1, keepdims=True) for p in ps])
        acc = functools.reduce(jnp.add, [_dot(p.astype(BF16), v) for p, v in zip(ps, vs)])
        o = acc / den
        for g, j in enumerate(heads):
            outs[j] = o[g * tq:(g + 1) * tq]
    o_ref[...] = jnp.concatenate(outs, axis=1).astype(BF16)


def _attention(q, parts, seq_q, tq):
    t = q.shape[0]
    nq = seq_q // tq
    in_specs = [pl.BlockSpec((tq, ATTN_Q_W), lambda b, i: (b * nq + i, 0))]
    args = [q]
    for k, v, seq_k in parts:
        in_specs += [pl.BlockSpec((seq_k, ATTN_KV_W), lambda b, i: (b, 0))] * 2
        args += [k, v]
    return pl.pallas_call(
        functools.partial(_attn_kernel, n_parts=len(parts)),
        grid=(t // seq_q, nq),
        in_specs=in_specs,
        out_specs=pl.BlockSpec((tq, ATTN_Q_W), lambda b, i: (b * nq + i, 0)),
        out_shape=jax.ShapeDtypeStruct((t, ATTN_Q_W), BF16),
        compiler_params=_params(2),
        name="attention_%dparts" % len(parts),
    )(*args)


def _block_diag(x, mask):
    return jnp.where(mask, jnp.concatenate([x] * HEADS_PER_GROUP, axis=0), jnp.zeros((), x.dtype))


def _packed_mm(x, y, mask):
    m = x.shape[0]
    xh, xl = _split2(x)
    yh, yl = _split2(y)
    r = _dot(jnp.concatenate([xh, xl], axis=0), _block_diag(yh, mask))
    return r[:m] + r[m:] + _dot(xh, _block_diag(yl, mask))


def _packed_unit_inverse(lms, eye, mask):
    def mm1(x, y):
        return _dot(x.astype(BF16), _block_diag(y.astype(BF16), mask))

    ns = [-lm for lm in lms]
    rs = [eye + n for n in ns]
    ps = [mm1(n, n) for n in ns]
    steps = int(math.log2(CHUNK)) - 1
    for s in range(steps):
        last = s == steps - 1
        prods = [mm1(r if last else jnp.concatenate([r, p], axis=0), p) for r, p in zip(rs, ps)]
        rs = [r + prod[:CHUNK] for r, prod in zip(rs, prods)]
        if not last:
            ps = [prod[CHUNK:] for prod in prods]
    resid = [eye - r + _packed_mm(n, r, mask) for n, r in zip(ns, rs)]
    return [r + mm1(r, e) for r, e in zip(rs, resid)]


def _delta_kernel(*refs, seq, n_seq, has_init):
    if has_init:
        (x_ref, z_ref, ba_ref, cw_ref, alog_ref, dtb_ref, gn_ref, s0f_ref, s0b_ref,
         o_ref, sf_ref, sb_ref, q_s, k_s, v_s, gate_s, o_s, st_s) = refs
    else:
        (x_ref, z_ref, ba_ref, cw_ref, alog_ref, dtb_ref, gn_ref,
         o_ref, sf_ref, sb_ref, q_s, k_s, v_s, gate_s, o_s, st_s) = refs
    n_chunks = seq // CHUNK
    rb = min(seq, 256)
    pad = 8
    half = (DN_CONV - 1) // 2
    n_gate = 4 * DN_HEADS

    exp_r = _iota((LANES, 4 * DN_W), 0)
    expand = ((exp_r < 3 * n_gate) & ((_iota((LANES, 4 * DN_W), 1) // DN_DK) == exp_r % n_gate)).astype(BF16)
    lane = _iota((1, LANES), 1)
    blk_r = _iota((rb, rb), 0)
    blk_c = _iota((rb, rb), 1)
    same_chunk = (blk_r // CHUNK) == (blk_c // CHUNK)
    cum_f = (same_chunk & (blk_c <= blk_r)).astype(BF16)
    cum_b = (same_chunk & (blk_c >= blk_r)).astype(BF16)
    for blk in range(n_seq * seq // rb):
        r0 = blk * rb
        in_seq = r0 % seq
        for part, dst in enumerate((q_s, k_s, v_s)):
            cols = slice(part * DN_W, (part + 1) * DN_W)
            xb = x_ref[r0:r0 + rb, cols]
            prev = x_ref[r0 - pad:r0, cols] if in_seq > 0 else jnp.zeros((pad, DN_W), F32)
            nxt = x_ref[r0 + rb:r0 + rb + pad, cols] if in_seq + rb < seq else jnp.zeros((pad, DN_W), F32)
            xe = jnp.concatenate([prev, xb, nxt], axis=0)
            ne = rb + 2 * pad
            y = jnp.zeros((rb, DN_W), F32)
            for tap in range(DN_CONV):
                d = tap - half
                sh = xe if d == 0 else pltpu.roll(xe, (ne - d) % ne, axis=0)
                y = y + sh[pad:pad + rb] * cw_ref[tap:tap + 1, cols]
            y = _silu(y)
            if part == 0:
                y = y * lax.rsqrt(_head_sumsq(y, DN_DK) + EPS) * (DN_DK ** -0.5)
            elif part == 1:
                y = y * lax.rsqrt(_head_sumsq(y, DN_DK) + EPS)
            dst[r0:r0 + rb, :] = y
        ba = ba_ref[r0:r0 + rb, :]
        decay = -jnp.exp(alog_ref[...]) * _softplus(ba + dtb_ref[...])
        vals = jnp.where(lane < 2 * DN_HEADS, _sigmoid(ba), jnp.where(lane < n_gate, decay, 0.0))
        narrow = jnp.where(lane < 2 * DN_HEADS, vals,
                           jnp.where(lane < 3 * DN_HEADS, _dot_exact_lhs(cum_f, vals),
                                     jnp.where(lane < n_gate, _dot_exact_lhs(cum_b, vals), 0.0)))
        t1, t2, t3 = (t.astype(F32) for t in _split3(narrow))
        stacked = t1 + pltpu.roll(t2, n_gate, axis=1) + pltpu.roll(t3, 2 * n_gate, axis=1)
        gate_s[r0:r0 + rb, :] = _dot(stacked.astype(BF16), expand)

    bd_mask = _same_block((GW, GW), DN_DK)
    row = _iota((CHUNK, GW), 0)
    col = _iota((CHUNK, GW), 1) % CHUNK
    diag = row == col
    eye = diag.astype(F32)
    dirs = ((col <= row, col < row, CHUNK - 1), (col >= row, col > row, 0))
    chains = [(s, d, g) for s in range(n_seq) for d in range(2) for g in range(N_GROUPS)]
    for ci, (s, d, g) in enumerate(chains):
        if has_init:
            s0 = (s0f_ref, s0b_ref)[d][s, :, g * GW:(g + 1) * GW]
            st_s[ci] = _block_diag(s0, bd_mask)
        else:
            st_s[ci] = jnp.zeros((GW, GW), F32)

    def bd(x):
        return _block_diag(x.astype(BF16), bd_mask)

    def chunk_step(n, carry):
        ins = []
        for s, d, g in chains:
            incl, strict, last_row = dirs[d]
            c = n if d == 0 else n_chunks - 1 - n
            rows = pl.ds(pl.multiple_of(s * seq + c * CHUNK, CHUNK), CHUNK)
            lanes = slice(g * GW, (g + 1) * GW)
            beta = gate_s[rows, pl.ds(d * DN_W + g * GW, GW)]
            gc = gate_s[rows, pl.ds(2 * DN_W + d * DN_W + g * GW, GW)]
            gc_col = jnp.sum(jnp.where(diag, gc, 0.0), axis=0, keepdims=True)
            g_last = gc[last_row:last_row + 1, :]
            dec = jnp.where(incl, jnp.exp(jnp.minimum(gc - gc_col, 0.0)), 0.0)
            eg = jnp.exp(gc)
            q = q_s[rows, lanes]
            k = k_s[rows, lanes]
            kb = k * beta
            ins.append(dict(rows=rows, lanes=lanes, d=d, strict=strict, dec=dec, g_last=g_last, k=k, kb=kb, q=q,
                            vb=v_s[rows, lanes] * beta, kbg=kb * eg, qg=q * eg,
                            k_dec=k * jnp.exp(g_last - gc)))
        grams = [_dot_nt(jnp.concatenate([i["kb"], i["q"]], axis=0).astype(BF16), bd(i["k"])) for i in ins]
        lms = [jnp.where(i["strict"], gm[:CHUNK] * i["dec"], 0.0) for i, gm in zip(ins, grams)]
        attn = [gm[CHUNK:] * i["dec"] for i, gm in zip(ins, grams)]
        ts = [t.astype(BF16) for t in _packed_unit_inverse(lms, eye, bd_mask)]
        us = [_dot(t, bd(i["vb"])) for i, t in zip(ins, ts)]
        ws = [_dot(t, bd(i["kbg"])) for i, t in zip(ins, ts)]
        states = [st_s[ci] for ci in range(len(chains))]
        ws_qs = [_dot(jnp.concatenate([w, i["qg"]], axis=0).astype(BF16), st.astype(BF16))
                 for i, w, st in zip(ins, ws, states)]
        v_news = [u - x[:CHUNK] for u, x in zip(us, ws_qs)]
        outs = [x[CHUNK:] + _dot(a.astype(BF16), bd(vn)) for x, a, vn in zip(ws_qs, attn, v_news)]
        upds = [_dot_tn(i["k_dec"].astype(BF16), vn.astype(BF16)) for i, vn in zip(ins, v_news)]
        for ci, (i, st, upd, o) in enumerate(zip(ins, states, upds, outs)):
            st_s[ci] = st * jnp.exp(i["g_last"]) + jnp.where(bd_mask, upd, 0.0)
            o_s[i["d"], i["rows"], i["lanes"]] = o
        return carry

    lax.fori_loop(0, n_chunks, chunk_step, 0)

    for blk in range(n_seq * seq // rb):
        rows = slice(blk * rb, (blk + 1) * rb)
        o = o_s[0, rows, :] + o_s[1, rows, :]
        o = o * lax.rsqrt(_head_sumsq(o, DN_DV) * (1.0 / DN_DV) + EPS) * gn_ref[...]
        o_ref[rows, :] = (o * _silu(z_ref[rows, :])).astype(BF16)
    for ci, (s, d, g) in enumerate(chains):
        st = st_s[ci]
        for hb in range(HEADS_PER_GROUP):
            (sf_ref, sb_ref)[d][s, g * HEADS_PER_GROUP + hb] = st[hb * DN_DK:(hb + 1) * DN_DK,
                                                                 hb * DN_DV:(hb + 1) * DN_DV]


def _delta(dqkv, dz, ba, lw, seq, n_seq, init=None):
    t = dqkv.shape[0]
    nb = t // seq
    rows = n_seq * seq
    has_init = init is not None
    seq_block = lambda w: pl.BlockSpec((rows, w), lambda b: (b, 0))
    state_block = pl.BlockSpec((n_seq, DN_DK, DN_W), lambda b: (b, 0, 0))
    in_specs = [seq_block(3 * DN_W), seq_block(DN_W), seq_block(LANES),
                _resident((DN_CONV, 3 * DN_W)), _resident((1, LANES)), _resident((1, LANES)),
                _resident((1, DN_W))]
    args = [dqkv, dz, ba, lw["conv_w"], lw["a_log"], lw["dt_bias"], lw["gn"]]
    if has_init:
        in_specs += [state_block, state_block]
        args += list(init)
    final_block = pl.BlockSpec((n_seq, DN_HEADS, DN_DK, DN_DV), lambda b: (b, 0, 0, 0))
    final_shape = jax.ShapeDtypeStruct((nb, DN_HEADS, DN_DK, DN_DV), F32)
    return pl.pallas_call(
        functools.partial(_delta_kernel, seq=seq, n_seq=n_seq, has_init=has_init),
        grid=(nb // n_seq,),
        in_specs=in_specs,
        out_specs=[seq_block(DN_W), final_block, final_block],
        out_shape=[jax.ShapeDtypeStruct((t, DN_W), BF16), final_shape, final_shape],
        scratch_shapes=[pltpu.VMEM((rows, DN_W), F32), pltpu.VMEM((rows, DN_W), F32),
                        pltpu.VMEM((rows, DN_W), F32), pltpu.VMEM((rows, 4 * DN_W), F32),
                        pltpu.VMEM((2, rows, DN_W), F32),
                        pltpu.VMEM((n_seq * 2 * N_GROUPS, GW, GW), F32)],
        compiler_params=_params(),
        name="delta_rule_init" if has_init else "delta_rule",
    )(*args)


def _post_kernel(x_ref, oa_ref, od_ref, gab_ref, mod_ref, g2_ref, wpa_ref, wpb_ref, wo_ref,
                 wg_ref, wu_ref, wd_ref, y_ref):
    _, _, gate1 = _mod_parts(mod_ref, True)
    shift2, scale2, gate2 = _mod_parts(mod_ref, False)
    gab = gab_ref[...]
    merged = (_sigmoid(gab[:, :D_MODEL]) * _dot(oa_ref[...], wpa_ref[...])
              + _sigmoid(gab[:, D_MODEL:]) * _dot(od_ref[...], wpb_ref[...]))
    x1 = x_ref[...] + gate1 * _dot(merged.astype(BF16), wo_ref[...])
    h2 = _rms_mod(x1, g2_ref[...], shift2, scale2).astype(BF16)
    act = _silu(_dot(h2, wg_ref[...])) * _dot(h2, wu_ref[...])
    y_ref[...] = x1 + gate2 * _dot(act.astype(BF16), wd_ref[...])


def _post(x, oa, od, gab, mod, group_of_tile, lw, tm):
    t = x.shape[0]
    row = lambda w: pl.BlockSpec((tm, w), lambda i: (i, 0))
    weights = [lw[k] for k in ("w_pa", "w_pb", "w_o", "w_gate", "w_up", "w_down")]
    return pl.pallas_call(
        _post_kernel,
        grid=(t // tm,),
        in_specs=[row(D_MODEL), row(ATTN_Q_W), row(DN_W), row(2 * D_MODEL),
                  pl.BlockSpec((None, 1, 6 * D_MODEL), lambda i: (group_of_tile(i), 0, 0)),
                  _resident((1, D_MODEL))] + [_resident(w.shape) for w in weights],
        out_specs=row(D_MODEL),
        out_shape=jax.ShapeDtypeStruct((t, D_MODEL), F32),
        compiler_params=_params(),
        name="post_block",
    )(x, oa, od, gab, mod, lw["norm2_g"], *weights)


def _rope_tables(n_tokens):
    quarter = HEAD_DIM // 4
    lane = jnp.arange(LANES)
    d = lane % HEAD_DIM
    inv = ROPE_THETA ** (-(d % quarter).astype(F32) / quarter)
    t = jnp.arange(n_tokens)
    pos = jnp.where(d[None, :] < HEAD_DIM // 2, (t // GRID_W)[:, None], (t % GRID_W)[:, None]).astype(F32)
    ang = pos * inv[None, :]
    sign = jnp.where((d % (2 * quarter)) < quarter, -1.0, 1.0)
    return jnp.cos(ang), jnp.sin(ang) * sign[None, :]


def _pack_states(s):
    b = s.shape[0]
    return s.transpose(0, 2, 1, 3).reshape(b, DN_DK, DN_W)


def _layer_weights(l, w_in, norm1_g, q_norm_g, k_norm_g, conv_w, a_log, dt_bias, dn_norm_g,
                   w_pa, w_pb, w_o, norm2_g, w_gate, w_up, w_down):
    wi = w_in[l]
    o_dn = ATTN_Q_W + 2 * ATTN_KV_W
    o_ba = o_dn + 4 * DN_W
    o_g = o_ba + 4 * DN_HEADS
    pad_small = lambda a: jnp.pad(a.reshape(1, -1), ((0, 0), (0, LANES - a.size)))
    return dict(
        w_qkv=wi[:, :o_dn].astype(BF16),
        w_dn=wi[:, o_dn:o_ba].astype(BF16),
        w_ba=jnp.pad(wi[:, o_ba:o_g], ((0, 0), (0, LANES - 4 * DN_HEADS))).astype(BF16),
        w_g=wi[:, o_g:].astype(BF16),
        norm1_g=norm1_g[l].reshape(1, D_MODEL),
        norm2_g=norm2_g[l].reshape(1, D_MODEL),
        gq=jnp.tile(q_norm_g[l], N_Q_HEADS).reshape(1, ATTN_Q_W),
        gk=jnp.tile(k_norm_g[l], N_KV_HEADS).reshape(1, ATTN_KV_W),
        conv_w=conv_w[l],
        a_log=pad_small(jnp.concatenate([jnp.zeros((2 * DN_HEADS,), F32), a_log[l].reshape(-1)])),
        dt_bias=pad_small(jnp.concatenate([jnp.zeros((2 * DN_HEADS,), F32), dt_bias[l].reshape(-1)])),
        gn=jnp.tile(dn_norm_g[l], DN_HEADS).reshape(1, DN_W),
        w_pa=w_pa[l].astype(BF16), w_pb=w_pb[l].astype(BF16), w_o=w_o[l].astype(BF16),
        w_gate=w_gate[l].astype(BF16), w_up=w_up[l].astype(BF16), w_down=w_down[l].astype(BF16),
    )


def kernel(x_prompt, x_sample, cache_k, cache_v, state_fwd, state_bwd, c, c_ctx, w_ada, b_ada, norm1_g, w_in,
           q_norm_g, k_norm_g, conv_w, a_log, dt_bias, dn_norm_g, w_pa, w_pb, w_o, norm2_g, w_gate, w_up, w_down):
    batch, seq, _ = x_prompt.shape
    dec_batch, dec_seq, _ = x_sample.shape
    depth = w_in.shape[0]
    past = cache_k.shape[2]
    assert dec_batch + 1 <= 8 and seq % CHUNK == 0 and dec_seq % CHUNK == 0

    cvecs = jnp.zeros((8, D_MODEL), F32).at[0].set(c_ctx).at[1:1 + dec_batch].set(c)
    rope_tables = _rope_tables(dec_seq)
    tm_in = 512
    tm_post = 256
    ctx_group = lambda i: 0

    def lat_group(tm):
        return lambda i: 1 + i // (dec_seq // tm)

    yp = x_prompt.reshape(batch * seq, D_MODEL)
    ys = x_sample.reshape(dec_batch * dec_seq, D_MODEL)
    ks_out, vs_out, sf_out, sb_out = [], [], [], []
    for l in range(depth):
        lw = _layer_weights(l, w_in, norm1_g, q_norm_g, k_norm_g, conv_w, a_log, dt_bias, dn_norm_g,
                            w_pa, w_pb, w_o, norm2_g, w_gate, w_up, w_down)
        mod = _modulation(cvecs, w_ada[l], b_ada[l])[:1 + dec_batch].reshape(1 + dec_batch, 1, 6 * D_MODEL)

        q, kn, v, dqkv, dz, ba, gab = _in_proj(yp, mod, ctx_group, lw, tm_in)
        oa = _attention(q, [(kn, v, seq)], seq, seq)
        od, sf, sb = _delta(dqkv, dz, ba, lw, seq, 2)
        yp = _post(yp, oa, od, gab, mod, ctx_group, lw, tm_post)
        ks_out.append(kn.reshape(batch, seq, N_KV_HEADS, HEAD_DIM))
        vs_out.append(v.reshape(batch, seq, N_KV_HEADS, HEAD_DIM))
        sf_out.append(sf)
        sb_out.append(sb)

        q, kr, v, dqkv, dz, ba, gab = _in_proj(ys, mod, lat_group(tm_in), lw, tm_in, rope_tables)
        ck = cache_k[:, l].reshape(dec_batch * past, ATTN_KV_W)
        cv = cache_v[:, l].reshape(dec_batch * past, ATTN_KV_W)
        oa = _attention(q, [(ck, cv, past), (kr, v, dec_seq)], dec_seq, 256)
        init = (_pack_states(state_fwd[:, l]), _pack_states(state_bwd[:, l]))
        od, _, _ = _delta(dqkv, dz, ba, lw, dec_seq, 1, init)
        ys = _post(ys, oa, od, gab, mod, lat_group(tm_post), lw, tm_post)

    return (yp.reshape(batch, seq, D_MODEL), ys.reshape(dec_batch, dec_seq, D_MODEL),
            jnp.stack(ks_out, axis=1), jnp.stack(vs_out, axis=1),
            jnp.stack(sf_out, axis=1), jnp.stack(sb_out, axis=1))
```

```python
import functools
import math

import jax
import jax.numpy as jnp
from jax import lax
from jax.experimental import pallas as pl
from jax.experimental.pallas import tpu as pltpu

D_MODEL = 1024
GRID_W = 64
HEAD_DIM = 64
N_Q_HEADS = 8
N_KV_HEADS = 2
Q_GROUP = N_Q_HEADS // N_KV_HEADS
ATTN_Q_W = N_Q_HEADS * HEAD_DIM
ATTN_KV_W = N_KV_HEADS * HEAD_DIM
ROPE_THETA = 10000.0
DN_HEADS = 8
DN_DK = 64
DN_DV = 64
DN_W = DN_HEADS * DN_DK
DN_CONV = 5
CHUNK = 64
EPS = 1e-6

LANES = 128
MXU_DIM = 256
VMEM_LIMIT_BYTES = 56 * 1024 * 1024

HEADS_PER_GROUP = MXU_DIM // DN_DK
N_GROUPS = DN_HEADS // HEADS_PER_GROUP
GW = HEADS_PER_GROUP * DN_DK

F32 = jnp.float32
BF16 = jnp.bfloat16


def _dot(a, b):
    return jnp.dot(a, b, preferred_element_type=F32)


def _dot_nt(a, b):
    return lax.dot_general(a, b, (((1,), (1,)), ((), ())), preferred_element_type=F32)


def _dot_tn(a, b):
    return lax.dot_general(a, b, (((0,), (0,)), ((), ())), preferred_element_type=F32)


def _split2(x):
    hi = x.astype(BF16)
    lo = (x - hi.astype(F32)).astype(BF16)
    return hi, lo


def _split3(x):
    hi = x.astype(BF16)
    r = x - hi.astype(F32)
    mid = r.astype(BF16)
    lo = (r - mid.astype(F32)).astype(BF16)
    return hi, mid, lo


def _dot_x3(a, b):
    m = a.shape[0]
    ah, al = _split2(a)
    bh, bl = _split2(b)
    r = _dot(jnp.concatenate([ah, al], axis=0), bh)
    return r[:m] + r[m:] + _dot(ah, bl)


def _dot_exact_rhs(a, b01):
    m = a.shape[0]
    a1, a2, a3 = _split3(a)
    r = _dot(jnp.concatenate([a1, a2, a3], axis=0), b01)
    return r[:m] + r[m:2 * m] + r[2 * m:]


def _dot_exact_lhs(a01, b):
    n = b.shape[1]
    b1, b2, b3 = _split3(b)
    r = _dot(a01, jnp.concatenate([b1, b2, b3], axis=1))
    return r[:, :n] + r[:, n:2 * n] + r[:, 2 * n:]


def _iota(shape, dim):
    return lax.broadcasted_iota(jnp.int32, shape, dim)


def _same_block(shape, width):
    return (_iota(shape, 0) // width) == (_iota(shape, 1) // width)


def _head_sumsq(x, width):
    m, n = x.shape
    slab = min(n, MXU_DIM)
    sel = _same_block((slab, slab), width).astype(BF16)
    outs = []
    for s in range(n // slab):
        xs = x[:, s * slab:(s + 1) * slab]
        outs.append(_dot((xs * xs).astype(BF16), sel))
    return outs[0] if len(outs) == 1 else jnp.concatenate(outs, axis=1)


def _sigmoid(x):
    return 1.0 / (1.0 + jnp.exp(-x))


def _silu(x):
    return x * _sigmoid(x)


def _softplus(x):
    return jnp.maximum(x, 0.0) + jnp.log(1.0 + jnp.exp(-jnp.abs(x)))


def _resident(shape):
    nd = len(shape)
    return pl.BlockSpec(shape, lambda *_: (0,) * nd, pipeline_mode=pl.Buffered(1))


def _params(n_axes=1):
    return pltpu.CompilerParams(dimension_semantics=("arbitrary",) * n_axes,
                                vmem_limit_bytes=VMEM_LIMIT_BYTES)


def _mod_kernel(c_ref, w_ref, b_ref, o_ref):
    c = c_ref[...]
    o_ref[...] = _dot_x3(_silu(c), w_ref[...]) + b_ref[...]


def _modulation(cvecs, w_ada, b_ada):
    n = w_ada.shape[1]
    tn = 1024
    return pl.pallas_call(
        _mod_kernel,
        grid=(n // tn,),
        in_specs=[pl.BlockSpec((8, D_MODEL), lambda j: (0, 0)),
                  pl.BlockSpec((D_MODEL, tn), lambda j: (0, j)),
                  pl.BlockSpec((1, tn), lambda j: (0, j))],
        out_specs=pl.BlockSpec((8, tn), lambda j: (0, j)),
        out_shape=jax.ShapeDtypeStruct((8, n), F32),
        compiler_params=_params(),
        name="adaln_modulation",
    )(cvecs, w_ada, b_ada.reshape(1, n))


def _mod_parts(mod_ref, first):
    m = mod_ref[...]
    base = 0 if first else 3 * D_MODEL
    return (m[:, base:base + D_MODEL], m[:, base + D_MODEL:base + 2 * D_MODEL],
            m[:, base + 2 * D_MODEL:base + 3 * D_MODEL])


def _rms_mod(x, g, shift, scale):
    ms = jnp.mean(x * x, axis=-1, keepdims=True)
    return (x * lax.rsqrt(ms + EPS) * g) * (1.0 + scale) + shift


def _rope(x, cos, sin_signed):
    outs = []
    half = HEAD_DIM // 4
    first_half = (_iota((1, LANES), 1) % (2 * half)) < half
    for s in range(x.shape[1] // LANES):
        xs = x[:, s * LANES:(s + 1) * LANES]
        partner = jnp.where(first_half, pltpu.roll(xs, LANES - half, axis=1), pltpu.roll(xs, half, axis=1))
        outs.append(xs * cos + partner * sin_signed)
    return outs[0] if len(outs) == 1 else jnp.concatenate(outs, axis=1)


_QKV_W = ATTN_Q_W + 2 * ATTN_KV_W
_DN_OFF = _QKV_W
_G_OFF = _DN_OFF + 4 * DN_W
_BA_OFF = _G_OFF + 2 * D_MODEL
_IN_W_PADDED = _BA_OFF + LANES


def _in_proj_kernel(*refs, rope, seq):
    if rope:
        (x_ref, mod_ref, g_ref, w_ref, gq_ref, gk_ref, cos_ref, sin_ref,
         q_ref, k_ref, v_ref, dqkv_ref, dz_ref, ba_ref, gab_ref) = refs
    else:
        (x_ref, mod_ref, g_ref, w_ref, gq_ref, gk_ref,
         q_ref, k_ref, v_ref, dqkv_ref, dz_ref, ba_ref, gab_ref) = refs
    shift, scale, _ = _mod_parts(mod_ref, True)
    h = _rms_mod(x_ref[...], g_ref[...], shift, scale).astype(BF16)

    qkv = _dot(h, w_ref[:, 0:_QKV_W])
    aq = qkv[:, :ATTN_Q_W]
    ak = qkv[:, ATTN_Q_W:ATTN_Q_W + ATTN_KV_W]
    av = qkv[:, ATTN_Q_W + ATTN_KV_W:]
    qn = aq * lax.rsqrt(_head_sumsq(aq, HEAD_DIM) * (1.0 / HEAD_DIM) + EPS) * gq_ref[...]
    kn = ak * lax.rsqrt(_head_sumsq(ak, HEAD_DIM) * (1.0 / HEAD_DIM) + EPS) * gk_ref[...]
    if rope:
        cos, sin = cos_ref[...], sin_ref[...]
        qn = _rope(qn, cos, sin)
        k_ref[...] = _rope(kn, cos, sin)
        v_ref[...] = av
    else:
        kt = kn.T
        vt = av.T
        for s in range(kt.shape[1] // seq):
            for hd in range(N_KV_HEADS):
                k_ref[s, hd] = kt[hd * HEAD_DIM:(hd + 1) * HEAD_DIM, s * seq:(s + 1) * seq]
                v_ref[s, hd] = vt[hd * HEAD_DIM:(hd + 1) * HEAD_DIM, s * seq:(s + 1) * seq]
    q_ref[...] = (qn * (HEAD_DIM ** -0.5)).astype(BF16)

    dn = _dot(h, w_ref[:, _DN_OFF:_G_OFF])
    dqkv_ref[...] = dn[:, :3 * DN_W]
    dz_ref[...] = dn[:, 3 * DN_W:]
    gab_ref[...] = _dot(h, w_ref[:, _G_OFF:_BA_OFF])
    ba_ref[...] = _dot(h, w_ref[:, _BA_OFF:_IN_W_PADDED])


def _in_proj(x, mod, group_of_tile, lw, tm, seq, rope_tables=None):
    t = x.shape[0]
    rope = rope_tables is not None
    row = lambda w: pl.BlockSpec((tm, w), lambda i: (i, 0))
    in_specs = [row(D_MODEL),
                pl.BlockSpec((None, 1, 6 * D_MODEL), lambda i: (group_of_tile(i), 0, 0)),
                _resident((1, D_MODEL)), _resident(lw["w_in"].shape),
                _resident((1, ATTN_Q_W)), _resident((1, ATTN_KV_W))]
    args = [x, mod, lw["norm1_g"], lw["w_in"], lw["gq"], lw["gk"]]
    if rope:
        cos, sin = rope_tables
        tiles_per_seq = cos.shape[0] // tm
        in_specs += [pl.BlockSpec((tm, LANES), lambda i: (i % tiles_per_seq, 0))] * 2
        args += [cos, sin]
        kv_spec = row(ATTN_KV_W)
        kv_shape = jax.ShapeDtypeStruct((t, ATTN_KV_W), F32)
    else:
        kv_spec = pl.BlockSpec((tm // seq, N_KV_HEADS, HEAD_DIM, seq), lambda i: (i, 0, 0, 0))
        kv_shape = jax.ShapeDtypeStruct((t // seq, N_KV_HEADS, HEAD_DIM, seq), F32)
    widths = (3 * DN_W, DN_W, LANES, 2 * D_MODEL)
    return pl.pallas_call(
        functools.partial(_in_proj_kernel, rope=rope, seq=seq),
        grid=(t // tm,),
        in_specs=in_specs,
        out_specs=[row(ATTN_Q_W), kv_spec, kv_spec] + [row(w) for w in widths],
        out_shape=[jax.ShapeDtypeStruct((t, ATTN_Q_W), BF16), kv_shape, kv_shape]
                  + [jax.ShapeDtypeStruct((t, w), F32) for w in widths],
        compiler_params=_params(),
        name="in_proj_rope" if rope else "in_proj",
    )(*args)


def _attn_kernel(*refs, transposed):
    n_parts = len(transposed)
    q_ref = refs[0]
    kv_refs = refs[1:1 + 2 * n_parts]
    o_ref = refs[1 + 2 * n_parts]
    q = q_ref[...]
    tq = q.shape[0]
    outs = [None] * N_Q_HEADS
    for kvh in range(N_KV_HEADS):
        cols = slice(kvh * HEAD_DIM, (kvh + 1) * HEAD_DIM)
        heads = range(kvh * Q_GROUP, (kvh + 1) * Q_GROUP)
        qs = jnp.concatenate([q[:, j * HEAD_DIM:(j + 1) * HEAD_DIM] for j in heads], axis=0)
        ss, vs = [], []
        for p in range(n_parts):
            k_ref, v_ref = kv_refs[2 * p], kv_refs[2 * p + 1]
            if transposed[p]:
                ss.append(_dot(qs, k_ref[kvh].astype(BF16)))
                vs.append(v_ref[kvh].astype(BF16))
            else:
                ss.append(_dot_nt(qs, k_ref[:, cols].astype(BF16)))
                vs.append(v_ref[:, cols].astype(BF16))
        m = functools.reduce(jnp.maximum, [jnp.max(s, axis=-1, keepdims=True) for s in ss])
        ps = [jnp.exp(s - m) for s in ss]
        den = functools.reduce(jnp.add, [jnp.sum(p, axis=-1, keepdims=True) for p in ps])
        acc = functools.reduce(jnp.add, [(_dot_nt if tr else _dot)(p.astype(BF16), v)
                                         for p, v, tr in zip(ps, vs, transposed)])
        o = acc / den
        for g, j in enumerate(heads):
            outs[j] = o[g * tq:(g + 1) * tq]
    o_ref[...] = jnp.concatenate(outs, axis=1).astype(BF16)


def _attention(q, parts, seq_q, tq):
    t = q.shape[0]
    nq = seq_q // tq
    in_specs = [pl.BlockSpec((tq, ATTN_Q_W), lambda b, i: (b * nq + i, 0))]
    args = [q]
    transposed = []
    for k, v, seq_k in parts:
        transposed.append(k.ndim == 4)
        if k.ndim == 4:
            in_specs += [pl.BlockSpec((None, N_KV_HEADS, HEAD_DIM, seq_k), lambda b, i: (b, 0, 0, 0))] * 2
        else:
            in_specs += [pl.BlockSpec((seq_k, ATTN_KV_W), lambda b, i: (b, 0))] * 2
        args += [k, v]
    return pl.pallas_call(
        functools.partial(_attn_kernel, transposed=tuple(transposed)),
        grid=(t // seq_q, nq),
        in_specs=in_specs,
        out_specs=pl.BlockSpec((tq, ATTN_Q_W), lambda b, i: (b * nq + i, 0)),
        out_shape=jax.ShapeDtypeStruct((t, ATTN_Q_W), BF16),
        compiler_params=_params(2),
        name="attention_%dparts" % len(parts),
    )(*args)


def _block_diag(x, mask):
    return jnp.where(mask, jnp.concatenate([x] * HEADS_PER_GROUP, axis=0), jnp.zeros((), x.dtype))


def _heads_transposed(x):
    xt = x.T
    return jnp.concatenate([xt[hb * DN_DK:(hb + 1) * DN_DK] for hb in range(HEADS_PER_GROUP)], axis=1)


def _packed_mm(x, y, mask):
    m = x.shape[0]
    xh, xl = _split2(x)
    yh, yl = _split2(y)
    r = _dot(jnp.concatenate([xh, xl], axis=0), _block_diag(yh, mask))
    return r[:m] + r[m:] + _dot(xh, _block_diag(yl, mask))


def _packed_unit_inverse(lms, eye, mask):
    def mm1(x, y):
        return _dot(x.astype(BF16), _block_diag(y.astype(BF16), mask))

    ns = [-lm for lm in lms]
    rs = [eye + n for n in ns]
    ps = [mm1(n, n) for n in ns]
    steps = int(math.log2(CHUNK)) - 1
    for s in range(steps):
        last = s == steps - 1
        prods = [mm1(r if last else jnp.concatenate([r, p], axis=0), p) for r, p in zip(rs, ps)]
        rs = [r + prod[:CHUNK] for r, prod in zip(rs, prods)]
        if not last:
            ps = [prod[CHUNK:] for prod in prods]
    resid = [eye - r + _packed_mm(n, r, mask) for n, r in zip(ns, rs)]
    return [r + mm1(r, e) for r, e in zip(rs, resid)]


def _delta_kernel(*refs, seq, n_seq, has_init):
    if has_init:
        (x_ref, z_ref, ba_ref, cw_ref, alog_ref, dtb_ref, gn_ref, s0f_ref, s0b_ref,
         o_ref, sf_ref, sb_ref, q_s, k_s, v_s, gate_s, o_s, st_s) = refs
    else:
        (x_ref, z_ref, ba_ref, cw_ref, alog_ref, dtb_ref, gn_ref,
         o_ref, sf_ref, sb_ref, q_s, k_s, v_s, gate_s, o_s, st_s) = refs
    n_chunks = seq // CHUNK
    rb = min(seq, 256)
    pad = 8
    half = (DN_CONV - 1) // 2
    n_gate = 4 * DN_HEADS

    exp_r = _iota((LANES, 4 * DN_W), 0)
    expand = ((exp_r < 3 * n_gate) & ((_iota((LANES, 4 * DN_W), 1) // DN_DK) == exp_r % n_gate)).astype(BF16)
    lane = _iota((1, LANES), 1)
    blk_r = _iota((rb, rb), 0)
    blk_c = _iota((rb, rb), 1)
    same_chunk = (blk_r // CHUNK) == (blk_c // CHUNK)
    cum_f = (same_chunk & (blk_c <= blk_r)).astype(BF16)
    cum_b = (same_chunk & (blk_c >= blk_r)).astype(BF16)
    for blk in range(n_seq * seq // rb):
        r0 = blk * rb
        in_seq = r0 % seq
        for part, dst in enumerate((q_s, k_s, v_s)):
            cols = slice(part * DN_W, (part + 1) * DN_W)
            xb = x_ref[r0:r0 + rb, cols]
            prev = x_ref[r0 - pad:r0, cols] if in_seq > 0 else jnp.zeros((pad, DN_W), F32)
            nxt = x_ref[r0 + rb:r0 + rb + pad, cols] if in_seq + rb < seq else jnp.zeros((pad, DN_W), F32)
            xe = jnp.concatenate([prev, xb, nxt], axis=0)
            ne = rb + 2 * pad
            y = jnp.zeros((rb, DN_W), F32)
            for tap in range(DN_CONV):
                d = tap - half
                sh = xe if d == 0 else pltpu.roll(xe, (ne - d) % ne, axis=0)
                y = y + sh[pad:pad + rb] * cw_ref[tap:tap + 1, cols]
            y = _silu(y)
            if part == 0:
                y = y * lax.rsqrt(_head_sumsq(y, DN_DK) + EPS) * (DN_DK ** -0.5)
            elif part == 1:
                y = y * lax.rsqrt(_head_sumsq(y, DN_DK) + EPS)
            dst[r0:r0 + rb, :] = y
        ba = ba_ref[r0:r0 + rb, :]
        decay = -jnp.exp(alog_ref[...]) * _softplus(ba + dtb_ref[...])
        vals = jnp.where(lane < 2 * DN_HEADS, _sigmoid(ba), jnp.where(lane < n_gate, decay, 0.0))
        narrow = jnp.where(lane < 2 * DN_HEADS, vals,
                           jnp.where(lane < 3 * DN_HEADS, _dot_exact_lhs(cum_f, vals),
                                     jnp.where(lane < n_gate, _dot_exact_lhs(cum_b, vals), 0.0)))
        t1, t2, t3 = (t.astype(F32) for t in _split3(narrow))
        stacked = t1 + pltpu.roll(t2, n_gate, axis=1) + pltpu.roll(t3, 2 * n_gate, axis=1)
        gate_s[r0:r0 + rb, :] = _dot(stacked.astype(BF16), expand)

    bd_mask = _same_block((GW, GW), DN_DK)
    row = _iota((CHUNK, GW), 0)
    col = _iota((CHUNK, GW), 1) % CHUNK
    diag = row == col
    eye = diag.astype(F32)
    dirs = ((col <= row, col < row, CHUNK - 1), (col >= row, col > row, 0))
    chains = [(s, d, g) for s in range(n_seq) for d in range(2) for g in range(N_GROUPS)]
    for ci, (s, d, g) in enumerate(chains):
        if has_init:
            st_s[ci] = (s0f_ref, s0b_ref)[d][s, :, g * GW:(g + 1) * GW]
        else:
            st_s[ci] = jnp.zeros((DN_DK, GW), F32)

    def bd(x):
        return _block_diag(x.astype(BF16), bd_mask)

    def chunk_step(n, carry):
        ins = []
        for s, d, g in chains:
            incl, strict, last_row = dirs[d]
            c = n if d == 0 else n_chunks - 1 - n
            rows = pl.ds(pl.multiple_of(s * seq + c * CHUNK, CHUNK), CHUNK)
            lanes = slice(g * GW, (g + 1) * GW)
            beta = gate_s[rows, pl.ds(d * DN_W + g * GW, GW)]
            gc = gate_s[rows, pl.ds(2 * DN_W + d * DN_W + g * GW, GW)]
            gc_col = jnp.sum(jnp.where(diag, gc, 0.0), axis=0, keepdims=True)
            g_last = gc[last_row:last_row + 1, :]
            dec = jnp.where(incl, jnp.exp(jnp.minimum(gc - gc_col, 0.0)), 0.0)
            eg = jnp.exp(gc)
            q = q_s[rows, lanes]
            k = k_s[rows, lanes]
            kb = k * beta
            ins.append(dict(rows=rows, lanes=lanes, d=d, strict=strict, dec=dec, g_last=g_last, k=k, kb=kb, q=q,
                            vb=v_s[rows, lanes] * beta, kbg=kb * eg, qg=q * eg,
                            k_dec=k * jnp.exp(g_last - gc)))
        grams = [_dot_nt(jnp.concatenate([i["kb"], i["q"]], axis=0).astype(BF16), bd(i["k"])) for i in ins]
        lms = [jnp.where(i["strict"], gm[:CHUNK] * i["dec"], 0.0) for i, gm in zip(ins, grams)]
        attn = [gm[CHUNK:] * i["dec"] for i, gm in zip(ins, grams)]
        ts = [t.astype(BF16) for t in _packed_unit_inverse(lms, eye, bd_mask)]
        us = [_dot(t, bd(i["vb"])) for i, t in zip(ins, ts)]
        ws = [_dot(t, bd(i["kbg"])) for i, t in zip(ins, ts)]
        states = [st_s[ci] for ci in range(len(chains))]
        ws_qs = [_dot(jnp.concatenate([w, i["qg"]], axis=0).astype(BF16), bd(st))
                 for i, w, st in zip(ins, ws, states)]
        v_bds = [bd(u - x[:CHUNK]) for u, x in zip(us, ws_qs)]
        outs = [x[CHUNK:] + _dot(a.astype(BF16), vb) for x, a, vb in zip(ws_qs, attn, v_bds)]
        upds = [_dot(_heads_transposed(i["k_dec"]).astype(BF16), vb) for i, vb in zip(ins, v_bds)]
        for ci, (i, st, upd, o) in enumerate(zip(ins, states, upds, outs)):
            st_s[ci] = st * jnp.exp(i["g_last"]) + upd
            o_s[i["d"], i["rows"], i["lanes"]] = o
        return carry

    lax.fori_loop(0, n_chunks, chunk_step, 0)

    for blk in range(n_seq * seq // rb):
        rows = slice(blk * rb, (blk + 1) * rb)
        o = o_s[0, rows, :] + o_s[1, rows, :]
        o = o * lax.rsqrt(_head_sumsq(o, DN_DV) * (1.0 / DN_DV) + EPS) * gn_ref[...]
        o_ref[rows, :] = (o * _silu(z_ref[rows, :])).astype(BF16)
    for ci, (s, d, g) in enumerate(chains):
        st = st_s[ci]
        for hb in range(HEADS_PER_GROUP):
            (sf_ref, sb_ref)[d][s, g * HEADS_PER_GROUP + hb] = st[:, hb * DN_DV:(hb + 1) * DN_DV]


def _delta(dqkv, dz, ba, lw, conv_w, layer, seq, n_seq, init=None):
    t = dqkv.shape[0]
    nb = t // seq
    rows = n_seq * seq
    has_init = init is not None
    seq_block = lambda w: pl.BlockSpec((rows, w), lambda b: (b, 0))
    state_block = pl.BlockSpec((n_seq, DN_DK, DN_W), lambda b: (b, 0, 0))
    in_specs = [seq_block(3 * DN_W), seq_block(DN_W), seq_block(LANES),
                pl.BlockSpec((None, DN_CONV, 3 * DN_W), lambda b: (layer, 0, 0), pipeline_mode=pl.Buffered(1)),
                _resident((1, LANES)), _resident((1, LANES)),
                _resident((1, DN_W))]
    args = [dqkv, dz, ba, conv_w, lw["a_log"], lw["dt_bias"], lw["gn"]]
    if has_init:
        in_specs += [state_block, state_block]
        args += list(init)
    final_block = pl.BlockSpec((n_seq, DN_HEADS, DN_DK, DN_DV), lambda b: (b, 0, 0, 0))
    final_shape = jax.ShapeDtypeStruct((nb, DN_HEADS, DN_DK, DN_DV), F32)
    return pl.pallas_call(
        functools.partial(_delta_kernel, seq=seq, n_seq=n_seq, has_init=has_init),
        grid=(nb // n_seq,),
        in_specs=in_specs,
        out_specs=[seq_block(DN_W), final_block, final_block],
        out_shape=[jax.ShapeDtypeStruct((t, DN_W), BF16), final_shape, final_shape],
        scratch_shapes=[pltpu.VMEM((rows, DN_W), F32), pltpu.VMEM((rows, DN_W), F32),
                        pltpu.VMEM((rows, DN_W), F32), pltpu.VMEM((rows, 4 * DN_W), F32),
                        pltpu.VMEM((2, rows, DN_W), F32),
                        pltpu.VMEM((n_seq * 2 * N_GROUPS, DN_DK, GW), F32)],
        compiler_params=_params(),
        name="delta_rule_init" if has_init else "delta_rule",
    )(*args)


def _post_kernel(x_ref, oa_ref, od_ref, gab_ref, mod_ref, g2_ref, wpa_ref, wpb_ref, wo_ref,
                 wg_ref, wu_ref, wd_ref, y_ref):
    _, _, gate1 = _mod_parts(mod_ref, True)
    shift2, scale2, gate2 = _mod_parts(mod_ref, False)
    gab = gab_ref[...]
    merged = (_sigmoid(gab[:, :D_MODEL]) * _dot(oa_ref[...], wpa_ref[...])
              + _sigmoid(gab[:, D_MODEL:]) * _dot(od_ref[...], wpb_ref[...]))
    x1 = x_ref[...] + gate1 * _dot(merged.astype(BF16), wo_ref[...])
    h2 = _rms_mod(x1, g2_ref[...], shift2, scale2).astype(BF16)
    act = _silu(_dot(h2, wg_ref[...])) * _dot(h2, wu_ref[...])
    y_ref[...] = x1 + gate2 * _dot(act.astype(BF16), wd_ref[...])


def _post(x, oa, od, gab, mod, group_of_tile, lw, tm):
    t = x.shape[0]
    row = lambda w: pl.BlockSpec((tm, w), lambda i: (i, 0))
    weights = [lw[k] for k in ("w_pa", "w_pb", "w_o", "w_gate", "w_up", "w_down")]
    return pl.pallas_call(
        _post_kernel,
        grid=(t // tm,),
        in_specs=[row(D_MODEL), row(ATTN_Q_W), row(DN_W), row(2 * D_MODEL),
                  pl.BlockSpec((None, 1, 6 * D_MODEL), lambda i: (group_of_tile(i), 0, 0)),
                  _resident((1, D_MODEL))] + [_resident(w.shape) for w in weights],
        out_specs=row(D_MODEL),
        out_shape=jax.ShapeDtypeStruct((t, D_MODEL), F32),
        compiler_params=_params(),
        name="post_block",
    )(x, oa, od, gab, mod, lw["norm2_g"], *weights)


def _rope_tables(n_tokens):
    quarter = HEAD_DIM // 4
    lane = jnp.arange(LANES)
    d = lane % HEAD_DIM
    inv = ROPE_THETA ** (-(d % quarter).astype(F32) / quarter)
    t = jnp.arange(n_tokens)
    pos = jnp.where(d[None, :] < HEAD_DIM // 2, (t // GRID_W)[:, None], (t % GRID_W)[:, None]).astype(F32)
    ang = pos * inv[None, :]
    sign = jnp.where((d % (2 * quarter)) < quarter, -1.0, 1.0)
    return jnp.cos(ang), jnp.sin(ang) * sign[None, :]


def _pack_states(s):
    b = s.shape[0]
    return s.transpose(0, 2, 1, 3).reshape(b, DN_DK, DN_W)


def _layer_weights(l, w_in, norm1_g, q_norm_g, k_norm_g, conv_w, a_log, dt_bias, dn_norm_g,
                   w_pa, w_pb, w_o, norm2_g, w_gate, w_up, w_down):
    wi = w_in[l]
    o_ba = _QKV_W + 4 * DN_W
    o_g = o_ba + 4 * DN_HEADS
    pad_small = lambda a: jnp.pad(a.reshape(1, -1), ((0, 0), (0, LANES - a.size)))
    return dict(
        w_in=jnp.concatenate([wi[:, :o_ba], wi[:, o_g:], wi[:, o_ba:o_g],
                              jnp.zeros((D_MODEL, LANES - 4 * DN_HEADS), F32)], axis=1).astype(BF16),
        norm1_g=norm1_g[l].reshape(1, D_MODEL),
        norm2_g=norm2_g[l].reshape(1, D_MODEL),
        gq=jnp.tile(q_norm_g[l], N_Q_HEADS).reshape(1, ATTN_Q_W),
        gk=jnp.tile(k_norm_g[l], N_KV_HEADS).reshape(1, ATTN_KV_W),
        a_log=pad_small(jnp.concatenate([jnp.zeros((2 * DN_HEADS,), F32), a_log[l].reshape(-1)])),
        dt_bias=pad_small(jnp.concatenate([jnp.zeros((2 * DN_HEADS,), F32), dt_bias[l].reshape(-1)])),
        gn=jnp.tile(dn_norm_g[l], DN_HEADS).reshape(1, DN_W),
        w_pa=w_pa[l].astype(BF16), w_pb=w_pb[l].astype(BF16), w_o=w_o[l].astype(BF16),
        w_gate=w_gate[l].astype(BF16), w_up=w_up[l].astype(BF16), w_down=w_down[l].astype(BF16),
    )


def kernel(x_prompt, x_sample, cache_k, cache_v, state_fwd, state_bwd, c, c_ctx, w_ada, b_ada, norm1_g, w_in,
           q_norm_g, k_norm_g, conv_w, a_log, dt_bias, dn_norm_g, w_pa, w_pb, w_o, norm2_g, w_gate, w_up, w_down):
    batch, seq, _ = x_prompt.shape
    dec_batch, dec_seq, _ = x_sample.shape
    depth = w_in.shape[0]
    past = cache_k.shape[2]
    assert dec_batch + 1 <= 8 and seq % CHUNK == 0 and dec_seq % CHUNK == 0

    cvecs = jnp.zeros((8, D_MODEL), F32).at[0].set(c_ctx).at[1:1 + dec_batch].set(c)
    rope_tables = _rope_tables(dec_seq)
    tm_in = 512
    tm_post = 256
    ctx_group = lambda i: 0

    def lat_group(tm):
        return lambda i: 1 + i // (dec_seq // tm)

    yp = x_prompt.reshape(batch * seq, D_MODEL)
    ys = x_sample.reshape(dec_batch * dec_seq, D_MODEL)
    ks_out, vs_out, sf_out, sb_out = [], [], [], []
    for l in range(depth):
        lw = _layer_weights(l, w_in, norm1_g, q_norm_g, k_norm_g, conv_w, a_log, dt_bias, dn_norm_g,
                            w_pa, w_pb, w_o, norm2_g, w_gate, w_up, w_down)
        mod = _modulation(cvecs, w_ada[l], b_ada[l])[:1 + dec_batch].reshape(1 + dec_batch, 1, 6 * D_MODEL)

        q, kt, vt, dqkv, dz, ba, gab = _in_proj(yp, mod, ctx_group, lw, tm_in, seq)
        oa = _attention(q, [(kt, vt, seq)], seq, seq)
        od, sf, sb = _delta(dqkv, dz, ba, lw, conv_w, l, seq, 2)
        yp = _post(yp, oa, od, gab, mod, ctx_group, lw, tm_post)
        ks_out.append(kt.transpose(0, 3, 1, 2))
        vs_out.append(vt.transpose(0, 3, 1, 2))
        sf_out.append(sf)
        sb_out.append(sb)

        q, kr, v, dqkv, dz, ba, gab = _in_proj(ys, mod, lat_group(tm_in), lw, tm_in, dec_seq, rope_tables)
        ck = cache_k[:, l].reshape(dec_batch * past, ATTN_KV_W)
        cv = cache_v[:, l].reshape(dec_batch * past, ATTN_KV_W)
        oa = _attention(q, [(ck, cv, past), (kr, v, dec_seq)], dec_seq, 256)
        init = (_pack_states(state_fwd[:, l]), _pack_states(state_bwd[:, l]))
        od, _, _ = _delta(dqkv, dz, ba, lw, conv_w, l, dec_seq, 1, init)
        ys = _post(ys, oa, od, gab, mod, lat_group(tm_post), lw, tm_post)

    return (yp.reshape(batch, seq, D_MODEL), ys.reshape(dec_batch, dec_seq, D_MODEL),
            jnp.stack(ks_out, axis=1), jnp.stack(vs_out, axis=1),
            jnp.stack(sf_out, axis=1), jnp.stack(sb_out, axis=1))
```

```python
import functools
import math

import jax
import jax.numpy as jnp
from jax import lax
from jax.experimental import pallas as pl
from jax.experimental.pallas import tpu as pltpu

D_MODEL = 1024
GRID_W = 64
HEAD_DIM = 64
N_Q_HEADS = 8
N_KV_HEADS = 2
Q_GROUP = N_Q_HEADS // N_KV_HEADS
ATTN_Q_W = N_Q_HEADS * HEAD_DIM
ATTN_KV_W = N_KV_HEADS * HEAD_DIM
ROPE_THETA = 10000.0
DN_HEADS = 8
DN_DK = 64
DN_DV = 64
DN_W = DN_HEADS * DN_DK
DN_CONV = 5
CHUNK = 64
EPS = 1e-6

LANES = 128
SUBLANES = 8
MXU_DIM = 256
VMEM_LIMIT_BYTES = 56 * 1024 * 1024

HEADS_PER_GROUP = MXU_DIM // DN_DK
N_GROUPS = DN_HEADS // HEADS_PER_GROUP
GW = HEADS_PER_GROUP * DN_DK

_QKV_W = ATTN_Q_W + 2 * ATTN_KV_W
_DN_OFF = _QKV_W
_BA_OFF = _DN_OFF + 4 * DN_W
_GATE_OFF = _BA_OFF + 4 * DN_HEADS
_BA_END = _BA_OFF + LANES

F32 = jnp.float32
BF16 = jnp.bfloat16


def _dot(a, b):
    return jnp.dot(a, b, preferred_element_type=F32)


def _dot_nt(a, b):
    return lax.dot_general(a, b, (((1,), (1,)), ((), ())), preferred_element_type=F32)


def _split2(x):
    hi = x.astype(BF16)
    lo = (x - hi.astype(F32)).astype(BF16)
    return hi, lo


def _split3(x):
    hi = x.astype(BF16)
    r = x - hi.astype(F32)
    mid = r.astype(BF16)
    lo = (r - mid.astype(F32)).astype(BF16)
    return hi, mid, lo


def _dot_x3(a, b):
    m = a.shape[0]
    ah, al = _split2(a)
    bh, bl = _split2(b)
    r = _dot(jnp.concatenate([ah, al], axis=0), bh)
    return r[:m] + r[m:] + _dot(ah, bl)


def _dot_exact_lhs(a01, b):
    n = b.shape[1]
    b1, b2, b3 = _split3(b)
    r = _dot(a01, jnp.concatenate([b1, b2, b3], axis=1))
    return r[:, :n] + r[:, n:2 * n] + r[:, 2 * n:]


def _iota(shape, dim):
    return lax.broadcasted_iota(jnp.int32, shape, dim)


def _same_block(shape, width):
    return (_iota(shape, 0) // width) == (_iota(shape, 1) // width)


def _head_sumsq(x, width):
    m, n = x.shape
    slab = min(n, MXU_DIM)
    sel = _same_block((slab, slab), width).astype(BF16)
    outs = []
    for s in range(n // slab):
        xs = x[:, s * slab:(s + 1) * slab]
        outs.append(_dot((xs * xs).astype(BF16), sel))
    return outs[0] if len(outs) == 1 else jnp.concatenate(outs, axis=1)


def _sigmoid(x):
    return 1.0 / (1.0 + jnp.exp(-x))


def _silu(x):
    return x * _sigmoid(x)


def _softplus(x):
    return jnp.maximum(x, 0.0) + jnp.log(1.0 + jnp.exp(-jnp.abs(x)))


def _resident(shape):
    nd = len(shape)
    return pl.BlockSpec(shape, lambda *_: (0,) * nd, pipeline_mode=pl.Buffered(1))


def _params(n_axes=1):
    return pltpu.CompilerParams(dimension_semantics=("arbitrary",) * n_axes,
                                vmem_limit_bytes=VMEM_LIMIT_BYTES)


def _mod_kernel(c_ref, w_ref, b_ref, o_ref):
    c = c_ref[...]
    o_ref[...] = _dot_x3(_silu(c), w_ref[...]) + b_ref[...]


def _modulation(cvecs, w_ada, b_ada):
    n = w_ada.shape[1]
    tn = 1024
    return pl.pallas_call(
        _mod_kernel,
        grid=(n // tn,),
        in_specs=[pl.BlockSpec((SUBLANES, D_MODEL), lambda j: (0, 0)),
                  pl.BlockSpec((D_MODEL, tn), lambda j: (0, j)),
                  pl.BlockSpec((1, tn), lambda j: (0, j))],
        out_specs=pl.BlockSpec((SUBLANES, tn), lambda j: (0, j)),
        out_shape=jax.ShapeDtypeStruct((SUBLANES, n), F32),
        compiler_params=_params(),
        name="adaln_modulation",
    )(cvecs, w_ada, b_ada.reshape(1, n))


def _mod_parts(mod_ref, first):
    m = mod_ref[...]
    base = 0 if first else 3 * D_MODEL
    return (m[:, base:base + D_MODEL], m[:, base + D_MODEL:base + 2 * D_MODEL],
            m[:, base + 2 * D_MODEL:base + 3 * D_MODEL])


def _rms_mod(x, g, shift, scale):
    ms = jnp.mean(x * x, axis=-1, keepdims=True)
    return (x * lax.rsqrt(ms + EPS) * g) * (1.0 + scale) + shift


def _rope(x, cos, sin_signed):
    outs = []
    half = HEAD_DIM // 4
    first_half = (_iota((1, LANES), 1) % (2 * half)) < half
    for s in range(x.shape[1] // LANES):
        xs = x[:, s * LANES:(s + 1) * LANES]
        partner = jnp.where(first_half, pltpu.roll(xs, LANES - half, axis=1), pltpu.roll(xs, half, axis=1))
        outs.append(xs * cos + partner * sin_signed)
    return outs[0] if len(outs) == 1 else jnp.concatenate(outs, axis=1)


def _conv_silu(prev, x, nxt, taps):
    rows = x.shape[0]
    xe = jnp.concatenate([prev, x, nxt], axis=0)
    ne = rows + 2 * SUBLANES
    half = (DN_CONV - 1) // 2
    y = jnp.zeros(x.shape, F32)
    for tap in range(DN_CONV):
        d = tap - half
        sh = xe if d == 0 else pltpu.roll(xe, (ne - d) % ne, axis=0)
        y = y + sh[SUBLANES:SUBLANES + rows] * taps[tap:tap + 1]
    return _silu(y)


def _in_proj_kernel(*refs, rope, halo, seq):
    refs = list(refs)
    x_ref = refs.pop(0)
    xp_ref, xn_ref = (refs.pop(0), refs.pop(0)) if halo else (None, None)
    mod_ref, g_ref, w_ref, gq_ref, gk_ref, cw_ref = (refs.pop(0) for _ in range(6))
    cos_ref, sin_ref = (refs.pop(0), refs.pop(0)) if rope else (None, None)
    q_ref, k_ref, v_ref, cqkv_ref, dz_ref, ba_ref = refs
    tm = x_ref.shape[0]
    shift, scale, _ = _mod_parts(mod_ref, True)
    h = _rms_mod(x_ref[...], g_ref[...], shift, scale).astype(BF16)

    qkv = _dot(h, w_ref[:, 0:_QKV_W])
    aq = qkv[:, :ATTN_Q_W]
    ak = qkv[:, ATTN_Q_W:ATTN_Q_W + ATTN_KV_W]
    av = qkv[:, ATTN_Q_W + ATTN_KV_W:]
    qn = aq * lax.rsqrt(_head_sumsq(aq, HEAD_DIM) * (1.0 / HEAD_DIM) + EPS) * gq_ref[...]
    kn = ak * lax.rsqrt(_head_sumsq(ak, HEAD_DIM) * (1.0 / HEAD_DIM) + EPS) * gk_ref[...]
    if rope:
        cos, sin = cos_ref[...], sin_ref[...]
        qn = _rope(qn, cos, sin)
        k_ref[...] = _rope(kn, cos, sin)
        v_ref[...] = av
    else:
        kt = kn.T
        vt = av.T
        for s in range(tm // seq):
            for hd in range(N_KV_HEADS):
                k_ref[s, hd] = kt[hd * HEAD_DIM:(hd + 1) * HEAD_DIM, s * seq:(s + 1) * seq]
                v_ref[s, hd] = vt[hd * HEAD_DIM:(hd + 1) * HEAD_DIM, s * seq:(s + 1) * seq]
    q_ref[...] = (qn * (HEAD_DIM ** -0.5)).astype(BF16)

    dn = _dot(h, w_ref[:, _DN_OFF:_BA_OFF])
    dz_ref[...] = dn[:, 3 * DN_W:]
    ba_ref[...] = _dot(h, w_ref[:, _BA_OFF:_BA_END])

    zeros = jnp.zeros((SUBLANES, 3 * DN_W), F32)
    if halo:
        tiles_per_seq = seq // tm
        pos = pl.program_id(0) % tiles_per_seq
        xh = jnp.concatenate([xp_ref[...], xn_ref[...]], axis=0)
        hh = _rms_mod(xh, g_ref[...], shift, scale).astype(BF16)
        dh = _dot(hh, w_ref[:, _DN_OFF:_DN_OFF + 3 * DN_W])
        edges = [(jnp.where(pos > 0, dh[:SUBLANES], 0.0), jnp.where(pos < tiles_per_seq - 1, dh[SUBLANES:], 0.0))]
        sub = tm
    else:
        sub = seq
        edges = [(zeros, zeros)] * (tm // seq)
    for s, (prev, nxt) in enumerate(edges):
        rows = slice(s * sub, (s + 1) * sub)
        for part in range(3):
            cols = slice(part * DN_W, (part + 1) * DN_W)
            y = _conv_silu(prev[:, cols], dn[rows, cols], nxt[:, cols], cw_ref[:, cols])
            if part == 0:
                y = y * lax.rsqrt(_head_sumsq(y, DN_DK) + EPS) * (DN_DK ** -0.5)
            elif part == 1:
                y = y * lax.rsqrt(_head_sumsq(y, DN_DK) + EPS)
            cqkv_ref[rows, cols] = y


def _in_proj(x, mod, group_of_tile, lw, conv_w, layer, tm, seq, rope_tables=None):
    t = x.shape[0]
    rope = rope_tables is not None
    halo = tm < seq
    assert tm % seq == 0 or seq % tm == 0
    row = lambda w: pl.BlockSpec((tm, w), lambda i: (i, 0))
    in_specs = [row(D_MODEL)]
    args = [x]
    if halo:
        blocks_per_tile = tm // SUBLANES
        last_block = t // SUBLANES - 1
        in_specs += [pl.BlockSpec((SUBLANES, D_MODEL), lambda i: (jnp.maximum(i * blocks_per_tile - 1, 0), 0)),
                     pl.BlockSpec((SUBLANES, D_MODEL),
                                  lambda i: (jnp.minimum((i + 1) * blocks_per_tile, last_block), 0))]
        args += [x, x]
    in_specs += [pl.BlockSpec((None, 1, 6 * D_MODEL), lambda i: (group_of_tile(i), 0, 0)),
                 _resident((1, D_MODEL)), _resident(lw["w_in"].shape),
                 _resident((1, ATTN_Q_W)), _resident((1, ATTN_KV_W)),
                 pl.BlockSpec((None, DN_CONV, 3 * DN_W), lambda i: (layer, 0, 0), pipeline_mode=pl.Buffered(1))]
    args += [mod, lw["norm1_g"], lw["w_in"], lw["gq"], lw["gk"], conv_w]
    if rope:
        cos, sin = rope_tables
        tiles_per_seq = cos.shape[0] // tm
        in_specs += [pl.BlockSpec((tm, LANES), lambda i: (i % tiles_per_seq, 0))] * 2
        args += [cos, sin]
        kv_spec = row(ATTN_KV_W)
        kv_shape = jax.ShapeDtypeStruct((t, ATTN_KV_W), F32)
    else:
        kv_spec = pl.BlockSpec((tm // seq, N_KV_HEADS, HEAD_DIM, seq), lambda i: (i, 0, 0, 0))
        kv_shape = jax.ShapeDtypeStruct((t // seq, N_KV_HEADS, HEAD_DIM, seq), F32)
    widths = (3 * DN_W, DN_W, LANES)
    return pl.pallas_call(
        functools.partial(_in_proj_kernel, rope=rope, halo=halo, seq=seq),
        grid=(t // tm,),
        in_specs=in_specs,
        out_specs=[row(ATTN_Q_W), kv_spec, kv_spec] + [row(w) for w in widths],
        out_shape=[jax.ShapeDtypeStruct((t, ATTN_Q_W), BF16), kv_shape, kv_shape]
                  + [jax.ShapeDtypeStruct((t, w), F32) for w in widths],
        compiler_params=_params(),
        name="in_proj_rope" if rope else "in_proj",
    )(*args)


def _attn_kernel(*refs, transposed):
    n_parts = len(transposed)
    q_ref = refs[0]
    kv_refs = refs[1:1 + 2 * n_parts]
    o_ref = refs[1 + 2 * n_parts]
    q = q_ref[...]
    tq = q.shape[0]
    outs = [None] * N_Q_HEADS
    for kvh in range(N_KV_HEADS):
        cols = slice(kvh * HEAD_DIM, (kvh + 1) * HEAD_DIM)
        heads = range(kvh * Q_GROUP, (kvh + 1) * Q_GROUP)
        qs = jnp.concatenate([q[:, j * HEAD_DIM:(j + 1) * HEAD_DIM] for j in heads], axis=0)
        ss, vs = [], []
        for p in range(n_parts):
            k_ref, v_ref = kv_refs[2 * p], kv_refs[2 * p + 1]
            if transposed[p]:
                ss.append(_dot(qs, k_ref[kvh].astype(BF16)))
                vs.append(v_ref[kvh].astype(BF16))
            else:
                ss.append(_dot_nt(qs, k_ref[:, cols].astype(BF16)))
                vs.append(v_ref[:, cols].astype(BF16))
        m = functools.reduce(jnp.maximum, [jnp.max(s, axis=-1, keepdims=True) for s in ss])
        ps = [jnp.exp(s - m) for s in ss]
        den = functools.reduce(jnp.add, [jnp.sum(p, axis=-1, keepdims=True) for p in ps])
        acc = functools.reduce(jnp.add, [(_dot_nt if tr else _dot)(p.astype(BF16), v)
                                         for p, v, tr in zip(ps, vs, transposed)])
        o = acc / den
        for g, j in enumerate(heads):
            outs[j] = o[g * tq:(g + 1) * tq]
    o_ref[...] = jnp.concatenate(outs, axis=1).astype(BF16)


def _attention(q, parts, seq_q, tq):
    t = q.shape[0]
    nq = seq_q // tq
    in_specs = [pl.BlockSpec((tq, ATTN_Q_W), lambda b, i: (b * nq + i, 0))]
    args = [q]
    transposed = []
    for k, v, seq_k in parts:
        transposed.append(k.ndim == 4)
        if k.ndim == 4:
            in_specs += [pl.BlockSpec((None, N_KV_HEADS, HEAD_DIM, seq_k), lambda b, i: (b, 0, 0, 0))] * 2
        else:
            in_specs += [pl.BlockSpec((seq_k, ATTN_KV_W), lambda b, i: (b, 0))] * 2
        args += [k, v]
    return pl.pallas_call(
        functools.partial(_attn_kernel, transposed=tuple(transposed)),
        grid=(t // seq_q, nq),
        in_specs=in_specs,
        out_specs=pl.BlockSpec((tq, ATTN_Q_W), lambda b, i: (b * nq + i, 0)),
        out_shape=jax.ShapeDtypeStruct((t, ATTN_Q_W), BF16),
        compiler_params=_params(2),
        name="attention_%dparts" % len(parts),
    )(*args)


def _block_diag(x, mask):
    return jnp.where(mask, jnp.concatenate([x] * HEADS_PER_GROUP, axis=0), jnp.zeros((), x.dtype))


def _heads_transposed(x):
    xt = x.T
    return jnp.concatenate([xt[hb * DN_DK:(hb + 1) * DN_DK] for hb in range(HEADS_PER_GROUP)], axis=1)


def _packed_mm(x, y, mask):
    m = x.shape[0]
    xh, xl = _split2(x)
    yh, yl = _split2(y)
    r = _dot(jnp.concatenate([xh, xl], axis=0), _block_diag(yh, mask))
    return r[:m] + r[m:] + _dot(xh, _block_diag(yl, mask))


def _packed_unit_inverse(lms, eye, mask):
    def mm1(x, y):
        return _dot(x.astype(BF16), _block_diag(y.astype(BF16), mask))

    ns = [-lm for lm in lms]
    rs = [eye + n for n in ns]
    ps = [mm1(n, n) for n in ns]
    steps = int(math.log2(CHUNK)) - 1
    for s in range(steps):
        last = s == steps - 1
        prods = [mm1(r if last else jnp.concatenate([r, p], axis=0), p) for r, p in zip(rs, ps)]
        rs = [r + prod[:CHUNK] for r, prod in zip(rs, prods)]
        if not last:
            ps = [prod[CHUNK:] for prod in prods]
    resid = [eye - r + _packed_mm(n, r, mask) for n, r in zip(ns, rs)]
    return [r + mm1(r, e) for r, e in zip(rs, resid)]


def _delta_kernel(*refs, seq, n_seq, has_init):
    if has_init:
        (x_ref, z_ref, ba_ref, alog_ref, dtb_ref, gn_ref, s0f_ref, s0b_ref,
         o_ref, sf_ref, sb_ref, gate_s, o_s, st_s) = refs
    else:
        (x_ref, z_ref, ba_ref, alog_ref, dtb_ref, gn_ref,
         o_ref, sf_ref, sb_ref, gate_s, o_s, st_s) = refs
    n_chunks = seq // CHUNK
    rb = MXU_DIM
    n_gate = 4 * DN_HEADS

    exp_r = _iota((LANES, 4 * DN_W), 0)
    expand = ((exp_r < 3 * n_gate) & ((_iota((LANES, 4 * DN_W), 1) // DN_DK) == exp_r % n_gate)).astype(BF16)
    lane = _iota((1, LANES), 1)
    blk_r = _iota((rb, rb), 0)
    blk_c = _iota((rb, rb), 1)
    same_chunk = (blk_r // CHUNK) == (blk_c // CHUNK)
    cum_f = (same_chunk & (blk_c <= blk_r)).astype(BF16)
    cum_b = (same_chunk & (blk_c >= blk_r)).astype(BF16)
    for blk in range(n_seq * seq // rb):
        rows = slice(blk * rb, (blk + 1) * rb)
        ba = ba_ref[rows, :]
        decay = -jnp.exp(alog_ref[...]) * _softplus(ba + dtb_ref[...])
        vals = jnp.where(lane < 2 * DN_HEADS, _sigmoid(ba), jnp.where(lane < n_gate, decay, 0.0))
        narrow = jnp.where(lane < 2 * DN_HEADS, vals,
                           jnp.where(lane < 3 * DN_HEADS, _dot_exact_lhs(cum_f, vals),
                                     jnp.where(lane < n_gate, _dot_exact_lhs(cum_b, vals), 0.0)))
        t1, t2, t3 = (t.astype(F32) for t in _split3(narrow))
        stacked = t1 + pltpu.roll(t2, n_gate, axis=1) + pltpu.roll(t3, 2 * n_gate, axis=1)
        gate_s[rows, :] = _dot(stacked.astype(BF16), expand)

    bd_mask = _same_block((GW, GW), DN_DK)
    row = _iota((CHUNK, GW), 0)
    col = _iota((CHUNK, GW), 1) % CHUNK
    diag = row == col
    eye = diag.astype(F32)
    dirs = ((col <= row, col < row, CHUNK - 1), (col >= row, col > row, 0))
    chains = [(s, d, g) for s in range(n_seq) for d in range(2) for g in range(N_GROUPS)]
    for ci, (s, d, g) in enumerate(chains):
        if has_init:
            st_s[ci] = (s0f_ref, s0b_ref)[d][s, :, g * GW:(g + 1) * GW]
        else:
            st_s[ci] = jnp.zeros((DN_DK, GW), F32)

    def bd(x):
        return _block_diag(x.astype(BF16), bd_mask)

    def chunk_step(n, carry):
        ins = []
        for s, d, g in chains:
            incl, strict, last_row = dirs[d]
            c = n if d == 0 else n_chunks - 1 - n
            rows = pl.ds(pl.multiple_of(s * seq + c * CHUNK, CHUNK), CHUNK)
            lanes = slice(g * GW, (g + 1) * GW)
            beta = gate_s[rows, pl.ds(d * DN_W + g * GW, GW)]
            gc = gate_s[rows, pl.ds(2 * DN_W + d * DN_W + g * GW, GW)]
            gc_col = jnp.sum(jnp.where(diag, gc, 0.0), axis=0, keepdims=True)
            g_last = gc[last_row:last_row + 1, :]
            dec = jnp.where(incl, jnp.exp(jnp.minimum(gc - gc_col, 0.0)), 0.0)
            eg = jnp.exp(gc)
            q = x_ref[rows, pl.ds(g * GW, GW)]
            k = x_ref[rows, pl.ds(DN_W + g * GW, GW)]
            v = x_ref[rows, pl.ds(2 * DN_W + g * GW, GW)]
            kb = k * beta
            ins.append(dict(rows=rows, lanes=lanes, d=d, strict=strict, dec=dec, g_last=g_last, k=k, kb=kb, q=q,
                            vb=v * beta, kbg=kb * eg, qg=q * eg, k_dec=k * jnp.exp(g_last - gc)))
        grams = [_dot_nt(jnp.concatenate([i["kb"], i["q"]], axis=0).astype(BF16), bd(i["k"])) for i in ins]
        lms = [jnp.where(i["strict"], gm[:CHUNK] * i["dec"], 0.0) for i, gm in zip(ins, grams)]
        attn = [gm[CHUNK:] * i["dec"] for i, gm in zip(ins, grams)]
        ts = [t.astype(BF16) for t in _packed_unit_inverse(lms, eye, bd_mask)]
        us = [_dot(t, bd(i["vb"])) for i, t in zip(ins, ts)]
        ws = [_dot(t, bd(i["kbg"])) for i, t in zip(ins, ts)]
        states = [st_s[ci] for ci in range(len(chains))]
        ws_qs = [_dot(jnp.concatenate([w, i["qg"]], axis=0).astype(BF16), bd(st))
                 for i, w, st in zip(ins, ws, states)]
        v_bds = [bd(u - x[:CHUNK]) for u, x in zip(us, ws_qs)]
        outs = [x[CHUNK:] + _dot(a.astype(BF16), vb) for x, a, vb in zip(ws_qs, attn, v_bds)]
        upds = [_dot(_heads_transposed(i["k_dec"]).astype(BF16), vb) for i, vb in zip(ins, v_bds)]
        for ci, (i, st, upd, o) in enumerate(zip(ins, states, upds, outs)):
            st_s[ci] = st * jnp.exp(i["g_last"]) + upd
            o_s[i["d"], i["rows"], i["lanes"]] = o
        return carry

    lax.fori_loop(0, n_chunks, chunk_step, 0)

    for blk in range(n_seq * seq // rb):
        rows = slice(blk * rb, (blk + 1) * rb)
        o = o_s[0, rows, :] + o_s[1, rows, :]
        o = o * lax.rsqrt(_head_sumsq(o, DN_DV) * (1.0 / DN_DV) + EPS) * gn_ref[...]
        o_ref[rows, :] = (o * _silu(z_ref[rows, :])).astype(BF16)
    for ci, (s, d, g) in enumerate(chains):
        st = st_s[ci]
        for hb in range(HEADS_PER_GROUP):
            (sf_ref, sb_ref)[d][s, g * HEADS_PER_GROUP + hb] = st[:, hb * DN_DV:(hb + 1) * DN_DV]


def _delta(cqkv, dz, ba, lw, seq, n_seq, init=None):
    t = cqkv.shape[0]
    nb = t // seq
    rows = n_seq * seq
    has_init = init is not None
    seq_block = lambda w: pl.BlockSpec((rows, w), lambda b: (b, 0))
    state_block = pl.BlockSpec((n_seq, DN_DK, DN_W), lambda b: (b, 0, 0))
    in_specs = [seq_block(3 * DN_W), seq_block(DN_W), seq_block(LANES),
                _resident((1, LANES)), _resident((1, LANES)), _resident((1, DN_W))]
    args = [cqkv, dz, ba, lw["a_log"], lw["dt_bias"], lw["gn"]]
    if has_init:
        in_specs += [state_block, state_block]
        args += list(init)
    final_block = pl.BlockSpec((n_seq, DN_HEADS, DN_DK, DN_DV), lambda b: (b, 0, 0, 0))
    final_shape = jax.ShapeDtypeStruct((nb, DN_HEADS, DN_DK, DN_DV), F32)
    return pl.pallas_call(
        functools.partial(_delta_kernel, seq=seq, n_seq=n_seq, has_init=has_init),
        grid=(nb // n_seq,),
        in_specs=in_specs,
        out_specs=[seq_block(DN_W), final_block, final_block],
        out_shape=[jax.ShapeDtypeStruct((t, DN_W), BF16), final_shape, final_shape],
        scratch_shapes=[pltpu.VMEM((rows, 4 * DN_W), F32),
                        pltpu.VMEM((2, rows, DN_W), F32),
                        pltpu.VMEM((n_seq * 2 * N_GROUPS, DN_DK, GW), F32)],
        compiler_params=_params(),
        name="delta_rule_init" if has_init else "delta_rule",
    )(*args)


def _post_kernel(x_ref, oa_ref, od_ref, mod_ref, g1_ref, g2_ref, wgab_ref, wpa_ref, wpb_ref, wo_ref,
                 wg_ref, wu_ref, wd_ref, y_ref):
    shift1, scale1, gate1 = _mod_parts(mod_ref, True)
    shift2, scale2, gate2 = _mod_parts(mod_ref, False)
    x = x_ref[...]
    gab = _dot(_rms_mod(x, g1_ref[...], shift1, scale1).astype(BF16), wgab_ref[...])
    merged = (_sigmoid(gab[:, :D_MODEL]) * _dot(oa_ref[...], wpa_ref[...])
              + _sigmoid(gab[:, D_MODEL:]) * _dot(od_ref[...], wpb_ref[...]))
    x1 = x + gate1 * _dot(merged.astype(BF16), wo_ref[...])
    h2 = _rms_mod(x1, g2_ref[...], shift2, scale2).astype(BF16)
    act = _silu(_dot(h2, wg_ref[...])) * _dot(h2, wu_ref[...])
    y_ref[...] = x1 + gate2 * _dot(act.astype(BF16), wd_ref[...])


def _post(x, oa, od, mod, group_of_tile, lw, tm):
    t = x.shape[0]
    row = lambda w: pl.BlockSpec((tm, w), lambda i: (i, 0))
    weights = [lw[k] for k in ("w_gab", "w_pa", "w_pb", "w_o", "w_gate", "w_up", "w_down")]
    return pl.pallas_call(
        _post_kernel,
        grid=(t // tm,),
        in_specs=[row(D_MODEL), row(ATTN_Q_W), row(DN_W),
                  pl.BlockSpec((None, 1, 6 * D_MODEL), lambda i: (group_of_tile(i), 0, 0)),
                  _resident((1, D_MODEL)), _resident((1, D_MODEL))] + [_resident(w.shape) for w in weights],
        out_specs=row(D_MODEL),
        out_shape=jax.ShapeDtypeStruct((t, D_MODEL), F32),
        compiler_params=_params(),
        name="post_block",
    )(x, oa, od, mod, lw["norm1_g"], lw["norm2_g"], *weights)


def _rope_tables(n_tokens):
    quarter = HEAD_DIM // 4
    lane = jnp.arange(LANES)
    d = lane % HEAD_DIM
    inv = ROPE_THETA ** (-(d % quarter).astype(F32) / quarter)
    t = jnp.arange(n_tokens)
    pos = jnp.where(d[None, :] < HEAD_DIM // 2, (t // GRID_W)[:, None], (t % GRID_W)[:, None]).astype(F32)
    ang = pos * inv[None, :]
    sign = jnp.where((d % (2 * quarter)) < quarter, -1.0, 1.0)
    return jnp.cos(ang), jnp.sin(ang) * sign[None, :]


def _pack_states(s):
    b = s.shape[0]
    return s.transpose(0, 2, 1, 3).reshape(b, DN_DK, DN_W)


def _layer_weights(l, w_in, norm1_g, q_norm_g, k_norm_g, a_log, dt_bias, dn_norm_g,
                   w_pa, w_pb, w_o, norm2_g, w_gate, w_up, w_down):
    wi = w_in[l]
    pad_small = lambda a: jnp.pad(a.reshape(1, -1), ((0, 0), (0, LANES - a.size)))
    return dict(
        w_in=wi[:, :_BA_END].astype(BF16),
        w_gab=wi[:, _GATE_OFF:].astype(BF16),
        norm1_g=norm1_g[l].reshape(1, D_MODEL),
        norm2_g=norm2_g[l].reshape(1, D_MODEL),
        gq=jnp.tile(q_norm_g[l], N_Q_HEADS).reshape(1, ATTN_Q_W),
        gk=jnp.tile(k_norm_g[l], N_KV_HEADS).reshape(1, ATTN_KV_W),
        a_log=pad_small(jnp.concatenate([jnp.zeros((2 * DN_HEADS,), F32), a_log[l].reshape(-1)])),
        dt_bias=pad_small(jnp.concatenate([jnp.zeros((2 * DN_HEADS,), F32), dt_bias[l].reshape(-1)])),
        gn=jnp.tile(dn_norm_g[l], DN_HEADS).reshape(1, DN_W),
        w_pa=w_pa[l].astype(BF16), w_pb=w_pb[l].astype(BF16), w_o=w_o[l].astype(BF16),
        w_gate=w_gate[l].astype(BF16), w_up=w_up[l].astype(BF16), w_down=w_down[l].astype(BF16),
    )


def kernel(x_prompt, x_sample, cache_k, cache_v, state_fwd, state_bwd, c, c_ctx, w_ada, b_ada, norm1_g, w_in,
           q_norm_g, k_norm_g, conv_w, a_log, dt_bias, dn_norm_g, w_pa, w_pb, w_o, norm2_g, w_gate, w_up, w_down):
    batch, seq, _ = x_prompt.shape
    dec_batch, dec_seq, _ = x_sample.shape
    depth = w_in.shape[0]
    past = cache_k.shape[2]
    assert dec_batch + 1 <= SUBLANES and seq % MXU_DIM == 0 and dec_seq % MXU_DIM == 0
    assert batch % 2 == 0 and dec_batch % 2 == 0

    cvecs = jnp.zeros((SUBLANES, D_MODEL), F32).at[0].set(c_ctx).at[1:1 + dec_batch].set(c)
    rope_tables = _rope_tables(dec_seq)
    tm_in = 512
    tm_post = 256
    ctx_group = lambda i: 0

    def lat_group(tm):
        return lambda i: 1 + i // (dec_seq // tm)

    yp = x_prompt.reshape(batch * seq, D_MODEL)
    ys = x_sample.reshape(dec_batch * dec_seq, D_MODEL)
    ks_out, vs_out, sf_out, sb_out = [], [], [], []
    for l in range(depth):
        lw = _layer_weights(l, w_in, norm1_g, q_norm_g, k_norm_g, a_log, dt_bias, dn_norm_g,
                            w_pa, w_pb, w_o, norm2_g, w_gate, w_up, w_down)
        mod = _modulation(cvecs, w_ada[l], b_ada[l])[:1 + dec_batch].reshape(1 + dec_batch, 1, 6 * D_MODEL)

        q, kt, vt, cqkv, dz, ba = _in_proj(yp, mod, ctx_group, lw, conv_w, l, tm_in, seq)
        oa = _attention(q, [(kt, vt, seq)], seq, seq)
        od, sf, sb = _delta(cqkv, dz, ba, lw, seq, 2)
        yp = _post(yp, oa, od, mod, ctx_group, lw, tm_post)
        ks_out.append(kt.transpose(0, 3, 1, 2))
        vs_out.append(vt.transpose(0, 3, 1, 2))
        sf_out.append(sf)
        sb_out.append(sb)

        q, kr, v, cqkv, dz, ba = _in_proj(ys, mod, lat_group(tm_in), lw, conv_w, l, tm_in, dec_seq, rope_tables)
        ck = cache_k[:, l].reshape(dec_batch * past, ATTN_KV_W)
        cv = cache_v[:, l].reshape(dec_batch * past, ATTN_KV_W)
        oa = _attention(q, [(ck, cv, past), (kr, v, dec_seq)], dec_seq, 256)
        init = (_pack_states(state_fwd[:, l]), _pack_states(state_bwd[:, l]))
        od, _, _ = _delta(cqkv, dz, ba, lw, dec_seq, 2, init)
        ys = _post(ys, oa, od, mod, lat_group(tm_post), lw, tm_post)

    return (yp.reshape(batch, seq, D_MODEL), ys.reshape(dec_batch, dec_seq, D_MODEL),
            jnp.stack(ks_out, axis=1), jnp.stack(vs_out, axis=1),
            jnp.stack(sf_out, axis=1), jnp.stack(sb_out, axis=1))
```

```python
import functools
import math

import jax
import jax.numpy as jnp
from jax import lax
from jax.experimental import pallas as pl
from jax.experimental.pallas import tpu as pltpu

D_MODEL = 1024
GRID_W = 64
HEAD_DIM = 64
N_Q_HEADS = 8
N_KV_HEADS = 2
Q_GROUP = N_Q_HEADS // N_KV_HEADS
ATTN_Q_W = N_Q_HEADS * HEAD_DIM
ATTN_KV_W = N_KV_HEADS * HEAD_DIM
ROPE_THETA = 10000.0
DN_HEADS = 8
DN_DK = 64
DN_DV = 64
DN_W = DN_HEADS * DN_DK
DN_CONV = 5
CHUNK = 64
EPS = 1e-6

LANES = 128
SUBLANES = 8
MXU_DIM = 256
VMEM_LIMIT_BYTES = 56 * 1024 * 1024

HEADS_PER_GROUP = MXU_DIM // DN_DK
N_GROUPS = DN_HEADS // HEADS_PER_GROUP
GW = HEADS_PER_GROUP * DN_DK

_QKV_W = ATTN_Q_W + 2 * ATTN_KV_W
_DN_OFF = _QKV_W
_BA_OFF = _DN_OFF + 4 * DN_W
_GATE_OFF = _BA_OFF + 4 * DN_HEADS
_BA_END = _BA_OFF + LANES

F32 = jnp.float32
BF16 = jnp.bfloat16


def _dot(a, b):
    return jnp.dot(a, b, preferred_element_type=F32)


def _dot_nt(a, b):
    return lax.dot_general(a, b, (((1,), (1,)), ((), ())), preferred_element_type=F32)


def _split2(x):
    hi = x.astype(BF16)
    lo = (x - hi.astype(F32)).astype(BF16)
    return hi, lo


def _split3(x):
    hi = x.astype(BF16)
    r = x - hi.astype(F32)
    mid = r.astype(BF16)
    lo = (r - mid.astype(F32)).astype(BF16)
    return hi, mid, lo


def _dot_exact_lhs(a01, b):
    n = b.shape[1]
    b1, b2, b3 = _split3(b)
    r = _dot(a01, jnp.concatenate([b1, b2, b3], axis=1))
    return r[:, :n] + r[:, n:2 * n] + r[:, 2 * n:]


def _iota(shape, dim):
    return lax.broadcasted_iota(jnp.int32, shape, dim)


def _same_block(shape, width):
    return (_iota(shape, 0) // width) == (_iota(shape, 1) // width)


def _head_sumsq(x, width):
    m, n = x.shape
    slab = min(n, MXU_DIM)
    sel = _same_block((slab, slab), width).astype(BF16)
    outs = []
    for s in range(n // slab):
        xs = x[:, s * slab:(s + 1) * slab]
        outs.append(_dot((xs * xs).astype(BF16), sel))
    return outs[0] if len(outs) == 1 else jnp.concatenate(outs, axis=1)


def _sigmoid(x):
    return 1.0 / (1.0 + jnp.exp(-x))


def _silu(x):
    return x * _sigmoid(x)


def _softplus(x):
    return jnp.maximum(x, 0.0) + jnp.log(1.0 + jnp.exp(-jnp.abs(x)))


def _resident(shape):
    nd = len(shape)
    return pl.BlockSpec(shape, lambda *_: (0,) * nd, pipeline_mode=pl.Buffered(1))


def _params(n_axes=1):
    return pltpu.CompilerParams(dimension_semantics=("arbitrary",) * n_axes,
                                vmem_limit_bytes=VMEM_LIMIT_BYTES)


def _mod_kernel(c_ref, w_ref, b_ref, o_ref):
    m = c_ref.shape[0]
    hi, lo = _split2(_silu(c_ref[...]))
    r = _dot(jnp.concatenate([hi, lo], axis=0), w_ref[...].astype(BF16))
    o_ref[...] = r[:m] + r[m:] + b_ref[...]


def _modulation(cvecs, w_ada, b_ada):
    n = w_ada.shape[1]
    tn = 1024
    return pl.pallas_call(
        _mod_kernel,
        grid=(n // tn,),
        in_specs=[pl.BlockSpec((SUBLANES, D_MODEL), lambda j: (0, 0)),
                  pl.BlockSpec((D_MODEL, tn), lambda j: (0, j)),
                  pl.BlockSpec((1, tn), lambda j: (0, j))],
        out_specs=pl.BlockSpec((SUBLANES, tn), lambda j: (0, j)),
        out_shape=jax.ShapeDtypeStruct((SUBLANES, n), F32),
        compiler_params=_params(),
        name="adaln_modulation",
    )(cvecs, w_ada, b_ada.reshape(1, n))


def _mod_parts(mod_ref, first):
    m = mod_ref[...]
    base = 0 if first else 3 * D_MODEL
    return (m[:, base:base + D_MODEL], m[:, base + D_MODEL:base + 2 * D_MODEL],
            m[:, base + 2 * D_MODEL:base + 3 * D_MODEL])


def _rms_mod(x, g, shift, scale):
    ms = jnp.mean(x * x, axis=-1, keepdims=True)
    return (x * lax.rsqrt(ms + EPS) * g) * (1.0 + scale) + shift


def _rope(x, cos, sin_signed):
    outs = []
    half = HEAD_DIM // 4
    first_half = (_iota((1, LANES), 1) % (2 * half)) < half
    for s in range(x.shape[1] // LANES):
        xs = x[:, s * LANES:(s + 1) * LANES]
        partner = jnp.where(first_half, pltpu.roll(xs, LANES - half, axis=1), pltpu.roll(xs, half, axis=1))
        outs.append(xs * cos + partner * sin_signed)
    return outs[0] if len(outs) == 1 else jnp.concatenate(outs, axis=1)


def _conv_silu(prev, x, nxt, taps):
    rows = x.shape[0]
    xe = jnp.concatenate([prev, x, nxt], axis=0)
    ne = rows + 2 * SUBLANES
    half = (DN_CONV - 1) // 2
    y = jnp.zeros(x.shape, F32)
    for tap in range(DN_CONV):
        d = tap - half
        sh = xe if d == 0 else pltpu.roll(xe, (ne - d) % ne, axis=0)
        y = y + sh[SUBLANES:SUBLANES + rows] * taps[tap:tap + 1]
    return _silu(y)


def _in_proj_kernel(*refs, rope, halo, seq):
    refs = list(refs)
    x_ref = refs.pop(0)
    xp_ref, xn_ref = (refs.pop(0), refs.pop(0)) if halo else (None, None)
    mod_ref, g_ref, w_ref, wgab_ref, gq_ref, gk_ref, cw_ref = (refs.pop(0) for _ in range(7))
    cos_ref, sin_ref = (refs.pop(0), refs.pop(0)) if rope else (None, None)
    q_ref, k_ref, v_ref, cqkv_ref, dz_ref, ba_ref, gab_ref = refs
    tm = x_ref.shape[0]
    shift, scale, _ = _mod_parts(mod_ref, True)
    h = _rms_mod(x_ref[...], g_ref[...], shift, scale).astype(BF16)

    qkv = _dot(h, w_ref[:, 0:_QKV_W])
    aq = qkv[:, :ATTN_Q_W]
    ak = qkv[:, ATTN_Q_W:ATTN_Q_W + ATTN_KV_W]
    av = qkv[:, ATTN_Q_W + ATTN_KV_W:]
    qn = aq * lax.rsqrt(_head_sumsq(aq, HEAD_DIM) * (1.0 / HEAD_DIM) + EPS) * gq_ref[...]
    kn = ak * lax.rsqrt(_head_sumsq(ak, HEAD_DIM) * (1.0 / HEAD_DIM) + EPS) * gk_ref[...]
    if rope:
        cos, sin = cos_ref[...], sin_ref[...]
        qn = _rope(qn, cos, sin)
        k_ref[...] = _rope(kn, cos, sin)
        v_ref[...] = av
    else:
        kt = kn.T
        vt = av.T
        for s in range(tm // seq):
            for hd in range(N_KV_HEADS):
                k_ref[s, hd] = kt[hd * HEAD_DIM:(hd + 1) * HEAD_DIM, s * seq:(s + 1) * seq]
                v_ref[s, hd] = vt[hd * HEAD_DIM:(hd + 1) * HEAD_DIM, s * seq:(s + 1) * seq]
    q_ref[...] = (qn * (HEAD_DIM ** -0.5)).astype(BF16)

    dn = _dot(h, w_ref[:, _DN_OFF:_BA_OFF])
    dz_ref[...] = dn[:, 3 * DN_W:]
    ba_ref[...] = _dot(h, w_ref[:, _BA_OFF:_BA_END])
    gab_ref[...] = _dot(h, wgab_ref[...]).astype(BF16)

    zeros = jnp.zeros((SUBLANES, 3 * DN_W), F32)
    if halo:
        tiles_per_seq = seq // tm
        pos = pl.program_id(0) % tiles_per_seq
        xh = jnp.concatenate([xp_ref[...], xn_ref[...]], axis=0)
        hh = _rms_mod(xh, g_ref[...], shift, scale).astype(BF16)
        dh = _dot(hh, w_ref[:, _DN_OFF:_DN_OFF + 3 * DN_W])
        edges = [(jnp.where(pos > 0, dh[:SUBLANES], 0.0), jnp.where(pos < tiles_per_seq - 1, dh[SUBLANES:], 0.0))]
        sub = tm
    else:
        sub = seq
        edges = [(zeros, zeros)] * (tm // seq)
    for s, (prev, nxt) in enumerate(edges):
        rows = slice(s * sub, (s + 1) * sub)
        for part in range(3):
            cols = slice(part * DN_W, (part + 1) * DN_W)
            y = _conv_silu(prev[:, cols], dn[rows, cols], nxt[:, cols], cw_ref[:, cols])
            if part == 0:
                y = y * lax.rsqrt(_head_sumsq(y, DN_DK) + EPS) * (DN_DK ** -0.5)
            elif part == 1:
                y = y * lax.rsqrt(_head_sumsq(y, DN_DK) + EPS)
            cqkv_ref[rows, cols] = y


def _in_proj(x, mod, group_of_tile, lw, conv_w, layer, tm, seq, rope_tables=None):
    t = x.shape[0]
    rope = rope_tables is not None
    halo = tm < seq
    assert tm % seq == 0 or seq % tm == 0
    row = lambda w: pl.BlockSpec((tm, w), lambda i: (i, 0))
    in_specs = [row(D_MODEL)]
    args = [x]
    if halo:
        blocks_per_tile = tm // SUBLANES
        last_block = t // SUBLANES - 1
        in_specs += [pl.BlockSpec((SUBLANES, D_MODEL), lambda i: (jnp.maximum(i * blocks_per_tile - 1, 0), 0)),
                     pl.BlockSpec((SUBLANES, D_MODEL),
                                  lambda i: (jnp.minimum((i + 1) * blocks_per_tile, last_block), 0))]
        args += [x, x]
    in_specs += [pl.BlockSpec((None, 1, 6 * D_MODEL), lambda i: (group_of_tile(i), 0, 0)),
                 _resident((1, D_MODEL)),
                 pl.BlockSpec((None, D_MODEL, _BA_END), lambda i: (layer, 0, 0), pipeline_mode=pl.Buffered(1)),
                 _resident(lw["w_gab"].shape),
                 _resident((1, ATTN_Q_W)), _resident((1, ATTN_KV_W)),
                 pl.BlockSpec((None, DN_CONV, 3 * DN_W), lambda i: (layer, 0, 0), pipeline_mode=pl.Buffered(1))]
    args += [mod, lw["norm1_g"], lw["w_in"], lw["w_gab"], lw["gq"], lw["gk"], conv_w]
    if rope:
        cos, sin = rope_tables
        tiles_per_seq = cos.shape[0] // tm
        in_specs += [pl.BlockSpec((tm, LANES), lambda i: (i % tiles_per_seq, 0))] * 2
        args += [cos, sin]
        kv_spec = row(ATTN_KV_W)
        kv_shape = jax.ShapeDtypeStruct((t, ATTN_KV_W), F32)
    else:
        kv_spec = pl.BlockSpec((tm // seq, N_KV_HEADS, HEAD_DIM, seq), lambda i: (i, 0, 0, 0))
        kv_shape = jax.ShapeDtypeStruct((t // seq, N_KV_HEADS, HEAD_DIM, seq), F32)
    widths = (3 * DN_W, DN_W, LANES)
    return pl.pallas_call(
        functools.partial(_in_proj_kernel, rope=rope, halo=halo, seq=seq),
        grid=(t // tm,),
        in_specs=in_specs,
        out_specs=[row(ATTN_Q_W), kv_spec, kv_spec] + [row(w) for w in widths] + [row(2 * D_MODEL)],
        out_shape=[jax.ShapeDtypeStruct((t, ATTN_Q_W), BF16), kv_shape, kv_shape]
                  + [jax.ShapeDtypeStruct((t, w), F32) for w in widths]
                  + [jax.ShapeDtypeStruct((t, 2 * D_MODEL), BF16)],
        compiler_params=_params(),
        name="in_proj_rope" if rope else "in_proj",
    )(*args)


def _attn_kernel(*refs, transposed):
    n_parts = len(transposed)
    q_ref = refs[0]
    kv_refs = refs[1:1 + 2 * n_parts]
    o_ref = refs[1 + 2 * n_parts]
    q = q_ref[...]
    tq = q.shape[0]
    outs = [None] * N_Q_HEADS
    for kvh in range(N_KV_HEADS):
        cols = slice(kvh * HEAD_DIM, (kvh + 1) * HEAD_DIM)
        heads = range(kvh * Q_GROUP, (kvh + 1) * Q_GROUP)
        qs = jnp.concatenate([q[:, j * HEAD_DIM:(j + 1) * HEAD_DIM] for j in heads], axis=0)
        ss, vs = [], []
        for p in range(n_parts):
            k_ref, v_ref = kv_refs[2 * p], kv_refs[2 * p + 1]
            if transposed[p]:
                ss.append(_dot(qs, k_ref[kvh].astype(BF16)))
                vs.append(v_ref[kvh].astype(BF16))
            else:
                ss.append(_dot_nt(qs, k_ref[:, cols].astype(BF16)))
                vs.append(v_ref[:, cols].astype(BF16))
        m = functools.reduce(jnp.maximum, [jnp.max(s, axis=-1, keepdims=True) for s in ss])
        ps = [jnp.exp(s - m) for s in ss]
        den = functools.reduce(jnp.add, [jnp.sum(p, axis=-1, keepdims=True) for p in ps])
        acc = functools.reduce(jnp.add, [(_dot_nt if tr else _dot)(p.astype(BF16), v)
                                         for p, v, tr in zip(ps, vs, transposed)])
        o = acc / den
        for g, j in enumerate(heads):
            outs[j] = o[g * tq:(g + 1) * tq]
    o_ref[...] = jnp.concatenate(outs, axis=1).astype(BF16)


def _attention(q, parts, seq_q, tq):
    t = q.shape[0]
    nq = seq_q // tq
    in_specs = [pl.BlockSpec((tq, ATTN_Q_W), lambda b, i: (b * nq + i, 0))]
    args = [q]
    transposed = []
    for k, v, seq_k in parts:
        transposed.append(k.ndim == 4)
        if k.ndim == 4:
            in_specs += [pl.BlockSpec((None, N_KV_HEADS, HEAD_DIM, seq_k), lambda b, i: (b, 0, 0, 0))] * 2
        else:
            in_specs += [pl.BlockSpec((seq_k, ATTN_KV_W), lambda b, i: (b, 0))] * 2
        args += [k, v]
    return pl.pallas_call(
        functools.partial(_attn_kernel, transposed=tuple(transposed)),
        grid=(t // seq_q, nq),
        in_specs=in_specs,
        out_specs=pl.BlockSpec((tq, ATTN_Q_W), lambda b, i: (b * nq + i, 0)),
        out_shape=jax.ShapeDtypeStruct((t, ATTN_Q_W), BF16),
        compiler_params=_params(2),
        name="attention_%dparts" % len(parts),
    )(*args)


def _block_diag(x, mask):
    return jnp.where(mask, jnp.concatenate([x] * HEADS_PER_GROUP, axis=0), jnp.zeros((), x.dtype))


def _heads_transposed(x):
    xt = x.T
    return jnp.concatenate([xt[hb * DN_DK:(hb + 1) * DN_DK] for hb in range(HEADS_PER_GROUP)], axis=1)


def _packed_mm(x, y, mask):
    m = x.shape[0]
    xh, xl = _split2(x)
    yh, yl = _split2(y)
    r = _dot(jnp.concatenate([xh, xl], axis=0), _block_diag(yh, mask))
    return r[:m] + r[m:] + _dot(xh, _block_diag(yl, mask))


def _packed_unit_inverse(lms, eye, mask):
    def mm1(x, y):
        return _dot(x, _block_diag(y, mask))

    ns = [-lm for lm in lms]
    rs = [eye + n for n in ns]
    ps = [mm1(nb, nb) for nb in (n.astype(BF16) for n in ns)]
    steps = int(math.log2(CHUNK)) - 1
    for s in range(steps):
        last = s == steps - 1
        pbs = [p.astype(BF16) for p in ps]
        prods = [mm1(r.astype(BF16) if last else jnp.concatenate([r.astype(BF16), pb], axis=0), pb)
                 for r, pb in zip(rs, pbs)]
        rs = [r + prod[:CHUNK] for r, prod in zip(rs, prods)]
        if not last:
            ps = [prod[CHUNK:] for prod in prods]
    resid = [eye - r + _packed_mm(n, r, mask) for n, r in zip(ns, rs)]
    return [r + mm1(r.astype(BF16), e.astype(BF16)) for r, e in zip(rs, resid)]


def _delta_kernel(*refs, seq, n_seq, has_init):
    if has_init:
        (x_ref, z_ref, ba_ref, alog_ref, dtb_ref, gn_ref, s0f_ref, s0b_ref,
         o_ref, sf_ref, sb_ref, gate_s, o_s, st_s) = refs
    else:
        (x_ref, z_ref, ba_ref, alog_ref, dtb_ref, gn_ref,
         o_ref, sf_ref, sb_ref, gate_s, o_s, st_s) = refs
    n_chunks = seq // CHUNK
    rb = MXU_DIM
    n_gate = 4 * DN_HEADS

    exp_r = _iota((LANES, 4 * DN_W), 0)
    expand = ((exp_r < 3 * n_gate) & ((_iota((LANES, 4 * DN_W), 1) // DN_DK) == exp_r % n_gate)).astype(BF16)
    lane = _iota((1, LANES), 1)
    blk_r = _iota((rb, rb), 0)
    blk_c = _iota((rb, rb), 1)
    same_chunk = (blk_r // CHUNK) == (blk_c // CHUNK)
    cum_f = (same_chunk & (blk_c <= blk_r)).astype(BF16)
    cum_b = (same_chunk & (blk_c >= blk_r)).astype(BF16)
    for blk in range(n_seq * seq // rb):
        rows = slice(blk * rb, (blk + 1) * rb)
        ba = ba_ref[rows, :]
        decay = -jnp.exp(alog_ref[...]) * _softplus(ba + dtb_ref[...])
        vals = jnp.where(lane < 2 * DN_HEADS, _sigmoid(ba), jnp.where(lane < n_gate, decay, 0.0))
        narrow = jnp.where(lane < 2 * DN_HEADS, vals,
                           jnp.where(lane < 3 * DN_HEADS, _dot_exact_lhs(cum_f, vals),
                                     jnp.where(lane < n_gate, _dot_exact_lhs(cum_b, vals), 0.0)))
        t1, t2, t3 = (t.astype(F32) for t in _split3(narrow))
        stacked = t1 + pltpu.roll(t2, n_gate, axis=1) + pltpu.roll(t3, 2 * n_gate, axis=1)
        gate_s[rows, :] = _dot(stacked.astype(BF16), expand)

    bd_mask = _same_block((GW, GW), DN_DK)
    row = _iota((CHUNK, GW), 0)
    col = _iota((CHUNK, GW), 1) % CHUNK
    diag = row == col
    eye = diag.astype(F32)
    dirs = ((col <= row, col < row, CHUNK - 1), (col >= row, col > row, 0))
    chains = [(s, d, g) for s in range(n_seq) for d in range(2) for g in range(N_GROUPS)]
    for ci, (s, d, g) in enumerate(chains):
        if has_init:
            st_s[ci] = (s0f_ref, s0b_ref)[d][s, :, g * GW:(g + 1) * GW]
        else:
            st_s[ci] = jnp.zeros((DN_DK, GW), F32)

    def bd(x):
        return _block_diag(x.astype(BF16), bd_mask)

    def chunk_step(n, carry):
        where = []
        for s, d, g in chains:
            c = n if d == 0 else n_chunks - 1 - n
            where.append((pl.ds(pl.multiple_of(s * seq + c * CHUNK, CHUNK), CHUNK), d, g))

        def load(ci, what):
            rows, d, g = where[ci]
            off = {"q": g * GW, "k": DN_W + g * GW, "v": 2 * DN_W + g * GW}
            if what in off:
                return x_ref[rows, pl.ds(off[what], GW)]
            return gate_s[rows, pl.ds((0 if what == "beta" else 2 * DN_W) + d * DN_W + g * GW, GW)]

        n_ch = len(chains)
        grams, decs = [], []
        for ci, (rows, d, g) in enumerate(where):
            incl = dirs[d][0]
            gc = load(ci, "gc")
            gc_col = jnp.sum(jnp.where(diag, gc, 0.0), axis=0, keepdims=True)
            decs.append(jnp.where(incl, jnp.exp(jnp.minimum(gc - gc_col, 0.0)), 0.0))
            k = load(ci, "k")
            lhs = jnp.concatenate([k * load(ci, "beta"), load(ci, "q")], axis=0).astype(BF16)
            grams.append(_dot_nt(lhs, bd(k)))
        lms = [jnp.where(dirs[d][1], gm[:CHUNK] * dec, 0.0) for (_, d, _), gm, dec in zip(where, grams, decs)]
        attn = [(gm[CHUNK:] * dec).astype(BF16) for gm, dec in zip(grams, decs)]
        ts = [t.astype(BF16) for t in _packed_unit_inverse(lms, eye, bd_mask)]
        us, ws = [], []
        for ci, t in enumerate(ts):
            beta = load(ci, "beta")
            us.append(_dot(t, bd(load(ci, "v") * beta)))
            ws.append(_dot(t, bd(load(ci, "k") * beta * jnp.exp(load(ci, "gc")))))
        states = [st_s[ci] for ci in range(n_ch)]
        ws_qs = [_dot(jnp.concatenate([w, load(ci, "q") * jnp.exp(load(ci, "gc"))], axis=0).astype(BF16), bd(st))
                 for ci, (w, st) in enumerate(zip(ws, states))]
        v_bds = [bd(u - x[:CHUNK]) for u, x in zip(us, ws_qs)]
        outs = [x[CHUNK:] + _dot(a, vb) for x, a, vb in zip(ws_qs, attn, v_bds)]
        for ci, (rows, d, g) in enumerate(where):
            gc = load(ci, "gc")
            last_row = dirs[d][2]
            g_last = gc[last_row:last_row + 1, :]
            k_dec = load(ci, "k") * jnp.exp(g_last - gc)
            upd = _dot(_heads_transposed(k_dec).astype(BF16), v_bds[ci])
            st_s[ci] = states[ci] * jnp.exp(g_last) + upd
            o_s[d, rows, g * GW:(g + 1) * GW] = outs[ci]
        return carry

    lax.fori_loop(0, n_chunks, chunk_step, 0)

    for blk in range(n_seq * seq // rb):
        rows = slice(blk * rb, (blk + 1) * rb)
        o = o_s[0, rows, :] + o_s[1, rows, :]
        o = o * lax.rsqrt(_head_sumsq(o, DN_DV) * (1.0 / DN_DV) + EPS) * gn_ref[...]
        o_ref[rows, :] = (o * _silu(z_ref[rows, :])).astype(BF16)
    for ci, (s, d, g) in enumerate(chains):
        st = st_s[ci]
        for hb in range(HEADS_PER_GROUP):
            (sf_ref, sb_ref)[d][s, g * HEADS_PER_GROUP + hb] = st[:, hb * DN_DV:(hb + 1) * DN_DV]


def _delta(cqkv, dz, ba, lw, seq, n_seq, init=None):
    t = cqkv.shape[0]
    nb = t // seq
    rows = n_seq * seq
    has_init = init is not None
    seq_block = lambda w: pl.BlockSpec((rows, w), lambda b: (b, 0))
    state_block = pl.BlockSpec((n_seq, DN_DK, DN_W), lambda b: (b, 0, 0))
    in_specs = [seq_block(3 * DN_W), seq_block(DN_W), seq_block(LANES),
                _resident((1, LANES)), _resident((1, LANES)), _resident((1, DN_W))]
    args = [cqkv, dz, ba, lw["a_log"], lw["dt_bias"], lw["gn"]]
    if has_init:
        in_specs += [state_block, state_block]
        args += list(init)
    final_block = pl.BlockSpec((n_seq, DN_HEADS, DN_DK, DN_DV), lambda b: (b, 0, 0, 0))
    final_shape = jax.ShapeDtypeStruct((nb, DN_HEADS, DN_DK, DN_DV), F32)
    return pl.pallas_call(
        functools.partial(_delta_kernel, seq=seq, n_seq=n_seq, has_init=has_init),
        grid=(nb // n_seq,),
        in_specs=in_specs,
        out_specs=[seq_block(DN_W), final_block, final_block],
        out_shape=[jax.ShapeDtypeStruct((t, DN_W), BF16), final_shape, final_shape],
        scratch_shapes=[pltpu.VMEM((rows, 4 * DN_W), F32),
                        pltpu.VMEM((2, rows, DN_W), F32),
                        pltpu.VMEM((n_seq * 2 * N_GROUPS, DN_DK, GW), F32)],
        compiler_params=_params(),
        name="delta_rule_init" if has_init else "delta_rule",
    )(*args)


def _post_kernel(x_ref, oa_ref, od_ref, gab_ref, mod_ref, g2_ref, wpa_ref, wpb_ref, wo_ref,
                 wg_ref, wu_ref, wd_ref, y_ref):
    _, _, gate1 = _mod_parts(mod_ref, True)
    shift2, scale2, gate2 = _mod_parts(mod_ref, False)
    gab = gab_ref[...].astype(F32)
    merged = (_sigmoid(gab[:, :D_MODEL]) * _dot(oa_ref[...], wpa_ref[...])
              + _sigmoid(gab[:, D_MODEL:]) * _dot(od_ref[...], wpb_ref[...]))
    x1 = x_ref[...] + gate1 * _dot(merged.astype(BF16), wo_ref[...])
    h2 = _rms_mod(x1, g2_ref[...], shift2, scale2).astype(BF16)
    act = _silu(_dot(h2, wg_ref[...])) * _dot(h2, wu_ref[...])
    y_ref[...] = x1 + gate2 * _dot(act.astype(BF16), wd_ref[...])


def _post(x, oa, od, gab, mod, group_of_tile, lw, tm):
    t = x.shape[0]
    row = lambda w: pl.BlockSpec((tm, w), lambda i: (i, 0))
    weights = [lw[k] for k in ("w_pa", "w_pb", "w_o", "w_gate", "w_up", "w_down")]
    return pl.pallas_call(
        _post_kernel,
        grid=(t // tm,),
        in_specs=[row(D_MODEL), row(ATTN_Q_W), row(DN_W), row(2 * D_MODEL),
                  pl.BlockSpec((None, 1, 6 * D_MODEL), lambda i: (group_of_tile(i), 0, 0)),
                  _resident((1, D_MODEL))] + [_resident(w.shape) for w in weights],
        out_specs=row(D_MODEL),
        out_shape=jax.ShapeDtypeStruct((t, D_MODEL), F32),
        compiler_params=_params(),
        name="post_block",
    )(x, oa, od, gab, mod, lw["norm2_g"], *weights)


def _rope_tables(n_tokens):
    quarter = HEAD_DIM // 4
    lane = jnp.arange(LANES)
    d = lane % HEAD_DIM
    inv = ROPE_THETA ** (-(d % quarter).astype(F32) / quarter)
    t = jnp.arange(n_tokens)
    pos = jnp.where(d[None, :] < HEAD_DIM // 2, (t // GRID_W)[:, None], (t % GRID_W)[:, None]).astype(F32)
    ang = pos * inv[None, :]
    sign = jnp.where((d % (2 * quarter)) < quarter, -1.0, 1.0)
    return jnp.cos(ang), jnp.sin(ang) * sign[None, :]


def _pack_states(s):
    b = s.shape[0]
    return s.transpose(0, 2, 1, 3).reshape(b, DN_DK, DN_W)


def _layer_weights(l, w_in, norm1_g, q_norm_g, k_norm_g, a_log, dt_bias, dn_norm_g,
                   w_pa, w_pb, w_o, norm2_g, w_gate, w_up, w_down):
    pad_small = lambda a: jnp.pad(a.reshape(1, -1), ((0, 0), (0, LANES - a.size)))
    return dict(
        w_in=w_in,
        w_gab=w_in[l, :, _GATE_OFF:],
        norm1_g=norm1_g[l].reshape(1, D_MODEL),
        norm2_g=norm2_g[l].reshape(1, D_MODEL),
        gq=jnp.tile(q_norm_g[l], N_Q_HEADS).reshape(1, ATTN_Q_W),
        gk=jnp.tile(k_norm_g[l], N_KV_HEADS).reshape(1, ATTN_KV_W),
        a_log=pad_small(jnp.concatenate([jnp.zeros((2 * DN_HEADS,), F32), a_log[l].reshape(-1)])),
        dt_bias=pad_small(jnp.concatenate([jnp.zeros((2 * DN_HEADS,), F32), dt_bias[l].reshape(-1)])),
        gn=jnp.tile(dn_norm_g[l], DN_HEADS).reshape(1, DN_W),
        w_pa=w_pa[l].astype(BF16), w_pb=w_pb[l].astype(BF16), w_o=w_o[l].astype(BF16),
        w_gate=w_gate[l].astype(BF16), w_up=w_up[l].astype(BF16), w_down=w_down[l].astype(BF16),
    )


def kernel(x_prompt, x_sample, cache_k, cache_v, state_fwd, state_bwd, c, c_ctx, w_ada, b_ada, norm1_g, w_in,
           q_norm_g, k_norm_g, conv_w, a_log, dt_bias, dn_norm_g, w_pa, w_pb, w_o, norm2_g, w_gate, w_up, w_down):
    batch, seq, _ = x_prompt.shape
    dec_batch, dec_seq, _ = x_sample.shape
    depth = w_in.shape[0]
    past = cache_k.shape[2]
    assert dec_batch + 1 <= SUBLANES and seq % MXU_DIM == 0 and dec_seq % MXU_DIM == 0
    assert batch % 2 == 0 and dec_batch % 2 == 0

    cvecs = jnp.zeros((SUBLANES, D_MODEL), F32).at[0].set(c_ctx).at[1:1 + dec_batch].set(c)
    rope_tables = _rope_tables(dec_seq)
    tm_in = 512
    tm_post = 256
    ctx_group = lambda i: 0

    def lat_group(tm):
        return lambda i: 1 + i // (dec_seq // tm)

    yp = x_prompt.reshape(batch * seq, D_MODEL)
    ys = x_sample.reshape(dec_batch * dec_seq, D_MODEL)
    ks_out, vs_out, sf_out, sb_out = [], [], [], []
    w_in = w_in.astype(BF16)
    for l in range(depth):
        lw = _layer_weights(l, w_in, norm1_g, q_norm_g, k_norm_g, a_log, dt_bias, dn_norm_g,
                            w_pa, w_pb, w_o, norm2_g, w_gate, w_up, w_down)
        mod = _modulation(cvecs, w_ada[l], b_ada[l])[:1 + dec_batch].reshape(1 + dec_batch, 1, 6 * D_MODEL)

        q, kt, vt, cqkv, dz, ba, gab = _in_proj(yp, mod, ctx_group, lw, conv_w, l, tm_in, seq)
        oa = _attention(q, [(kt, vt, seq)], seq, seq)
        od, sf, sb = _delta(cqkv, dz, ba, lw, seq, 2)
        yp = _post(yp, oa, od, gab, mod, ctx_group, lw, tm_post)
        ks_out.append(kt.transpose(0, 3, 1, 2))
        vs_out.append(vt.transpose(0, 3, 1, 2))
        sf_out.append(sf)
        sb_out.append(sb)

        q, kr, v, cqkv, dz, ba, gab = _in_proj(ys, mod, lat_group(tm_in), lw, conv_w, l, tm_in, dec_seq,
                                               rope_tables)
        ck = cache_k[:, l].reshape(dec_batch * past, ATTN_KV_W)
        cv = cache_v[:, l].reshape(dec_batch * past, ATTN_KV_W)
        oa = _attention(q, [(ck, cv, past), (kr, v, dec_seq)], dec_seq, 256)
        init = (_pack_states(state_fwd[:, l]), _pack_states(state_bwd[:, l]))
        od, _, _ = _delta(cqkv, dz, ba, lw, dec_seq, 2, init)
        ys = _post(ys, oa, od, gab, mod, lat_group(tm_post), lw, tm_post)

    return (yp.reshape(batch, seq, D_MODEL), ys.reshape(dec_batch, dec_seq, D_MODEL),
            jnp.stack(ks_out, axis=1), jnp.stack(vs_out, axis=1),
            jnp.stack(sf_out, axis=1), jnp.stack(sb_out, axis=1))
```

```python
import functools
import math

import jax
import jax.numpy as jnp
from jax import lax
from jax.experimental import pallas as pl
from jax.experimental.pallas import tpu as pltpu

D_MODEL = 1024
GRID_W = 64
HEAD_DIM = 64
N_Q_HEADS = 8
N_KV_HEADS = 2
Q_GROUP = N_Q_HEADS // N_KV_HEADS
ATTN_Q_W = N_Q_HEADS * HEAD_DIM
ATTN_KV_W = N_KV_HEADS * HEAD_DIM
ROPE_THETA = 10000.0
DN_HEADS = 8
DN_DK = 64
DN_DV = 64
DN_W = DN_HEADS * DN_DK
DN_CONV = 5
CHUNK = 64
EPS = 1e-6
LOG2_E = math.log2(math.e)

LANES = 128
SUBLANES = 8
MXU_DIM = 256
VMEM_LIMIT_BYTES = 56 * 1024 * 1024

HEADS_PER_GROUP = MXU_DIM // DN_DK
N_GROUPS = DN_HEADS // HEADS_PER_GROUP
GW = HEADS_PER_GROUP * DN_DK

_QKV_W = ATTN_Q_W + 2 * ATTN_KV_W
_DN_OFF = _QKV_W
_BA_OFF = _DN_OFF + 4 * DN_W
_GATE_OFF = _BA_OFF + 4 * DN_HEADS
_BA_END = _BA_OFF + LANES

F32 = jnp.float32
BF16 = jnp.bfloat16


def _dot(a, b):
    return jnp.dot(a, b, preferred_element_type=F32)


def _dot_nt(a, b):
    return lax.dot_general(a, b, (((1,), (1,)), ((), ())), preferred_element_type=F32)


def _split2(x):
    hi = x.astype(BF16)
    lo = (x - hi.astype(F32)).astype(BF16)
    return hi, lo


def _split3(x):
    hi = x.astype(BF16)
    r = x - hi.astype(F32)
    mid = r.astype(BF16)
    lo = (r - mid.astype(F32)).astype(BF16)
    return hi, mid, lo


def _dot_exact_lhs(a01, b):
    n = b.shape[1]
    b1, b2, b3 = _split3(b)
    r = _dot(a01, jnp.concatenate([b1, b2, b3], axis=1))
    return r[:, :n] + r[:, n:2 * n] + r[:, 2 * n:]


def _iota(shape, dim):
    return lax.broadcasted_iota(jnp.int32, shape, dim)


def _same_block(shape, width):
    return (_iota(shape, 0) // width) == (_iota(shape, 1) // width)


def _head_sumsq(x, width):
    m, n = x.shape
    slab = min(n, MXU_DIM)
    sel = _same_block((slab, slab), width).astype(BF16)
    outs = []
    for s in range(n // slab):
        xs = x[:, s * slab:(s + 1) * slab]
        outs.append(_dot((xs * xs).astype(BF16), sel))
    return outs[0] if len(outs) == 1 else jnp.concatenate(outs, axis=1)


def _sigmoid(x):
    return 1.0 / (1.0 + jnp.exp(-x))


def _silu(x):
    return x * _sigmoid(x)


def _softplus(x):
    return jnp.maximum(x, 0.0) + jnp.log(1.0 + jnp.exp(-jnp.abs(x)))


def _resident(shape):
    nd = len(shape)
    return pl.BlockSpec(shape, lambda *_: (0,) * nd, pipeline_mode=pl.Buffered(1))


def _params(n_axes=1):
    return pltpu.CompilerParams(dimension_semantics=("arbitrary",) * n_axes,
                                vmem_limit_bytes=VMEM_LIMIT_BYTES)


def _mod_kernel(c_ref, w_ref, b_ref, o_ref):
    m = c_ref.shape[0]
    hi, lo = _split2(_silu(c_ref[...]))
    r = _dot(jnp.concatenate([hi, lo], axis=0), w_ref[...].astype(BF16))
    o_ref[...] = r[:m] + r[m:] + b_ref[...]


def _modulation(cvecs, w_ada, b_ada):
    n = w_ada.shape[1]
    tn = 1024
    return pl.pallas_call(
        _mod_kernel,
        grid=(n // tn,),
        in_specs=[pl.BlockSpec((SUBLANES, D_MODEL), lambda j: (0, 0)),
                  pl.BlockSpec((D_MODEL, tn), lambda j: (0, j)),
                  pl.BlockSpec((1, tn), lambda j: (0, j))],
        out_specs=pl.BlockSpec((SUBLANES, tn), lambda j: (0, j)),
        out_shape=jax.ShapeDtypeStruct((SUBLANES, n), F32),
        compiler_params=_params(),
        name="adaln_modulation",
    )(cvecs, w_ada, b_ada.reshape(1, n))


def _mod_parts(mod_ref, first):
    m = mod_ref[...]
    base = 0 if first else 3 * D_MODEL
    return (m[:, base:base + D_MODEL], m[:, base + D_MODEL:base + 2 * D_MODEL],
            m[:, base + 2 * D_MODEL:base + 3 * D_MODEL])


def _rms_mod(x, g, shift, scale):
    ms = jnp.mean(x * x, axis=-1, keepdims=True)
    return (x * lax.rsqrt(ms + EPS) * g) * (1.0 + scale) + shift


def _rope(x, cos, sin_signed):
    outs = []
    half = HEAD_DIM // 4
    first_half = (_iota((1, LANES), 1) % (2 * half)) < half
    for s in range(x.shape[1] // LANES):
        xs = x[:, s * LANES:(s + 1) * LANES]
        partner = jnp.where(first_half, pltpu.roll(xs, LANES - half, axis=1), pltpu.roll(xs, half, axis=1))
        outs.append(xs * cos + partner * sin_signed)
    return outs[0] if len(outs) == 1 else jnp.concatenate(outs, axis=1)


def _conv_silu(prev, x, nxt, taps):
    rows = x.shape[0]
    xe = jnp.concatenate([prev, x, nxt], axis=0)
    ne = rows + 2 * SUBLANES
    half = (DN_CONV - 1) // 2
    y = jnp.zeros(x.shape, F32)
    for tap in range(DN_CONV):
        d = tap - half
        sh = xe if d == 0 else pltpu.roll(xe, (ne - d) % ne, axis=0)
        y = y + sh[SUBLANES:SUBLANES + rows] * taps[tap:tap + 1]
    return _silu(y)


def _in_proj_kernel(*refs, rope, halo, seq):
    refs = list(refs)
    x_ref = refs.pop(0)
    xp_ref, xn_ref = (refs.pop(0), refs.pop(0)) if halo else (None, None)
    mod_ref, g_ref, w_ref, wgab_ref, gq_ref, gk_ref, cw_ref = (refs.pop(0) for _ in range(7))
    cos_ref, sin_ref = (refs.pop(0), refs.pop(0)) if rope else (None, None)
    q_ref, k_ref, v_ref, cqkv_ref, dz_ref, ba_ref, gab_ref = refs
    tm = x_ref.shape[0]
    shift, scale, _ = _mod_parts(mod_ref, True)
    h = _rms_mod(x_ref[...], g_ref[...], shift, scale).astype(BF16)

    qkv = _dot(h, w_ref[:, 0:_QKV_W])
    aq = qkv[:, :ATTN_Q_W]
    ak = qkv[:, ATTN_Q_W:ATTN_Q_W + ATTN_KV_W]
    av = qkv[:, ATTN_Q_W + ATTN_KV_W:]
    qn = aq * lax.rsqrt(_head_sumsq(aq, HEAD_DIM) * (1.0 / HEAD_DIM) + EPS) * gq_ref[...]
    kn = ak * lax.rsqrt(_head_sumsq(ak, HEAD_DIM) * (1.0 / HEAD_DIM) + EPS) * gk_ref[...]
    if rope:
        cos, sin = cos_ref[...], sin_ref[...]
        qn = _rope(qn, cos, sin)
        k_ref[...] = _rope(kn, cos, sin)
        v_ref[...] = av
    else:
        kt = kn.T
        vt = av.T
        for s in range(tm // seq):
            for hd in range(N_KV_HEADS):
                k_ref[s, hd] = kt[hd * HEAD_DIM:(hd + 1) * HEAD_DIM, s * seq:(s + 1) * seq]
                v_ref[s, hd] = vt[hd * HEAD_DIM:(hd + 1) * HEAD_DIM, s * seq:(s + 1) * seq]
    q_ref[...] = (qn * (HEAD_DIM ** -0.5 * LOG2_E)).astype(BF16)

    dn = _dot(h, w_ref[:, _DN_OFF:_BA_OFF])
    dz_ref[...] = dn[:, 3 * DN_W:]
    ba_ref[...] = _dot(h, w_ref[:, _BA_OFF:_BA_END])
    gab_ref[...] = _dot(h, wgab_ref[...]).astype(BF16)

    zeros = jnp.zeros((SUBLANES, 3 * DN_W), F32)
    if halo:
        tiles_per_seq = seq // tm
        pos = pl.program_id(0) % tiles_per_seq
        xh = jnp.concatenate([xp_ref[...], xn_ref[...]], axis=0)
        hh = _rms_mod(xh, g_ref[...], shift, scale).astype(BF16)
        dh = _dot(hh, w_ref[:, _DN_OFF:_DN_OFF + 3 * DN_W])
        edges = [(jnp.where(pos > 0, dh[:SUBLANES], 0.0), jnp.where(pos < tiles_per_seq - 1, dh[SUBLANES:], 0.0))]
        sub = tm
    else:
        sub = seq
        edges = [(zeros, zeros)] * (tm // seq)
    for s, (prev, nxt) in enumerate(edges):
        rows = slice(s * sub, (s + 1) * sub)
        for part in range(3):
            cols = slice(part * DN_W, (part + 1) * DN_W)
            y = _conv_silu(prev[:, cols], dn[rows, cols], nxt[:, cols], cw_ref[:, cols])
            if part == 0:
                y = y * lax.rsqrt(_head_sumsq(y, DN_DK) + EPS) * (DN_DK ** -0.5)
            elif part == 1:
                y = y * lax.rsqrt(_head_sumsq(y, DN_DK) + EPS)
            cqkv_ref[rows, cols] = y


def _in_proj(x, mod, group_of_tile, lw, conv_w, layer, tm, seq, rope_tables=None):
    t = x.shape[0]
    rope = rope_tables is not None
    halo = tm < seq
    assert tm % seq == 0 or seq % tm == 0
    row = lambda w: pl.BlockSpec((tm, w), lambda i: (i, 0))
    in_specs = [row(D_MODEL)]
    args = [x]
    if halo:
        blocks_per_tile = tm // SUBLANES
        last_block = t // SUBLANES - 1
        in_specs += [pl.BlockSpec((SUBLANES, D_MODEL), lambda i: (jnp.maximum(i * blocks_per_tile - 1, 0), 0)),
                     pl.BlockSpec((SUBLANES, D_MODEL),
                                  lambda i: (jnp.minimum((i + 1) * blocks_per_tile, last_block), 0))]
        args += [x, x]
    in_specs += [pl.BlockSpec((None, 1, 6 * D_MODEL), lambda i: (group_of_tile(i), 0, 0)),
                 _resident((1, D_MODEL)),
                 pl.BlockSpec((None, D_MODEL, _BA_END), lambda i: (layer, 0, 0), pipeline_mode=pl.Buffered(1)),
                 _resident(lw["w_gab"].shape),
                 _resident((1, ATTN_Q_W)), _resident((1, ATTN_KV_W)),
                 pl.BlockSpec((None, DN_CONV, 3 * DN_W), lambda i: (layer, 0, 0), pipeline_mode=pl.Buffered(1))]
    args += [mod, lw["norm1_g"], lw["w_in"], lw["w_gab"], lw["gq"], lw["gk"], conv_w]
    if rope:
        cos, sin = rope_tables
        tiles_per_seq = cos.shape[0] // tm
        in_specs += [pl.BlockSpec((tm, LANES), lambda i: (i % tiles_per_seq, 0))] * 2
        args += [cos, sin]
        kv_spec = row(ATTN_KV_W)
        kv_shape = jax.ShapeDtypeStruct((t, ATTN_KV_W), F32)
    else:
        kv_spec = pl.BlockSpec((tm // seq, N_KV_HEADS, HEAD_DIM, seq), lambda i: (i, 0, 0, 0))
        kv_shape = jax.ShapeDtypeStruct((t // seq, N_KV_HEADS, HEAD_DIM, seq), F32)
    widths = (3 * DN_W, DN_W, LANES)
    return pl.pallas_call(
        functools.partial(_in_proj_kernel, rope=rope, halo=halo, seq=seq),
        grid=(t // tm,),
        in_specs=in_specs,
        out_specs=[row(ATTN_Q_W), kv_spec, kv_spec] + [row(w) for w in widths] + [row(2 * D_MODEL)],
        out_shape=[jax.ShapeDtypeStruct((t, ATTN_Q_W), BF16), kv_shape, kv_shape]
                  + [jax.ShapeDtypeStruct((t, w), F32) for w in widths]
                  + [jax.ShapeDtypeStruct((t, 2 * D_MODEL), BF16)],
        compiler_params=_params(),
        name="in_proj_rope" if rope else "in_proj",
    )(*args)


def _attn_kernel(*refs, transposed, n_seq):
    n_parts = len(transposed)
    q_ref = refs[0]
    kv_refs = refs[1:1 + 2 * n_parts]
    o_ref = refs[1 + 2 * n_parts]
    tq = q_ref.shape[0] // n_seq
    qt = q_ref[...].astype(F32).T.astype(BF16)
    chains = [(s, kvh) for s in range(n_seq) for kvh in range(N_KV_HEADS)]
    qgs = [jnp.concatenate([qt[j * HEAD_DIM:(j + 1) * HEAD_DIM, s * tq:(s + 1) * tq]
                            for j in range(kvh * Q_GROUP, (kvh + 1) * Q_GROUP)], axis=1) for s, kvh in chains]

    def keys(p, s, kvh):
        k_ref = kv_refs[2 * p]
        k = k_ref[s, kvh].T if transposed[p] else k_ref[:, kvh * HEAD_DIM:(kvh + 1) * HEAD_DIM]
        return k.astype(BF16)

    def values_t(p, s, kvh):
        v_ref = kv_refs[2 * p + 1]
        v = v_ref[s, kvh] if transposed[p] else v_ref[:, kvh * HEAD_DIM:(kvh + 1) * HEAD_DIM].T
        return v.astype(BF16)

    ss = [[_dot(keys(p, s, kvh), qg) for p in range(n_parts)] for (s, kvh), qg in zip(chains, qgs)]
    ms = [functools.reduce(jnp.maximum, [jnp.max(sc, axis=0, keepdims=True) for sc in sp]) for sp in ss]
    ps = [[jnp.exp2(sc - m) for sc in sp] for sp, m in zip(ss, ms)]
    dens = [functools.reduce(jnp.add, [jnp.sum(p, axis=0, keepdims=True) for p in pp]) for pp in ps]
    accs = [functools.reduce(jnp.add, [_dot(values_t(p, s, kvh), pr.astype(BF16)) for p, pr in enumerate(pp)])
            for (s, kvh), pp in zip(chains, ps)]
    for s in range(n_seq):
        outs = []
        for kvh in range(N_KV_HEADS):
            o = accs[s * N_KV_HEADS + kvh] / dens[s * N_KV_HEADS + kvh]
            outs += [o[:, g * tq:(g + 1) * tq] for g in range(Q_GROUP)]
        o_ref[s * tq:(s + 1) * tq, :] = jnp.concatenate(outs, axis=0).T.astype(BF16)


def _attention(q, parts, seq_q, tq, n_seq=1):
    t = q.shape[0]
    nq = seq_q // tq
    assert n_seq == 1 or (nq == 1 and all(k.ndim == 4 for k, _, _ in parts))
    in_specs = [pl.BlockSpec((n_seq * tq, ATTN_Q_W), lambda b, i: (b * nq + i, 0))]
    args = [q]
    transposed = []
    for k, v, seq_k in parts:
        transposed.append(k.ndim == 4)
        if k.ndim == 4:
            in_specs += [pl.BlockSpec((n_seq, N_KV_HEADS, HEAD_DIM, seq_k), lambda b, i: (b, 0, 0, 0))] * 2
        else:
            in_specs += [pl.BlockSpec((seq_k, ATTN_KV_W), lambda b, i: (b, 0))] * 2
        args += [k, v]
    return pl.pallas_call(
        functools.partial(_attn_kernel, transposed=tuple(transposed), n_seq=n_seq),
        grid=(t // (seq_q * n_seq), nq),
        in_specs=in_specs,
        out_specs=pl.BlockSpec((n_seq * tq, ATTN_Q_W), lambda b, i: (b * nq + i, 0)),
        out_shape=jax.ShapeDtypeStruct((t, ATTN_Q_W), BF16),
        compiler_params=_params(2),
        name="attention_%dparts" % len(parts),
    )(*args)


def _block_diag(x, mask):
    return jnp.where(mask, jnp.concatenate([x] * HEADS_PER_GROUP, axis=0), jnp.zeros((), x.dtype))


def _heads_transposed(x):
    xt = x.T
    return jnp.concatenate([xt[hb * DN_DK:(hb + 1) * DN_DK] for hb in range(HEADS_PER_GROUP)], axis=1)


def _packed_mm(x, y, mask):
    m = x.shape[0]
    xh, xl = _split2(x)
    yh, yl = _split2(y)
    r = _dot(jnp.concatenate([xh, xl], axis=0), _block_diag(yh, mask))
    return r[:m] + r[m:] + _dot(xh, _block_diag(yl, mask))


def _packed_unit_inverse(lms, eye, mask):
    def mm1(x, y):
        return _dot(x, _block_diag(y, mask))

    ns = [-lm for lm in lms]
    rs = [eye + n for n in ns]
    ps = [mm1(nb, nb) for nb in (n.astype(BF16) for n in ns)]
    steps = int(math.log2(CHUNK)) - 1
    for s in range(steps):
        last = s == steps - 1
        pbs = [p.astype(BF16) for p in ps]
        prods = [mm1(r.astype(BF16) if last else jnp.concatenate([r.astype(BF16), pb], axis=0), pb)
                 for r, pb in zip(rs, pbs)]
        rs = [r + prod[:CHUNK] for r, prod in zip(rs, prods)]
        if not last:
            ps = [prod[CHUNK:] for prod in prods]
    resid = [eye - r + _packed_mm(n, r, mask) for n, r in zip(ns, rs)]
    return [r + mm1(r.astype(BF16), e.astype(BF16)) for r, e in zip(rs, resid)]


def _delta_kernel(*refs, seq, n_seq, has_init):
    if has_init:
        (x_ref, z_ref, ba_ref, alog_ref, dtb_ref, gn_ref, s0f_ref, s0b_ref,
         o_ref, sf_ref, sb_ref, gate_s, o_s, st_s) = refs
    else:
        (x_ref, z_ref, ba_ref, alog_ref, dtb_ref, gn_ref,
         o_ref, sf_ref, sb_ref, gate_s, o_s, st_s) = refs
    n_chunks = seq // CHUNK
    rb = MXU_DIM
    n_gate = 4 * DN_HEADS

    exp_r = _iota((LANES, 4 * DN_W), 0)
    expand = ((exp_r < 3 * n_gate) & ((_iota((LANES, 4 * DN_W), 1) // DN_DK) == exp_r % n_gate)).astype(BF16)
    lane = _iota((1, LANES), 1)
    blk_r = _iota((rb, rb), 0)
    blk_c = _iota((rb, rb), 1)
    same_chunk = (blk_r // CHUNK) == (blk_c // CHUNK)
    cum_f = (same_chunk & (blk_c <= blk_r)).astype(BF16)
    cum_b = (same_chunk & (blk_c >= blk_r)).astype(BF16)
    for blk in range(n_seq * seq // rb):
        rows = slice(blk * rb, (blk + 1) * rb)
        ba = ba_ref[rows, :]
        decay = -jnp.exp(alog_ref[...]) * _softplus(ba + dtb_ref[...])
        vals = jnp.where(lane < 2 * DN_HEADS, _sigmoid(ba), jnp.where(lane < n_gate, decay, 0.0))
        narrow = jnp.where(lane < 2 * DN_HEADS, vals,
                           jnp.where(lane < 3 * DN_HEADS, _dot_exact_lhs(cum_f, vals),
                                     jnp.where(lane < n_gate, _dot_exact_lhs(cum_b, vals), 0.0)))
        t1, t2, t3 = (t.astype(F32) for t in _split3(narrow))
        stacked = t1 + pltpu.roll(t2, n_gate, axis=1) + pltpu.roll(t3, 2 * n_gate, axis=1)
        gate_s[rows, :] = _dot(stacked.astype(BF16), expand)

    bd_mask = _same_block((GW, GW), DN_DK)
    row = _iota((CHUNK, GW), 0)
    col = _iota((CHUNK, GW), 1) % CHUNK
    diag = row == col
    eye = diag.astype(F32)
    dirs = ((col <= row, col < row, CHUNK - 1), (col >= row, col > row, 0))
    chains = [(s, d, g) for s in range(n_seq) for d in range(2) for g in range(N_GROUPS)]
    for ci, (s, d, g) in enumerate(chains):
        if has_init:
            st_s[ci] = (s0f_ref, s0b_ref)[d][s, :, g * GW:(g + 1) * GW]
        else:
            st_s[ci] = jnp.zeros((DN_DK, GW), F32)

    def bd(x):
        return _block_diag(x.astype(BF16), bd_mask)

    def chunk_step(n, carry):
        where = []
        for s, d, g in chains:
            c = n if d == 0 else n_chunks - 1 - n
            where.append((pl.ds(pl.multiple_of(s * seq + c * CHUNK, CHUNK), CHUNK), d, g))

        def load(ci, what):
            rows, d, g = where[ci]
            off = {"q": g * GW, "k": DN_W + g * GW, "v": 2 * DN_W + g * GW}
            if what in off:
                return x_ref[rows, pl.ds(off[what], GW)]
            return gate_s[rows, pl.ds((0 if what == "beta" else 2 * DN_W) + d * DN_W + g * GW, GW)]

        n_ch = len(chains)
        grams, decs = [], []
        for ci, (rows, d, g) in enumerate(where):
            incl = dirs[d][0]
            gc = load(ci, "gc")
            gc_col = jnp.sum(jnp.where(diag, gc, 0.0), axis=0, keepdims=True)
            decs.append(jnp.where(incl, jnp.exp(jnp.minimum(gc - gc_col, 0.0)), 0.0))
            k = load(ci, "k")
            lhs = jnp.concatenate([k * load(ci, "beta"), load(ci, "q")], axis=0).astype(BF16)
            grams.append(_dot_nt(lhs, bd(k)))
        lms = [jnp.where(dirs[d][1], gm[:CHUNK] * dec, 0.0) for (_, d, _), gm, dec in zip(where, grams, decs)]
        attn = [(gm[CHUNK:] * dec).astype(BF16) for gm, dec in zip(grams, decs)]
        ts = [t.astype(BF16) for t in _packed_unit_inverse(lms, eye, bd_mask)]
        us, ws = [], []
        for ci, t in enumerate(ts):
            beta = load(ci, "beta")
            us.append(_dot(t, bd(load(ci, "v") * beta)))
            ws.append(_dot(t, bd(load(ci, "k") * beta * jnp.exp(load(ci, "gc")))))
        states = [st_s[ci] for ci in range(n_ch)]
        ws_qs = [_dot(jnp.concatenate([w, load(ci, "q") * jnp.exp(load(ci, "gc"))], axis=0).astype(BF16), bd(st))
                 for ci, (w, st) in enumerate(zip(ws, states))]
        v_bds = [bd(u - x[:CHUNK]) for u, x in zip(us, ws_qs)]
        outs = [x[CHUNK:] + _dot(a, vb) for x, a, vb in zip(ws_qs, attn, v_bds)]
        for ci, (rows, d, g) in enumerate(where):
            gc = load(ci, "gc")
            last_row = dirs[d][2]
            g_last = gc[last_row:last_row + 1, :]
            k_dec = load(ci, "k") * jnp.exp(g_last - gc)
            upd = _dot(_heads_transposed(k_dec).astype(BF16), v_bds[ci])
            st_s[ci] = states[ci] * jnp.exp(g_last) + upd
            o_s[d, rows, g * GW:(g + 1) * GW] = outs[ci]
        return carry

    lax.fori_loop(0, n_chunks, chunk_step, 0)

    for blk in range(n_seq * seq // rb):
        rows = slice(blk * rb, (blk + 1) * rb)
        o = o_s[0, rows, :] + o_s[1, rows, :]
        o = o * lax.rsqrt(_head_sumsq(o, DN_DV) * (1.0 / DN_DV) + EPS) * gn_ref[...]
        o_ref[rows, :] = (o * _silu(z_ref[rows, :])).astype(BF16)
    for ci, (s, d, g) in enumerate(chains):
        st = st_s[ci]
        for hb in range(HEADS_PER_GROUP):
            (sf_ref, sb_ref)[d][s, g * HEADS_PER_GROUP + hb] = st[:, hb * DN_DV:(hb + 1) * DN_DV]


def _delta(cqkv, dz, ba, lw, seq, n_seq, init=None):
    t = cqkv.shape[0]
    nb = t // seq
    rows = n_seq * seq
    has_init = init is not None
    seq_block = lambda w: pl.BlockSpec((rows, w), lambda b: (b, 0))
    state_block = pl.BlockSpec((n_seq, DN_DK, DN_W), lambda b: (b, 0, 0))
    in_specs = [seq_block(3 * DN_W), seq_block(DN_W), seq_block(LANES),
                _resident((1, LANES)), _resident((1, LANES)), _resident((1, DN_W))]
    args = [cqkv, dz, ba, lw["a_log"], lw["dt_bias"], lw["gn"]]
    if has_init:
        in_specs += [state_block, state_block]
        args += list(init)
    final_block = pl.BlockSpec((n_seq, DN_HEADS, DN_DK, DN_DV), lambda b: (b, 0, 0, 0))
    final_shape = jax.ShapeDtypeStruct((nb, DN_HEADS, DN_DK, DN_DV), F32)
    return pl.pallas_call(
        functools.partial(_delta_kernel, seq=seq, n_seq=n_seq, has_init=has_init),
        grid=(nb // n_seq,),
        in_specs=in_specs,
        out_specs=[seq_block(DN_W), final_block, final_block],
        out_shape=[jax.ShapeDtypeStruct((t, DN_W), BF16), final_shape, final_shape],
        scratch_shapes=[pltpu.VMEM((rows, 4 * DN_W), F32),
                        pltpu.VMEM((2, rows, DN_W), F32),
                        pltpu.VMEM((n_seq * 2 * N_GROUPS, DN_DK, GW), F32)],
        compiler_params=_params(),
        name="delta_rule_init" if has_init else "delta_rule",
    )(*args)


def _post_kernel(x_ref, oa_ref, od_ref, gab_ref, mod_ref, g2_ref, wpa_ref, wpb_ref, wo_ref,
                 wg_ref, wu_ref, wd_ref, y_ref):
    _, _, gate1 = _mod_parts(mod_ref, True)
    shift2, scale2, gate2 = _mod_parts(mod_ref, False)
    gab = gab_ref[...].astype(F32)
    merged = (_sigmoid(gab[:, :D_MODEL]) * _dot(oa_ref[...], wpa_ref[...])
              + _sigmoid(gab[:, D_MODEL:]) * _dot(od_ref[...], wpb_ref[...]))
    x1 = x_ref[...] + gate1 * _dot(merged.astype(BF16), wo_ref[...])
    h2 = _rms_mod(x1, g2_ref[...], shift2, scale2).astype(BF16)
    act = _silu(_dot(h2, wg_ref[...])) * _dot(h2, wu_ref[...])
    y_ref[...] = x1 + gate2 * _dot(act.astype(BF16), wd_ref[...])


def _post(x, oa, od, gab, mod, group_of_tile, lw, tm):
    t = x.shape[0]
    row = lambda w: pl.BlockSpec((tm, w), lambda i: (i, 0))
    weights = [lw[k] for k in ("w_pa", "w_pb", "w_o", "w_gate", "w_up", "w_down")]
    return pl.pallas_call(
        _post_kernel,
        grid=(t // tm,),
        in_specs=[row(D_MODEL), row(ATTN_Q_W), row(DN_W), row(2 * D_MODEL),
                  pl.BlockSpec((None, 1, 6 * D_MODEL), lambda i: (group_of_tile(i), 0, 0)),
                  _resident((1, D_MODEL))] + [_resident(w.shape) for w in weights],
        out_specs=row(D_MODEL),
        out_shape=jax.ShapeDtypeStruct((t, D_MODEL), F32),
        compiler_params=_params(),
        name="post_block",
    )(x, oa, od, gab, mod, lw["norm2_g"], *weights)


def _rope_tables(n_tokens):
    quarter = HEAD_DIM // 4
    lane = jnp.arange(LANES)
    d = lane % HEAD_DIM
    inv = ROPE_THETA ** (-(d % quarter).astype(F32) / quarter)
    t = jnp.arange(n_tokens)
    pos = jnp.where(d[None, :] < HEAD_DIM // 2, (t // GRID_W)[:, None], (t % GRID_W)[:, None]).astype(F32)
    ang = pos * inv[None, :]
    sign = jnp.where((d % (2 * quarter)) < quarter, -1.0, 1.0)
    return jnp.cos(ang), jnp.sin(ang) * sign[None, :]


def _pack_states(s):
    b = s.shape[0]
    return s.transpose(0, 2, 1, 3).reshape(b, DN_DK, DN_W)


def _layer_weights(l, w_in, norm1_g, q_norm_g, k_norm_g, a_log, dt_bias, dn_norm_g,
                   w_pa, w_pb, w_o, norm2_g, w_gate, w_up, w_down):
    pad_small = lambda a: jnp.pad(a.reshape(1, -1), ((0, 0), (0, LANES - a.size)))
    return dict(
        w_in=w_in,
        w_gab=w_in[l, :, _GATE_OFF:],
        norm1_g=norm1_g[l].reshape(1, D_MODEL),
        norm2_g=norm2_g[l].reshape(1, D_MODEL),
        gq=jnp.tile(q_norm_g[l], N_Q_HEADS).reshape(1, ATTN_Q_W),
        gk=jnp.tile(k_norm_g[l], N_KV_HEADS).reshape(1, ATTN_KV_W),
        a_log=pad_small(jnp.concatenate([jnp.zeros((2 * DN_HEADS,), F32), a_log[l].reshape(-1)])),
        dt_bias=pad_small(jnp.concatenate([jnp.zeros((2 * DN_HEADS,), F32), dt_bias[l].reshape(-1)])),
        gn=jnp.tile(dn_norm_g[l], DN_HEADS).reshape(1, DN_W),
        w_pa=w_pa[l].astype(BF16), w_pb=w_pb[l].astype(BF16), w_o=w_o[l].astype(BF16),
        w_gate=w_gate[l].astype(BF16), w_up=w_up[l].astype(BF16), w_down=w_down[l].astype(BF16),
    )


def kernel(x_prompt, x_sample, cache_k, cache_v, state_fwd, state_bwd, c, c_ctx, w_ada, b_ada, norm1_g, w_in,
           q_norm_g, k_norm_g, conv_w, a_log, dt_bias, dn_norm_g, w_pa, w_pb, w_o, norm2_g, w_gate, w_up, w_down):
    batch, seq, _ = x_prompt.shape
    dec_batch, dec_seq, _ = x_sample.shape
    depth = w_in.shape[0]
    past = cache_k.shape[2]
    assert dec_batch + 1 <= SUBLANES and seq % MXU_DIM == 0 and dec_seq % MXU_DIM == 0
    assert batch % 2 == 0 and dec_batch % 2 == 0

    cvecs = jnp.zeros((SUBLANES, D_MODEL), F32).at[0].set(c_ctx).at[1:1 + dec_batch].set(c)
    rope_tables = _rope_tables(dec_seq)
    tm_in = 512
    tm_post = 256
    ctx_group = lambda i: 0

    def lat_group(tm):
        return lambda i: 1 + i // (dec_seq // tm)

    yp = x_prompt.reshape(batch * seq, D_MODEL)
    ys = x_sample.reshape(dec_batch * dec_seq, D_MODEL)
    ks_out, vs_out, sf_out, sb_out = [], [], [], []
    w_in = w_in.astype(BF16)
    for l in range(depth):
        lw = _layer_weights(l, w_in, norm1_g, q_norm_g, k_norm_g, a_log, dt_bias, dn_norm_g,
                            w_pa, w_pb, w_o, norm2_g, w_gate, w_up, w_down)
        mod = _modulation(cvecs, w_ada[l], b_ada[l])[:1 + dec_batch].reshape(1 + dec_batch, 1, 6 * D_MODEL)

        q, kt, vt, cqkv, dz, ba, gab = _in_proj(yp, mod, ctx_group, lw, conv_w, l, tm_in, seq)
        oa = _attention(q, [(kt, vt, seq)], seq, seq, 2)
        od, sf, sb = _delta(cqkv, dz, ba, lw, seq, 2)
        yp = _post(yp, oa, od, gab, mod, ctx_group, lw, tm_post)
        ks_out.append(kt.transpose(0, 3, 1, 2))
        vs_out.append(vt.transpose(0, 3, 1, 2))
        sf_out.append(sf)
        sb_out.append(sb)

        q, kr, v, cqkv, dz, ba, gab = _in_proj(ys, mod, lat_group(tm_in), lw, conv_w, l, tm_in, dec_seq,
                                               rope_tables)
        ck = cache_k[:, l].reshape(dec_batch * past, ATTN_KV_W)
        cv = cache_v[:, l].reshape(dec_batch * past, ATTN_KV_W)
        oa = _attention(q, [(ck, cv, past), (kr, v, dec_seq)], dec_seq, 256)
        init = (_pack_states(state_fwd[:, l]), _pack_states(state_bwd[:, l]))
        od, _, _ = _delta(cqkv, dz, ba, lw, dec_seq, 2, init)
        ys = _post(ys, oa, od, gab, mod, lat_group(tm_post), lw, tm_post)

    return (yp.reshape(batch, seq, D_MODEL), ys.reshape(dec_batch, dec_seq, D_MODEL),
            jnp.stack(ks_out, axis=1), jnp.stack(vs_out, axis=1),
            jnp.stack(sf_out, axis=1), jnp.stack(sb_out, axis=1))
```

```python
import functools
import math

import jax
import jax.numpy as jnp
from jax import lax
from jax.experimental import pallas as pl
from jax.experimental.pallas import tpu as pltpu

D_MODEL = 1024
GRID_W = 64
HEAD_DIM = 64
N_Q_HEADS = 8
N_KV_HEADS = 2
Q_GROUP = N_Q_HEADS // N_KV_HEADS
ATTN_Q_W = N_Q_HEADS * HEAD_DIM
ATTN_KV_W = N_KV_HEADS * HEAD_DIM
ROPE_THETA = 10000.0
DN_HEADS = 8
DN_DK = 64
DN_DV = 64
DN_W = DN_HEADS * DN_DK
DN_CONV = 5
CHUNK = 64
EPS = 1e-6
LOG2_E = math.log2(math.e)

LANES = 128
SUBLANES = 8
MXU_DIM = 256
VMEM_LIMIT_BYTES = 56 * 1024 * 1024

HEADS_PER_GROUP = MXU_DIM // DN_DK
N_GROUPS = DN_HEADS // HEADS_PER_GROUP
GW = HEADS_PER_GROUP * DN_DK

_QKV_W = ATTN_Q_W + 2 * ATTN_KV_W
_DN_OFF = _QKV_W
_BA_OFF = _DN_OFF + 4 * DN_W
_GATE_OFF = _BA_OFF + 4 * DN_HEADS
_BA_END = _BA_OFF + LANES

F32 = jnp.float32
BF16 = jnp.bfloat16


def _dot(a, b):
    return jnp.dot(a, b, preferred_element_type=F32)


def _dot_nt(a, b):
    return lax.dot_general(a, b, (((1,), (1,)), ((), ())), preferred_element_type=F32)


def _split2(x):
    hi = x.astype(BF16)
    lo = (x - hi.astype(F32)).astype(BF16)
    return hi, lo


def _split3(x):
    hi = x.astype(BF16)
    r = x - hi.astype(F32)
    mid = r.astype(BF16)
    lo = (r - mid.astype(F32)).astype(BF16)
    return hi, mid, lo


def _dot_exact_lhs(a01, b):
    n = b.shape[1]
    b1, b2, b3 = _split3(b)
    r = _dot(a01, jnp.concatenate([b1, b2, b3], axis=1))
    return r[:, :n] + r[:, n:2 * n] + r[:, 2 * n:]


def _iota(shape, dim):
    return lax.broadcasted_iota(jnp.int32, shape, dim)


def _same_block(shape, width):
    return (_iota(shape, 0) // width) == (_iota(shape, 1) // width)


def _head_sumsq(x, width):
    m, n = x.shape
    slab = min(n, MXU_DIM)
    sel = _same_block((slab, slab), width).astype(BF16)
    outs = []
    for s in range(n // slab):
        xs = x[:, s * slab:(s + 1) * slab]
        outs.append(_dot((xs * xs).astype(BF16), sel))
    return outs[0] if len(outs) == 1 else jnp.concatenate(outs, axis=1)


def _sigmoid(x):
    return 1.0 / (1.0 + jnp.exp(-x))


def _silu(x):
    return x * _sigmoid(x)


def _softplus(x):
    return jnp.maximum(x, 0.0) + jnp.log(1.0 + jnp.exp(-jnp.abs(x)))


def _resident(shape):
    nd = len(shape)
    return pl.BlockSpec(shape, lambda *_: (0,) * nd, pipeline_mode=pl.Buffered(1))


def _params(n_axes=1):
    return pltpu.CompilerParams(dimension_semantics=("arbitrary",) * n_axes,
                                vmem_limit_bytes=VMEM_LIMIT_BYTES)


def _mod_kernel(c_ref, w_ref, b_ref, o_ref):
    m = c_ref.shape[0]
    hi, lo = _split2(_silu(c_ref[...]))
    r = _dot(jnp.concatenate([hi, lo], axis=0), w_ref[...].astype(BF16))
    o_ref[...] = r[:m] + r[m:] + b_ref[...]


def _modulation(cvecs, w_ada, b_ada):
    n = w_ada.shape[1]
    tn = 1024
    return pl.pallas_call(
        _mod_kernel,
        grid=(n // tn,),
        in_specs=[pl.BlockSpec((SUBLANES, D_MODEL), lambda j: (0, 0)),
                  pl.BlockSpec((D_MODEL, tn), lambda j: (0, j)),
                  pl.BlockSpec((1, tn), lambda j: (0, j))],
        out_specs=pl.BlockSpec((SUBLANES, tn), lambda j: (0, j)),
        out_shape=jax.ShapeDtypeStruct((SUBLANES, n), F32),
        compiler_params=_params(),
        name="adaln_modulation",
    )(cvecs, w_ada, b_ada.reshape(1, n))


def _mod_parts(mod_ref, first):
    m = mod_ref[...]
    base = 0 if first else 3 * D_MODEL
    return (m[:, base:base + D_MODEL], m[:, base + D_MODEL:base + 2 * D_MODEL],
            m[:, base + 2 * D_MODEL:base + 3 * D_MODEL])


def _rms_mod(x, g, shift, scale):
    ms = jnp.mean(x * x, axis=-1, keepdims=True)
    return (x * lax.rsqrt(ms + EPS) * g) * (1.0 + scale) + shift


def _rope(x, cos, sin_signed):
    outs = []
    half = HEAD_DIM // 4
    first_half = (_iota((1, LANES), 1) % (2 * half)) < half
    for s in range(x.shape[1] // LANES):
        xs = x[:, s * LANES:(s + 1) * LANES]
        partner = jnp.where(first_half, pltpu.roll(xs, LANES - half, axis=1), pltpu.roll(xs, half, axis=1))
        outs.append(xs * cos + partner * sin_signed)
    return outs[0] if len(outs) == 1 else jnp.concatenate(outs, axis=1)


def _conv_silu(prev, x, nxt, taps):
    rows = x.shape[0]
    xe = jnp.concatenate([prev, x, nxt], axis=0)
    ne = rows + 2 * SUBLANES
    half = (DN_CONV - 1) // 2
    y = jnp.zeros(x.shape, F32)
    for tap in range(DN_CONV):
        d = tap - half
        sh = xe if d == 0 else pltpu.roll(xe, (ne - d) % ne, axis=0)
        y = y + sh[SUBLANES:SUBLANES + rows] * taps[tap:tap + 1]
    return _silu(y)


def _in_proj_kernel(*refs, rope, halo, seq):
    refs = list(refs)
    x_ref = refs.pop(0)
    xp_ref, xn_ref = (refs.pop(0), refs.pop(0)) if halo else (None, None)
    mod_ref, g_ref, w_ref, wgab_ref, gq_ref, gk_ref, cw_ref = (refs.pop(0) for _ in range(7))
    cos_ref, sin_ref = (refs.pop(0), refs.pop(0)) if rope else (None, None)
    q_ref, k_ref, v_ref, cqkv_ref, dz_ref, ba_ref, gab_ref = refs
    tm = x_ref.shape[0]
    shift, scale, _ = _mod_parts(mod_ref, True)
    h = _rms_mod(x_ref[...], g_ref[...], shift, scale).astype(BF16)

    qkv = _dot(h, w_ref[:, 0:_QKV_W])
    aq = qkv[:, :ATTN_Q_W]
    ak = qkv[:, ATTN_Q_W:ATTN_Q_W + ATTN_KV_W]
    av = qkv[:, ATTN_Q_W + ATTN_KV_W:]
    qn = aq * lax.rsqrt(_head_sumsq(aq, HEAD_DIM) * (1.0 / HEAD_DIM) + EPS) * gq_ref[...]
    kn = ak * lax.rsqrt(_head_sumsq(ak, HEAD_DIM) * (1.0 / HEAD_DIM) + EPS) * gk_ref[...]
    if rope:
        cos, sin = cos_ref[...], sin_ref[...]
        qn = _rope(qn, cos, sin)
        k_ref[...] = _rope(kn, cos, sin)
        v_ref[...] = av
    else:
        kt = kn.T
        vt = av.T
        for s in range(tm // seq):
            for hd in range(N_KV_HEADS):
                k_ref[s, hd] = kt[hd * HEAD_DIM:(hd + 1) * HEAD_DIM, s * seq:(s + 1) * seq]
                v_ref[s, hd] = vt[hd * HEAD_DIM:(hd + 1) * HEAD_DIM, s * seq:(s + 1) * seq]
    q_ref[...] = (qn * (HEAD_DIM ** -0.5 * LOG2_E)).astype(BF16)

    dn = _dot(h, w_ref[:, _DN_OFF:_BA_OFF])
    dz_ref[...] = dn[:, 3 * DN_W:]
    ba_ref[...] = _dot(h, w_ref[:, _BA_OFF:_BA_END])
    gab_ref[...] = _dot(h, wgab_ref[...]).astype(BF16)

    zeros = jnp.zeros((SUBLANES, 3 * DN_W), F32)
    if halo:
        tiles_per_seq = seq // tm
        pos = pl.program_id(0) % tiles_per_seq
        xh = jnp.concatenate([xp_ref[...], xn_ref[...]], axis=0)
        hh = _rms_mod(xh, g_ref[...], shift, scale).astype(BF16)
        dh = _dot(hh, w_ref[:, _DN_OFF:_DN_OFF + 3 * DN_W])
        edges = [(jnp.where(pos > 0, dh[:SUBLANES], 0.0), jnp.where(pos < tiles_per_seq - 1, dh[SUBLANES:], 0.0))]
        sub = tm
    else:
        sub = seq
        edges = [(zeros, zeros)] * (tm // seq)
    for s, (prev, nxt) in enumerate(edges):
        rows = slice(s * sub, (s + 1) * sub)
        for part in range(3):
            cols = slice(part * DN_W, (part + 1) * DN_W)
            y = _conv_silu(prev[:, cols], dn[rows, cols], nxt[:, cols], cw_ref[:, cols])
            if part == 0:
                y = y * lax.rsqrt(_head_sumsq(y, DN_DK) + EPS) * (DN_DK ** -0.5)
            elif part == 1:
                y = y * lax.rsqrt(_head_sumsq(y, DN_DK) + EPS)
            cqkv_ref[rows, cols] = y


def _in_proj(x, mod, group_of_tile, lw, conv_w, layer, tm, seq, rope_tables=None):
    t = x.shape[0]
    rope = rope_tables is not None
    halo = tm < seq
    assert tm % seq == 0 or seq % tm == 0
    row = lambda w: pl.BlockSpec((tm, w), lambda i: (i, 0))
    in_specs = [row(D_MODEL)]
    args = [x]
    if halo:
        blocks_per_tile = tm // SUBLANES
        last_block = t // SUBLANES - 1
        in_specs += [pl.BlockSpec((SUBLANES, D_MODEL), lambda i: (jnp.maximum(i * blocks_per_tile - 1, 0), 0)),
                     pl.BlockSpec((SUBLANES, D_MODEL),
                                  lambda i: (jnp.minimum((i + 1) * blocks_per_tile, last_block), 0))]
        args += [x, x]
    in_specs += [pl.BlockSpec((None, 1, 6 * D_MODEL), lambda i: (group_of_tile(i), 0, 0)),
                 _resident((1, D_MODEL)),
                 pl.BlockSpec((None, D_MODEL, _BA_END), lambda i: (layer, 0, 0), pipeline_mode=pl.Buffered(1)),
                 _resident(lw["w_gab"].shape),
                 _resident((1, ATTN_Q_W)), _resident((1, ATTN_KV_W)),
                 pl.BlockSpec((None, DN_CONV, 3 * DN_W), lambda i: (layer, 0, 0), pipeline_mode=pl.Buffered(1))]
    args += [mod, lw["norm1_g"], lw["w_in"], lw["w_gab"], lw["gq"], lw["gk"], conv_w]
    if rope:
        cos, sin = rope_tables
        tiles_per_seq = cos.shape[0] // tm
        in_specs += [pl.BlockSpec((tm, LANES), lambda i: (i % tiles_per_seq, 0))] * 2
        args += [cos, sin]
        kv_spec = row(ATTN_KV_W)
        kv_shape = jax.ShapeDtypeStruct((t, ATTN_KV_W), F32)
    else:
        kv_spec = pl.BlockSpec((tm // seq, N_KV_HEADS, HEAD_DIM, seq), lambda i: (i, 0, 0, 0))
        kv_shape = jax.ShapeDtypeStruct((t // seq, N_KV_HEADS, HEAD_DIM, seq), F32)
    widths = (3 * DN_W, DN_W, LANES)
    return pl.pallas_call(
        functools.partial(_in_proj_kernel, rope=rope, halo=halo, seq=seq),
        grid=(t // tm,),
        in_specs=in_specs,
        out_specs=[row(ATTN_Q_W), kv_spec, kv_spec] + [row(w) for w in widths] + [row(2 * D_MODEL)],
        out_shape=[jax.ShapeDtypeStruct((t, ATTN_Q_W), BF16), kv_shape, kv_shape]
                  + [jax.ShapeDtypeStruct((t, w), F32) for w in widths]
                  + [jax.ShapeDtypeStruct((t, 2 * D_MODEL), BF16)],
        compiler_params=_params(),
        name="in_proj_rope" if rope else "in_proj",
    )(*args)


def _attn_kernel(*refs, transposed, n_seq):
    n_parts = len(transposed)
    q_ref = refs[0]
    kv_refs = refs[1:1 + 2 * n_parts]
    o_ref = refs[1 + 2 * n_parts]
    tq = q_ref.shape[0] // n_seq
    qt = q_ref[...].astype(F32).T.astype(BF16)
    chains = [(s, kvh) for s in range(n_seq) for kvh in range(N_KV_HEADS)]
    qgs = [jnp.concatenate([qt[j * HEAD_DIM:(j + 1) * HEAD_DIM, s * tq:(s + 1) * tq]
                            for j in range(kvh * Q_GROUP, (kvh + 1) * Q_GROUP)], axis=1) for s, kvh in chains]

    def keys(p, s, kvh):
        k_ref = kv_refs[2 * p]
        k = k_ref[s, kvh].T if transposed[p] else k_ref[:, kvh * HEAD_DIM:(kvh + 1) * HEAD_DIM]
        return k.astype(BF16)

    def values_t(p, s, kvh):
        v_ref = kv_refs[2 * p + 1]
        v = v_ref[s, kvh] if transposed[p] else v_ref[:, kvh * HEAD_DIM:(kvh + 1) * HEAD_DIM].T
        return v.astype(BF16)

    ss = [[_dot(keys(p, s, kvh), qg) for p in range(n_parts)] for (s, kvh), qg in zip(chains, qgs)]
    ms = [functools.reduce(jnp.maximum, [jnp.max(sc, axis=0, keepdims=True) for sc in sp]) for sp in ss]
    ps = [[jnp.exp2(sc - m) for sc in sp] for sp, m in zip(ss, ms)]
    dens = [functools.reduce(jnp.add, [jnp.sum(p, axis=0, keepdims=True) for p in pp]) for pp in ps]
    accs = [functools.reduce(jnp.add, [_dot(values_t(p, s, kvh), pr.astype(BF16)) for p, pr in enumerate(pp)])
            for (s, kvh), pp in zip(chains, ps)]
    for s in range(n_seq):
        outs = []
        for kvh in range(N_KV_HEADS):
            o = accs[s * N_KV_HEADS + kvh] / dens[s * N_KV_HEADS + kvh]
            outs += [o[:, g * tq:(g + 1) * tq] for g in range(Q_GROUP)]
        o_ref[s * tq:(s + 1) * tq, :] = jnp.concatenate(outs, axis=0).T.astype(BF16)


def _attention(q, parts, seq_q, tq, n_seq=1):
    t = q.shape[0]
    nq = seq_q // tq
    assert n_seq == 1 or (nq == 1 and all(k.ndim == 4 for k, _, _ in parts))
    in_specs = [pl.BlockSpec((n_seq * tq, ATTN_Q_W), lambda b, i: (b * nq + i, 0))]
    args = [q]
    transposed = []
    for k, v, seq_k in parts:
        transposed.append(k.ndim == 4)
        if k.ndim == 4:
            in_specs += [pl.BlockSpec((n_seq, N_KV_HEADS, HEAD_DIM, seq_k), lambda b, i: (b, 0, 0, 0))] * 2
        else:
            in_specs += [pl.BlockSpec((seq_k, ATTN_KV_W), lambda b, i: (b, 0))] * 2
        args += [k, v]
    return pl.pallas_call(
        functools.partial(_attn_kernel, transposed=tuple(transposed), n_seq=n_seq),
        grid=(t // (seq_q * n_seq), nq),
        in_specs=in_specs,
        out_specs=pl.BlockSpec((n_seq * tq, ATTN_Q_W), lambda b, i: (b * nq + i, 0)),
        out_shape=jax.ShapeDtypeStruct((t, ATTN_Q_W), BF16),
        compiler_params=_params(2),
        name="attention_%dparts" % len(parts),
    )(*args)


def _block_diag(x, mask):
    reps = LANES // DN_DK
    return [jnp.where(mask, jnp.concatenate([x[:, c * LANES:(c + 1) * LANES]] * reps, axis=0),
                      jnp.zeros((), x.dtype)) for c in range(x.shape[1] // LANES)]


def _bdot(x, tiles, nt=False):
    op = _dot_nt if nt else _dot
    return jnp.concatenate([op(x[:, c * LANES:(c + 1) * LANES], t) for c, t in enumerate(tiles)], axis=1)


def _heads_transposed(x):
    xt = x.T
    return jnp.concatenate([xt[hb * DN_DK:(hb + 1) * DN_DK] for hb in range(HEADS_PER_GROUP)], axis=1)


def _packed_unit_inverse(lms, eye, mask):
    def mm1(x, y):
        return _bdot(x, _block_diag(y, mask))

    ns = [-lm for lm in lms]
    rs = [eye + n for n in ns]
    ps = [mm1(nb, nb) for nb in (n.astype(BF16) for n in ns)]
    steps = int(math.log2(CHUNK)) - 1
    for s in range(steps):
        last = s == steps - 1
        pbs = [p.astype(BF16) for p in ps]
        prods = [mm1(r.astype(BF16) if last else jnp.concatenate([r.astype(BF16), pb], axis=0), pb)
                 for r, pb in zip(rs, pbs)]
        rs = [r + prod[:CHUNK] for r, prod in zip(rs, prods)]
        if not last:
            ps = [prod[CHUNK:] for prod in prods]
    xbs = [r.astype(BF16) for r in rs]
    xs = [xb.astype(F32) for xb in xbs]
    resid = []
    for n, x, xb in zip(ns, xs, xbs):
        nh, nl = _split2(n)
        prod = _bdot(jnp.concatenate([nh, nl], axis=0), _block_diag(xb, mask))
        resid.append(eye - x + prod[:CHUNK] + prod[CHUNK:])
    return [x + mm1(xb, e.astype(BF16)) for x, xb, e in zip(xs, xbs, resid)]


def _delta_kernel(*refs, seq, n_seq, has_init):
    if has_init:
        (x_ref, z_ref, ba_ref, alog_ref, dtb_ref, gn_ref, s0f_ref, s0b_ref,
         o_ref, sf_ref, sb_ref, gate_s, o_s, st_s) = refs
    else:
        (x_ref, z_ref, ba_ref, alog_ref, dtb_ref, gn_ref,
         o_ref, sf_ref, sb_ref, gate_s, o_s, st_s) = refs
    n_chunks = seq // CHUNK
    rb = MXU_DIM
    n_gate = 4 * DN_HEADS

    exp_r = _iota((LANES, 4 * DN_W), 0)
    expand = ((exp_r < 3 * n_gate) & ((_iota((LANES, 4 * DN_W), 1) // DN_DK) == exp_r % n_gate)).astype(BF16)
    lane = _iota((1, LANES), 1)
    blk_r = _iota((rb, rb), 0)
    blk_c = _iota((rb, rb), 1)
    same_chunk = (blk_r // CHUNK) == (blk_c // CHUNK)
    cum_f = (same_chunk & (blk_c <= blk_r)).astype(BF16)
    cum_b = (same_chunk & (blk_c >= blk_r)).astype(BF16)
    for blk in range(n_seq * seq // rb):
        rows = slice(blk * rb, (blk + 1) * rb)
        ba = ba_ref[rows, :]
        decay = -jnp.exp(alog_ref[...]) * _softplus(ba + dtb_ref[...])
        vals = jnp.where(lane < 2 * DN_HEADS, _sigmoid(ba), jnp.where(lane < n_gate, decay, 0.0))
        narrow = jnp.where(lane < 2 * DN_HEADS, vals,
                           jnp.where(lane < 3 * DN_HEADS, _dot_exact_lhs(cum_f, vals),
                                     jnp.where(lane < n_gate, _dot_exact_lhs(cum_b, vals), 0.0)))
        t1, t2, t3 = (t.astype(F32) for t in _split3(narrow))
        stacked = t1 + pltpu.roll(t2, n_gate, axis=1) + pltpu.roll(t3, 2 * n_gate, axis=1)
        gate_s[rows, :] = _dot(stacked.astype(BF16), expand)

    bd_mask = _same_block((LANES, LANES), DN_DK)
    row = _iota((CHUNK, GW), 0)
    col = _iota((CHUNK, GW), 1) % CHUNK
    diag = row == col
    eye = diag.astype(F32)
    dirs = ((col <= row, col < row, CHUNK - 1), (col >= row, col > row, 0))
    chains = [(s, d, g) for s in range(n_seq) for d in range(2) for g in range(N_GROUPS)]
    for ci, (s, d, g) in enumerate(chains):
        if has_init:
            st_s[ci] = (s0f_ref, s0b_ref)[d][s, :, g * GW:(g + 1) * GW]
        else:
            st_s[ci] = jnp.zeros((DN_DK, GW), F32)

    def bd(x):
        return _block_diag(x.astype(BF16), bd_mask)

    def chunk_step(n, carry):
        where = []
        for s, d, g in chains:
            c = n if d == 0 else n_chunks - 1 - n
            where.append((pl.ds(pl.multiple_of(s * seq + c * CHUNK, CHUNK), CHUNK), d, g))

        def load(ci, what):
            rows, d, g = where[ci]
            off = {"q": g * GW, "k": DN_W + g * GW, "v": 2 * DN_W + g * GW}
            if what in off:
                return x_ref[rows, pl.ds(off[what], GW)]
            return gate_s[rows, pl.ds((0 if what == "beta" else 2 * DN_W) + d * DN_W + g * GW, GW)]

        n_ch = len(chains)
        grams, decs = [], []
        for ci, (rows, d, g) in enumerate(where):
            incl = dirs[d][0]
            gc = load(ci, "gc")
            gc_col = jnp.sum(jnp.where(diag, gc, 0.0), axis=0, keepdims=True)
            decs.append(jnp.where(incl, jnp.exp(jnp.minimum(gc - gc_col, 0.0)), 0.0))
            k = load(ci, "k")
            lhs = jnp.concatenate([k * load(ci, "beta"), load(ci, "q")], axis=0).astype(BF16)
            grams.append(_bdot(lhs, bd(k), nt=True))
        lms = [jnp.where(dirs[d][1], gm[:CHUNK] * dec, 0.0) for (_, d, _), gm, dec in zip(where, grams, decs)]
        attn = [(gm[CHUNK:] * dec).astype(BF16) for gm, dec in zip(grams, decs)]
        ts = [t.astype(BF16) for t in _packed_unit_inverse(lms, eye, bd_mask)]
        us, ws = [], []
        for ci, t in enumerate(ts):
            beta = load(ci, "beta")
            us.append(_bdot(t, bd(load(ci, "v") * beta)))
            ws.append(_bdot(t, bd(load(ci, "k") * beta * jnp.exp(load(ci, "gc")))))
        states = [st_s[ci] for ci in range(n_ch)]
        ws_qs = [_bdot(jnp.concatenate([w, load(ci, "q") * jnp.exp(load(ci, "gc"))], axis=0).astype(BF16), bd(st))
                 for ci, (w, st) in enumerate(zip(ws, states))]
        v_bds = [bd(u - x[:CHUNK]) for u, x in zip(us, ws_qs)]
        outs = [x[CHUNK:] + _bdot(a, vb) for x, a, vb in zip(ws_qs, attn, v_bds)]
        for ci, (rows, d, g) in enumerate(where):
            gc = load(ci, "gc")
            last_row = dirs[d][2]
            g_last = gc[last_row:last_row + 1, :]
            k_dec = load(ci, "k") * jnp.exp(g_last - gc)
            upd = _bdot(_heads_transposed(k_dec).astype(BF16), v_bds[ci])
            st_s[ci] = states[ci] * jnp.exp(g_last) + upd
            o_s[d, rows, g * GW:(g + 1) * GW] = outs[ci]
        return carry

    lax.fori_loop(0, n_chunks, chunk_step, 0)

    for blk in range(n_seq * seq // rb):
        rows = slice(blk * rb, (blk + 1) * rb)
        o = o_s[0, rows, :] + o_s[1, rows, :]
        o = o * lax.rsqrt(_head_sumsq(o, DN_DV) * (1.0 / DN_DV) + EPS) * gn_ref[...]
        o_ref[rows, :] = (o * _silu(z_ref[rows, :])).astype(BF16)
    for ci, (s, d, g) in enumerate(chains):
        st = st_s[ci]
        for hb in range(HEADS_PER_GROUP):
            (sf_ref, sb_ref)[d][s, g * HEADS_PER_GROUP + hb] = st[:, hb * DN_DV:(hb + 1) * DN_DV]


def _delta(cqkv, dz, ba, lw, seq, n_seq, init=None):
    t = cqkv.shape[0]
    nb = t // seq
    rows = n_seq * seq
    has_init = init is not None
    seq_block = lambda w: pl.BlockSpec((rows, w), lambda b: (b, 0))
    state_block = pl.BlockSpec((n_seq, DN_DK, DN_W), lambda b: (b, 0, 0))
    in_specs = [seq_block(3 * DN_W), seq_block(DN_W), seq_block(LANES),
                _resident((1, LANES)), _resident((1, LANES)), _resident((1, DN_W))]
    args = [cqkv, dz, ba, lw["a_log"], lw["dt_bias"], lw["gn"]]
    if has_init:
        in_specs += [state_block, state_block]
        args += list(init)
    final_block = pl.BlockSpec((n_seq, DN_HEADS, DN_DK, DN_DV), lambda b: (b, 0, 0, 0))
    final_shape = jax.ShapeDtypeStruct((nb, DN_HEADS, DN_DK, DN_DV), F32)
    return pl.pallas_call(
        functools.partial(_delta_kernel, seq=seq, n_seq=n_seq, has_init=has_init),
        grid=(nb // n_seq,),
        in_specs=in_specs,
        out_specs=[seq_block(DN_W), final_block, final_block],
        out_shape=[jax.ShapeDtypeStruct((t, DN_W), BF16), final_shape, final_shape],
        scratch_shapes=[pltpu.VMEM((rows, 4 * DN_W), F32),
                        pltpu.VMEM((2, rows, DN_W), F32),
                        pltpu.VMEM((n_seq * 2 * N_GROUPS, DN_DK, GW), F32)],
        compiler_params=_params(),
        name="delta_rule_init" if has_init else "delta_rule",
    )(*args)


def _post_kernel(x_ref, oa_ref, od_ref, gab_ref, mod_ref, g2_ref, wpa_ref, wpb_ref, wo_ref,
                 wg_ref, wu_ref, wd_ref, y_ref):
    _, _, gate1 = _mod_parts(mod_ref, True)
    shift2, scale2, gate2 = _mod_parts(mod_ref, False)
    gab = gab_ref[...].astype(F32)
    merged = (_sigmoid(gab[:, :D_MODEL]) * _dot(oa_ref[...], wpa_ref[...])
              + _sigmoid(gab[:, D_MODEL:]) * _dot(od_ref[...], wpb_ref[...]))
    x1 = x_ref[...] + gate1 * _dot(merged.astype(BF16), wo_ref[...])
    h2 = _rms_mod(x1, g2_ref[...], shift2, scale2).astype(BF16)
    act = _silu(_dot(h2, wg_ref[...])) * _dot(h2, wu_ref[...])
    y_ref[...] = x1 + gate2 * _dot(act.astype(BF16), wd_ref[...])


def _post(x, oa, od, gab, mod, group_of_tile, lw, tm):
    t = x.shape[0]
    row = lambda w: pl.BlockSpec((tm, w), lambda i: (i, 0))
    weights = [lw[k] for k in ("w_pa", "w_pb", "w_o", "w_gate", "w_up", "w_down")]
    return pl.pallas_call(
        _post_kernel,
        grid=(t // tm,),
        in_specs=[row(D_MODEL), row(ATTN_Q_W), row(DN_W), row(2 * D_MODEL),
                  pl.BlockSpec((None, 1, 6 * D_MODEL), lambda i: (group_of_tile(i), 0, 0)),
                  _resident((1, D_MODEL))] + [_resident(w.shape) for w in weights],
        out_specs=row(D_MODEL),
        out_shape=jax.ShapeDtypeStruct((t, D_MODEL), F32),
        compiler_params=_params(),
        name="post_block",
    )(x, oa, od, gab, mod, lw["norm2_g"], *weights)


def _rope_tables(n_tokens):
    quarter = HEAD_DIM // 4
    lane = jnp.arange(LANES)
    d = lane % HEAD_DIM
    inv = ROPE_THETA ** (-(d % quarter).astype(F32) / quarter)
    t = jnp.arange(n_tokens)
    pos = jnp.where(d[None, :] < HEAD_DIM // 2, (t // GRID_W)[:, None], (t % GRID_W)[:, None]).astype(F32)
    ang = pos * inv[None, :]
    sign = jnp.where((d % (2 * quarter)) < quarter, -1.0, 1.0)
    return jnp.cos(ang), jnp.sin(ang) * sign[None, :]


def _pack_states(s):
    b = s.shape[0]
    return s.transpose(0, 2, 1, 3).reshape(b, DN_DK, DN_W)


def _layer_weights(l, w_in, norm1_g, q_norm_g, k_norm_g, a_log, dt_bias, dn_norm_g,
                   w_pa, w_pb, w_o, norm2_g, w_gate, w_up, w_down):
    pad_small = lambda a: jnp.pad(a.reshape(1, -1), ((0, 0), (0, LANES - a.size)))
    return dict(
        w_in=w_in,
        w_gab=w_in[l, :, _GATE_OFF:],
        norm1_g=norm1_g[l].reshape(1, D_MODEL),
        norm2_g=norm2_g[l].reshape(1, D_MODEL),
        gq=jnp.tile(q_norm_g[l], N_Q_HEADS).reshape(1, ATTN_Q_W),
        gk=jnp.tile(k_norm_g[l], N_KV_HEADS).reshape(1, ATTN_KV_W),
        a_log=pad_small(jnp.concatenate([jnp.zeros((2 * DN_HEADS,), F32), a_log[l].reshape(-1)])),
        dt_bias=pad_small(jnp.concatenate([jnp.zeros((2 * DN_HEADS,), F32), dt_bias[l].reshape(-1)])),
        gn=jnp.tile(dn_norm_g[l], DN_HEADS).reshape(1, DN_W),
        w_pa=w_pa[l].astype(BF16), w_pb=w_pb[l].astype(BF16), w_o=w_o[l].astype(BF16),
        w_gate=w_gate[l].astype(BF16), w_up=w_up[l].astype(BF16), w_down=w_down[l].astype(BF16),
    )


def kernel(x_prompt, x_sample, cache_k, cache_v, state_fwd, state_bwd, c, c_ctx, w_ada, b_ada, norm1_g, w_in,
           q_norm_g, k_norm_g, conv_w, a_log, dt_bias, dn_norm_g, w_pa, w_pb, w_o, norm2_g, w_gate, w_up, w_down):
    batch, seq, _ = x_prompt.shape
    dec_batch, dec_seq, _ = x_sample.shape
    depth = w_in.shape[0]
    past = cache_k.shape[2]
    assert dec_batch + 1 <= SUBLANES and seq % MXU_DIM == 0 and dec_seq % MXU_DIM == 0
    assert batch % 2 == 0 and dec_batch % 2 == 0

    cvecs = jnp.zeros((SUBLANES, D_MODEL), F32).at[0].set(c_ctx).at[1:1 + dec_batch].set(c)
    rope_tables = _rope_tables(dec_seq)
    tm_in = 512
    tm_post = 256
    ctx_group = lambda i: 0

    def lat_group(tm):
        return lambda i: 1 + i // (dec_seq // tm)

    yp = x_prompt.reshape(batch * seq, D_MODEL)
    ys = x_sample.reshape(dec_batch * dec_seq, D_MODEL)
    ks_out, vs_out, sf_out, sb_out = [], [], [], []
    w_in = w_in.astype(BF16)
    for l in range(depth):
        lw = _layer_weights(l, w_in, norm1_g, q_norm_g, k_norm_g, a_log, dt_bias, dn_norm_g,
                            w_pa, w_pb, w_o, norm2_g, w_gate, w_up, w_down)
        mod = _modulation(cvecs, w_ada[l], b_ada[l])[:1 + dec_batch].reshape(1 + dec_batch, 1, 6 * D_MODEL)

        q, kt, vt, cqkv, dz, ba, gab = _in_proj(yp, mod, ctx_group, lw, conv_w, l, tm_in, seq)
        oa = _attention(q, [(kt, vt, seq)], seq, seq, 2)
        od, sf, sb = _delta(cqkv, dz, ba, lw, seq, 2)
        yp = _post(yp, oa, od, gab, mod, ctx_group, lw, tm_post)
        ks_out.append(kt.transpose(0, 3, 1, 2))
        vs_out.append(vt.transpose(0, 3, 1, 2))
        sf_out.append(sf)
        sb_out.append(sb)

        q, kr, v, cqkv, dz, ba, gab = _in_proj(ys, mod, lat_group(tm_in), lw, conv_w, l, tm_in, dec_seq,
                                               rope_tables)
        ck = cache_k[:, l].reshape(dec_batch * past, ATTN_KV_W)
        cv = cache_v[:, l].reshape(dec_batch * past, ATTN_KV_W)
        oa = _attention(q, [(ck, cv, past), (kr, v, dec_seq)], dec_seq, 256)
        init = (_pack_states(state_fwd[:, l]), _pack_states(state_bwd[:, l]))
        od, _, _ = _delta(cqkv, dz, ba, lw, dec_seq, 2, init)
        ys = _post(ys, oa, od, gab, mod, lat_group(tm_post), lw, tm_post)

    return (yp.reshape(batch, seq, D_MODEL), ys.reshape(dec_batch, dec_seq, D_MODEL),
            jnp.stack(ks_out, axis=1), jnp.stack(vs_out, axis=1),
            jnp.stack(sf_out, axis=1), jnp.stack(sb_out, axis=1))
```

```python
import functools
import math

import jax
import jax.numpy as jnp
from jax import lax
from jax.experimental import pallas as pl
from jax.experimental.pallas import tpu as pltpu

D_MODEL = 1024
GRID_W = 64
HEAD_DIM = 64
N_Q_HEADS = 8
N_KV_HEADS = 2
Q_GROUP = N_Q_HEADS // N_KV_HEADS
ATTN_Q_W = N_Q_HEADS * HEAD_DIM
ATTN_KV_W = N_KV_HEADS * HEAD_DIM
ROPE_THETA = 10000.0
DN_HEADS = 8
DN_DK = 64
DN_DV = 64
DN_W = DN_HEADS * DN_DK
DN_CONV = 5
CHUNK = 64
EPS = 1e-6
LOG2_E = math.log2(math.e)

LANES = 128
SUBLANES = 8
MXU_DIM = 256
VMEM_LIMIT_BYTES = 56 * 1024 * 1024

HEADS_PER_GROUP = MXU_DIM // DN_DK
N_GROUPS = DN_HEADS // HEADS_PER_GROUP
GW = HEADS_PER_GROUP * DN_DK
INVERSE_BASE = 8

_QKV_W = ATTN_Q_W + 2 * ATTN_KV_W
_DN_OFF = _QKV_W
_BA_OFF = _DN_OFF + 4 * DN_W
_GATE_OFF = _BA_OFF + 4 * DN_HEADS
_BA_END = _BA_OFF + LANES

F32 = jnp.float32
BF16 = jnp.bfloat16


def _dot(a, b):
    return jnp.dot(a, b, preferred_element_type=F32)


def _dot_nt(a, b):
    return lax.dot_general(a, b, (((1,), (1,)), ((), ())), preferred_element_type=F32)


def _split2(x):
    hi = x.astype(BF16)
    lo = (x - hi.astype(F32)).astype(BF16)
    return hi, lo


def _split3(x):
    hi = x.astype(BF16)
    r = x - hi.astype(F32)
    mid = r.astype(BF16)
    lo = (r - mid.astype(F32)).astype(BF16)
    return hi, mid, lo


def _dot_exact_lhs(a01, b):
    n = b.shape[1]
    b1, b2, b3 = _split3(b)
    r = _dot(a01, jnp.concatenate([b1, b2, b3], axis=1))
    return r[:, :n] + r[:, n:2 * n] + r[:, 2 * n:]


def _iota(shape, dim):
    return lax.broadcasted_iota(jnp.int32, shape, dim)


def _same_block(shape, width):
    return (_iota(shape, 0) // width) == (_iota(shape, 1) // width)


def _head_sumsq(x, width):
    m, n = x.shape
    slab = min(n, MXU_DIM)
    sel = _same_block((slab, slab), width).astype(BF16)
    outs = []
    for s in range(n // slab):
        xs = x[:, s * slab:(s + 1) * slab]
        outs.append(_dot((xs * xs).astype(BF16), sel))
    return outs[0] if len(outs) == 1 else jnp.concatenate(outs, axis=1)


def _sigmoid(x):
    return 1.0 / (1.0 + jnp.exp(-x))


def _silu(x):
    return x * _sigmoid(x)


def _softplus(x):
    return jnp.maximum(x, 0.0) + jnp.log(1.0 + jnp.exp(-jnp.abs(x)))


def _resident(shape):
    nd = len(shape)
    return pl.BlockSpec(shape, lambda *_: (0,) * nd, pipeline_mode=pl.Buffered(1))


def _params(n_axes=1):
    return pltpu.CompilerParams(dimension_semantics=("arbitrary",) * n_axes,
                                vmem_limit_bytes=VMEM_LIMIT_BYTES)


def _mod_kernel(c_ref, w_ref, b_ref, o_ref):
    m = c_ref.shape[0]
    hi, lo = _split2(_silu(c_ref[...]))
    r = _dot(jnp.concatenate([hi, lo], axis=0), w_ref[...].astype(BF16))
    o_ref[...] = r[:m] + r[m:] + b_ref[...]


def _modulation(cvecs, w_ada, b_ada):
    n = w_ada.shape[1]
    tn = 1024
    return pl.pallas_call(
        _mod_kernel,
        grid=(n // tn,),
        in_specs=[pl.BlockSpec((SUBLANES, D_MODEL), lambda j: (0, 0)),
                  pl.BlockSpec((D_MODEL, tn), lambda j: (0, j)),
                  pl.BlockSpec((1, tn), lambda j: (0, j))],
        out_specs=pl.BlockSpec((SUBLANES, tn), lambda j: (0, j)),
        out_shape=jax.ShapeDtypeStruct((SUBLANES, n), F32),
        compiler_params=_params(),
        name="adaln_modulation",
    )(cvecs, w_ada, b_ada.reshape(1, n))


def _mod_parts(mod_ref, first):
    m = mod_ref[...]
    base = 0 if first else 3 * D_MODEL
    return (m[:, base:base + D_MODEL], m[:, base + D_MODEL:base + 2 * D_MODEL],
            m[:, base + 2 * D_MODEL:base + 3 * D_MODEL])


def _rms_mod(x, g, shift, scale):
    ms = jnp.mean(x * x, axis=-1, keepdims=True)
    return (x * lax.rsqrt(ms + EPS) * g) * (1.0 + scale) + shift


def _rope(x, cos, sin_signed):
    outs = []
    half = HEAD_DIM // 4
    first_half = (_iota((1, LANES), 1) % (2 * half)) < half
    for s in range(x.shape[1] // LANES):
        xs = x[:, s * LANES:(s + 1) * LANES]
        partner = jnp.where(first_half, pltpu.roll(xs, LANES - half, axis=1), pltpu.roll(xs, half, axis=1))
        outs.append(xs * cos + partner * sin_signed)
    return outs[0] if len(outs) == 1 else jnp.concatenate(outs, axis=1)


def _conv_silu(prev, x, nxt, taps):
    rows = x.shape[0]
    xe = jnp.concatenate([prev, x, nxt], axis=0)
    ne = rows + 2 * SUBLANES
    half = (DN_CONV - 1) // 2
    y = jnp.zeros(x.shape, F32)
    for tap in range(DN_CONV):
        d = tap - half
        sh = xe if d == 0 else pltpu.roll(xe, (ne - d) % ne, axis=0)
        y = y + sh[SUBLANES:SUBLANES + rows] * taps[tap:tap + 1]
    return _silu(y)


def _in_proj_kernel(*refs, rope, halo, seq):
    refs = list(refs)
    x_ref = refs.pop(0)
    xp_ref, xn_ref = (refs.pop(0), refs.pop(0)) if halo else (None, None)
    mod_ref, g_ref, w_ref, wgab_ref, gq_ref, gk_ref, cw_ref = (refs.pop(0) for _ in range(7))
    cos_ref, sin_ref = (refs.pop(0), refs.pop(0)) if rope else (None, None)
    q_ref, k_ref, v_ref, cqkv_ref, dz_ref, ba_ref, gab_ref = refs
    tm = x_ref.shape[0]
    shift, scale, _ = _mod_parts(mod_ref, True)
    h = _rms_mod(x_ref[...], g_ref[...], shift, scale).astype(BF16)

    qkv = _dot(h, w_ref[:, 0:_QKV_W])
    aq = qkv[:, :ATTN_Q_W]
    ak = qkv[:, ATTN_Q_W:ATTN_Q_W + ATTN_KV_W]
    av = qkv[:, ATTN_Q_W + ATTN_KV_W:]
    qn = aq * lax.rsqrt(_head_sumsq(aq, HEAD_DIM) * (1.0 / HEAD_DIM) + EPS) * gq_ref[...]
    kn = ak * lax.rsqrt(_head_sumsq(ak, HEAD_DIM) * (1.0 / HEAD_DIM) + EPS) * gk_ref[...]
    if rope:
        cos, sin = cos_ref[...], sin_ref[...]
        qn = _rope(qn, cos, sin)
        k_ref[...] = _rope(kn, cos, sin)
        v_ref[...] = av
    else:
        kt = kn.T
        vt = av.T
        for s in range(tm // seq):
            for hd in range(N_KV_HEADS):
                k_ref[s, hd] = kt[hd * HEAD_DIM:(hd + 1) * HEAD_DIM, s * seq:(s + 1) * seq]
                v_ref[s, hd] = vt[hd * HEAD_DIM:(hd + 1) * HEAD_DIM, s * seq:(s + 1) * seq]
    q_ref[...] = (qn * (HEAD_DIM ** -0.5 * LOG2_E)).astype(BF16)

    dn = _dot(h, w_ref[:, _DN_OFF:_BA_OFF])
    dz_ref[...] = dn[:, 3 * DN_W:]
    ba_ref[...] = _dot(h, w_ref[:, _BA_OFF:_BA_END])
    gab_ref[...] = _dot(h, wgab_ref[...]).astype(BF16)

    zeros = jnp.zeros((SUBLANES, 3 * DN_W), F32)
    if halo:
        tiles_per_seq = seq // tm
        pos = pl.program_id(0) % tiles_per_seq
        xh = jnp.concatenate([xp_ref[...], xn_ref[...]], axis=0)
        hh = _rms_mod(xh, g_ref[...], shift, scale).astype(BF16)
        dh = _dot(hh, w_ref[:, _DN_OFF:_DN_OFF + 3 * DN_W])
        edges = [(jnp.where(pos > 0, dh[:SUBLANES], 0.0), jnp.where(pos < tiles_per_seq - 1, dh[SUBLANES:], 0.0))]
        sub = tm
    else:
        sub = seq
        edges = [(zeros, zeros)] * (tm // seq)
    for s, (prev, nxt) in enumerate(edges):
        rows = slice(s * sub, (s + 1) * sub)
        for part in range(3):
            cols = slice(part * DN_W, (part + 1) * DN_W)
            y = _conv_silu(prev[:, cols], dn[rows, cols], nxt[:, cols], cw_ref[:, cols])
            if part == 0:
                y = y * lax.rsqrt(_head_sumsq(y, DN_DK) + EPS) * (DN_DK ** -0.5)
            elif part == 1:
                y = y * lax.rsqrt(_head_sumsq(y, DN_DK) + EPS)
            cqkv_ref[rows, cols] = y


def _in_proj(x, mod, group_of_tile, lw, conv_w, layer, tm, seq, rope_tables=None):
    t = x.shape[0]
    rope = rope_tables is not None
    halo = tm < seq
    assert tm % seq == 0 or seq % tm == 0
    row = lambda w: pl.BlockSpec((tm, w), lambda i: (i, 0))
    in_specs = [row(D_MODEL)]
    args = [x]
    if halo:
        blocks_per_tile = tm // SUBLANES
        last_block = t // SUBLANES - 1
        in_specs += [pl.BlockSpec((SUBLANES, D_MODEL), lambda i: (jnp.maximum(i * blocks_per_tile - 1, 0), 0)),
                     pl.BlockSpec((SUBLANES, D_MODEL),
                                  lambda i: (jnp.minimum((i + 1) * blocks_per_tile, last_block), 0))]
        args += [x, x]
    in_specs += [pl.BlockSpec((None, 1, 6 * D_MODEL), lambda i: (group_of_tile(i), 0, 0)),
                 _resident((1, D_MODEL)),
                 pl.BlockSpec((None, D_MODEL, _BA_END), lambda i: (layer, 0, 0), pipeline_mode=pl.Buffered(1)),
                 _resident(lw["w_gab"].shape),
                 _resident((1, ATTN_Q_W)), _resident((1, ATTN_KV_W)),
                 pl.BlockSpec((None, DN_CONV, 3 * DN_W), lambda i: (layer, 0, 0), pipeline_mode=pl.Buffered(1))]
    args += [mod, lw["norm1_g"], lw["w_in"], lw["w_gab"], lw["gq"], lw["gk"], conv_w]
    if rope:
        cos, sin = rope_tables
        tiles_per_seq = cos.shape[0] // tm
        in_specs += [pl.BlockSpec((tm, LANES), lambda i: (i % tiles_per_seq, 0))] * 2
        args += [cos, sin]
        kv_spec = row(ATTN_KV_W)
        kv_shape = jax.ShapeDtypeStruct((t, ATTN_KV_W), F32)
    else:
        kv_spec = pl.BlockSpec((tm // seq, N_KV_HEADS, HEAD_DIM, seq), lambda i: (i, 0, 0, 0))
        kv_shape = jax.ShapeDtypeStruct((t // seq, N_KV_HEADS, HEAD_DIM, seq), F32)
    widths = (3 * DN_W, DN_W, LANES)
    return pl.pallas_call(
        functools.partial(_in_proj_kernel, rope=rope, halo=halo, seq=seq),
        grid=(t // tm,),
        in_specs=in_specs,
        out_specs=[row(ATTN_Q_W), kv_spec, kv_spec] + [row(w) for w in widths] + [row(2 * D_MODEL)],
        out_shape=[jax.ShapeDtypeStruct((t, ATTN_Q_W), BF16), kv_shape, kv_shape]
                  + [jax.ShapeDtypeStruct((t, w), F32) for w in widths]
                  + [jax.ShapeDtypeStruct((t, 2 * D_MODEL), BF16)],
        compiler_params=_params(),
        name="in_proj_rope" if rope else "in_proj",
    )(*args)


def _attn_kernel(*refs, transposed, n_seq):
    n_parts = len(transposed)
    q_ref = refs[0]
    kv_refs = refs[1:1 + 2 * n_parts]
    o_ref = refs[1 + 2 * n_parts]
    tq = q_ref.shape[0] // n_seq
    qt = q_ref[...].astype(F32).T.astype(BF16)
    chains = [(s, kvh) for s in range(n_seq) for kvh in range(N_KV_HEADS)]
    qgs = [jnp.concatenate([qt[j * HEAD_DIM:(j + 1) * HEAD_DIM, s * tq:(s + 1) * tq]
                            for j in range(kvh * Q_GROUP, (kvh + 1) * Q_GROUP)], axis=1) for s, kvh in chains]

    def keys(p, s, kvh):
        k_ref = kv_refs[2 * p]
        k = k_ref[s, kvh].T if transposed[p] else k_ref[:, kvh * HEAD_DIM:(kvh + 1) * HEAD_DIM]
        return k.astype(BF16)

    def values_t(p, s, kvh):
        v_ref = kv_refs[2 * p + 1]
        v = v_ref[s, kvh] if transposed[p] else v_ref[:, kvh * HEAD_DIM:(kvh + 1) * HEAD_DIM].T
        return v.astype(BF16)

    ss = [[_dot(keys(p, s, kvh), qg) for p in range(n_parts)] for (s, kvh), qg in zip(chains, qgs)]
    ms = [functools.reduce(jnp.maximum, [jnp.max(sc, axis=0, keepdims=True) for sc in sp]) for sp in ss]
    ps = [[jnp.exp2(sc - m) for sc in sp] for sp, m in zip(ss, ms)]
    dens = [functools.reduce(jnp.add, [jnp.sum(p, axis=0, keepdims=True) for p in pp]) for pp in ps]
    accs = [functools.reduce(jnp.add, [_dot(values_t(p, s, kvh), pr.astype(BF16)) for p, pr in enumerate(pp)])
            for (s, kvh), pp in zip(chains, ps)]
    for s in range(n_seq):
        outs = []
        for kvh in range(N_KV_HEADS):
            o = accs[s * N_KV_HEADS + kvh] / dens[s * N_KV_HEADS + kvh]
            outs += [o[:, g * tq:(g + 1) * tq] for g in range(Q_GROUP)]
        o_ref[s * tq:(s + 1) * tq, :] = jnp.concatenate(outs, axis=0).T.astype(BF16)


def _attention(q, parts, seq_q, tq, n_seq=1):
    t = q.shape[0]
    nq = seq_q // tq
    assert n_seq == 1 or (nq == 1 and all(k.ndim == 4 for k, _, _ in parts))
    in_specs = [pl.BlockSpec((n_seq * tq, ATTN_Q_W), lambda b, i: (b * nq + i, 0))]
    args = [q]
    transposed = []
    for k, v, seq_k in parts:
        transposed.append(k.ndim == 4)
        if k.ndim == 4:
            in_specs += [pl.BlockSpec((n_seq, N_KV_HEADS, HEAD_DIM, seq_k), lambda b, i: (b, 0, 0, 0))] * 2
        else:
            in_specs += [pl.BlockSpec((seq_k, ATTN_KV_W), lambda b, i: (b, 0))] * 2
        args += [k, v]
    return pl.pallas_call(
        functools.partial(_attn_kernel, transposed=tuple(transposed), n_seq=n_seq),
        grid=(t // (seq_q * n_seq), nq),
        in_specs=in_specs,
        out_specs=pl.BlockSpec((n_seq * tq, ATTN_Q_W), lambda b, i: (b * nq + i, 0)),
        out_shape=jax.ShapeDtypeStruct((t, ATTN_Q_W), BF16),
        compiler_params=_params(2),
        name="attention_%dparts" % len(parts),
    )(*args)


def _block_diag(x, mask):
    reps = LANES // DN_DK
    return [jnp.where(mask, jnp.concatenate([x[:, c * LANES:(c + 1) * LANES]] * reps, axis=0),
                      jnp.zeros((), x.dtype)) for c in range(x.shape[1] // LANES)]


def _bdot(x, tiles, nt=False):
    op = _dot_nt if nt else _dot
    return jnp.concatenate([op(x[:, c * LANES:(c + 1) * LANES], t) for c, t in enumerate(tiles)], axis=1)


def _heads_transposed(x):
    xt = x.T
    return jnp.concatenate([xt[hb * DN_DK:(hb + 1) * DN_DK] for hb in range(HEADS_PER_GROUP)], axis=1)


def _packed_unit_inverse(lms, eye, row, col, mask):
    def mm1(x, y):
        return _bdot(x, _block_diag(y, mask))

    b = INVERSE_BASE
    ns = [-jnp.where((row // b) == (col // b), lm, 0.0) for lm in lms]
    ts = [eye + n for n in ns]
    ps = [mm1(nb, nb) for nb in (n.astype(BF16) for n in ns)]
    steps = int(math.log2(b)) - 1
    for s in range(steps):
        last = s == steps - 1
        pbs = [p.astype(BF16) for p in ps]
        prods = [mm1(t.astype(BF16) if last else jnp.concatenate([t.astype(BF16), pb], axis=0), pb)
                 for t, pb in zip(ts, pbs)]
        ts = [t + prod[:CHUNK] for t, prod in zip(ts, prods)]
        if not last:
            ps = [prod[CHUNK:] for prod in prods]
    while b < CHUNK:
        between = ((row // (2 * b)) == (col // (2 * b))) & ((row // b) != (col // b))
        tbs = [t.astype(BF16) for t in ts]
        ys = [mm1(tb, jnp.where(between, lm, 0.0).astype(BF16)) for lm, tb in zip(lms, tbs)]
        ts = [t - mm1(y.astype(BF16), tb) for t, tb, y in zip(ts, tbs, ys)]
        b *= 2
    xbs = [t.astype(BF16) for t in ts]
    xs = [xb.astype(F32) for xb in xbs]
    resid = []
    for lm, x, xb in zip(lms, xs, xbs):
        lh, ll = _split2(lm)
        prod = _bdot(jnp.concatenate([lh, ll], axis=0), _block_diag(xb, mask))
        resid.append(eye - x - prod[:CHUNK] - prod[CHUNK:])
    return [x + mm1(xb, e.astype(BF16)) for x, xb, e in zip(xs, xbs, resid)]


def _delta_kernel(*refs, seq, n_seq, has_init):
    if has_init:
        (x_ref, z_ref, ba_ref, alog_ref, dtb_ref, gn_ref, s0f_ref, s0b_ref,
         o_ref, sf_ref, sb_ref, gate_s, o_s, st_s) = refs
    else:
        (x_ref, z_ref, ba_ref, alog_ref, dtb_ref, gn_ref,
         o_ref, sf_ref, sb_ref, gate_s, o_s, st_s) = refs
    n_chunks = seq // CHUNK
    rb = MXU_DIM
    n_gate = 4 * DN_HEADS

    exp_r = _iota((LANES, 4 * DN_W), 0)
    expand = ((exp_r < 3 * n_gate) & ((_iota((LANES, 4 * DN_W), 1) // DN_DK) == exp_r % n_gate)).astype(BF16)
    lane = _iota((1, LANES), 1)
    blk_r = _iota((rb, rb), 0)
    blk_c = _iota((rb, rb), 1)
    same_chunk = (blk_r // CHUNK) == (blk_c // CHUNK)
    cum_f = (same_chunk & (blk_c <= blk_r)).astype(BF16)
    cum_b = (same_chunk & (blk_c >= blk_r)).astype(BF16)
    for blk in range(n_seq * seq // rb):
        rows = slice(blk * rb, (blk + 1) * rb)
        ba = ba_ref[rows, :]
        decay = -jnp.exp(alog_ref[...]) * _softplus(ba + dtb_ref[...])
        vals = jnp.where(lane < 2 * DN_HEADS, _sigmoid(ba), jnp.where(lane < n_gate, decay, 0.0))
        narrow = jnp.where(lane < 2 * DN_HEADS, vals,
                           jnp.where(lane < 3 * DN_HEADS, _dot_exact_lhs(cum_f, vals),
                                     jnp.where(lane < n_gate, _dot_exact_lhs(cum_b, vals), 0.0)))
        t1, t2, t3 = (t.astype(F32) for t in _split3(narrow))
        stacked = t1 + pltpu.roll(t2, n_gate, axis=1) + pltpu.roll(t3, 2 * n_gate, axis=1)
        gate_s[rows, :] = _dot(stacked.astype(BF16), expand)

    bd_mask = _same_block((LANES, LANES), DN_DK)
    row = _iota((CHUNK, GW), 0)
    col = _iota((CHUNK, GW), 1) % CHUNK
    diag = row == col
    eye = diag.astype(F32)
    dirs = ((col <= row, col < row, CHUNK - 1), (col >= row, col > row, 0))
    chains = [(s, d, g) for s in range(n_seq) for d in range(2) for g in range(N_GROUPS)]
    for ci, (s, d, g) in enumerate(chains):
        if has_init:
            st_s[ci] = (s0f_ref, s0b_ref)[d][s, :, g * GW:(g + 1) * GW]
        else:
            st_s[ci] = jnp.zeros((DN_DK, GW), F32)

    def bd(x):
        return _block_diag(x.astype(BF16), bd_mask)

    def chunk_step(n, carry):
        where = []
        for s, d, g in chains:
            c = n if d == 0 else n_chunks - 1 - n
            where.append((pl.ds(pl.multiple_of(s * seq + c * CHUNK, CHUNK), CHUNK), d, g))

        def load(ci, what):
            rows, d, g = where[ci]
            off = {"q": g * GW, "k": DN_W + g * GW, "v": 2 * DN_W + g * GW}
            if what in off:
                return x_ref[rows, pl.ds(off[what], GW)]
            return gate_s[rows, pl.ds((0 if what == "beta" else 2 * DN_W) + d * DN_W + g * GW, GW)]

        n_ch = len(chains)
        grams, decs = [], []
        for ci, (rows, d, g) in enumerate(where):
            incl = dirs[d][0]
            gc = load(ci, "gc")
            gc_col = jnp.sum(jnp.where(diag, gc, 0.0), axis=0, keepdims=True)
            decs.append(jnp.where(incl, jnp.exp(jnp.minimum(gc - gc_col, 0.0)), 0.0))
            k = load(ci, "k")
            lhs = jnp.concatenate([k * load(ci, "beta"), load(ci, "q")], axis=0).astype(BF16)
            grams.append(_bdot(lhs, bd(k), nt=True))
        lms = [jnp.where(dirs[d][1], gm[:CHUNK] * dec, 0.0) for (_, d, _), gm, dec in zip(where, grams, decs)]
        attn = [(gm[CHUNK:] * dec).astype(BF16) for gm, dec in zip(grams, decs)]
        ts = [t.astype(BF16) for t in _packed_unit_inverse(lms, eye, row, col, bd_mask)]
        us, ws = [], []
        for ci, t in enumerate(ts):
            beta = load(ci, "beta")
            us.append(_bdot(t, bd(load(ci, "v") * beta)))
            ws.append(_bdot(t, bd(load(ci, "k") * beta * jnp.exp(load(ci, "gc")))))
        states = [st_s[ci] for ci in range(n_ch)]
        ws_qs = [_bdot(jnp.concatenate([w, load(ci, "q") * jnp.exp(load(ci, "gc"))], axis=0).astype(BF16), bd(st))
                 for ci, (w, st) in enumerate(zip(ws, states))]
        g_lasts, lhs2 = [], []
        for ci, (rows, d, g) in enumerate(where):
            gc = load(ci, "gc")
            last_row = dirs[d][2]
            g_last = gc[last_row:last_row + 1, :]
            g_lasts.append(g_last)
            k_dec = load(ci, "k") * jnp.exp(g_last - gc)
            lhs2.append(jnp.concatenate([attn[ci], _heads_transposed(k_dec).astype(BF16)], axis=0))
        v_bds = [bd(u - x[:CHUNK]) for u, x in zip(us, ws_qs)]
        avs = [_bdot(l2, vb) for l2, vb in zip(lhs2, v_bds)]
        for ci, (rows, d, g) in enumerate(where):
            st_s[ci] = states[ci] * jnp.exp(g_lasts[ci]) + avs[ci][CHUNK:]
            o_s[d, rows, g * GW:(g + 1) * GW] = ws_qs[ci][CHUNK:] + avs[ci][:CHUNK]
        return carry

    lax.fori_loop(0, n_chunks, chunk_step, 0)

    for blk in range(n_seq * seq // rb):
        rows = slice(blk * rb, (blk + 1) * rb)
        o = o_s[0, rows, :] + o_s[1, rows, :]
        o = o * lax.rsqrt(_head_sumsq(o, DN_DV) * (1.0 / DN_DV) + EPS) * gn_ref[...]
        o_ref[rows, :] = (o * _silu(z_ref[rows, :])).astype(BF16)
    for ci, (s, d, g) in enumerate(chains):
        st = st_s[ci]
        for hb in range(HEADS_PER_GROUP):
            (sf_ref, sb_ref)[d][s, g * HEADS_PER_GROUP + hb] = st[:, hb * DN_DV:(hb + 1) * DN_DV]


def _delta(cqkv, dz, ba, lw, seq, n_seq, init=None):
    t = cqkv.shape[0]
    nb = t // seq
    rows = n_seq * seq
    has_init = init is not None
    seq_block = lambda w: pl.BlockSpec((rows, w), lambda b: (b, 0))
    state_block = pl.BlockSpec((n_seq, DN_DK, DN_W), lambda b: (b, 0, 0))
    in_specs = [seq_block(3 * DN_W), seq_block(DN_W), seq_block(LANES),
                _resident((1, LANES)), _resident((1, LANES)), _resident((1, DN_W))]
    args = [cqkv, dz, ba, lw["a_log"], lw["dt_bias"], lw["gn"]]
    if has_init:
        in_specs += [state_block, state_block]
        args += list(init)
    final_block = pl.BlockSpec((n_seq, DN_HEADS, DN_DK, DN_DV), lambda b: (b, 0, 0, 0))
    final_shape = jax.ShapeDtypeStruct((nb, DN_HEADS, DN_DK, DN_DV), F32)
    return pl.pallas_call(
        functools.partial(_delta_kernel, seq=seq, n_seq=n_seq, has_init=has_init),
        grid=(nb // n_seq,),
        in_specs=in_specs,
        out_specs=[seq_block(DN_W), final_block, final_block],
        out_shape=[jax.ShapeDtypeStruct((t, DN_W), BF16), final_shape, final_shape],
        scratch_shapes=[pltpu.VMEM((rows, 4 * DN_W), F32),
                        pltpu.VMEM((2, rows, DN_W), F32),
                        pltpu.VMEM((n_seq * 2 * N_GROUPS, DN_DK, GW), F32)],
        compiler_params=_params(),
        name="delta_rule_init" if has_init else "delta_rule",
    )(*args)


def _post_kernel(x_ref, oa_ref, od_ref, gab_ref, mod_ref, g2_ref, wpa_ref, wpb_ref, wo_ref,
                 wg_ref, wu_ref, wd_ref, y_ref):
    _, _, gate1 = _mod_parts(mod_ref, True)
    shift2, scale2, gate2 = _mod_parts(mod_ref, False)
    gab = gab_ref[...].astype(F32)
    merged = (_sigmoid(gab[:, :D_MODEL]) * _dot(oa_ref[...], wpa_ref[...])
              + _sigmoid(gab[:, D_MODEL:]) * _dot(od_ref[...], wpb_ref[...]))
    x1 = x_ref[...] + gate1 * _dot(merged.astype(BF16), wo_ref[...])
    h2 = _rms_mod(x1, g2_ref[...], shift2, scale2).astype(BF16)
    act = _silu(_dot(h2, wg_ref[...])) * _dot(h2, wu_ref[...])
    y_ref[...] = x1 + gate2 * _dot(act.astype(BF16), wd_ref[...])


def _post(x, oa, od, gab, mod, group_of_tile, lw, tm):
    t = x.shape[0]
    row = lambda w: pl.BlockSpec((tm, w), lambda i: (i, 0))
    weights = [lw[k] for k in ("w_pa", "w_pb", "w_o", "w_gate", "w_up", "w_down")]
    return pl.pallas_call(
        _post_kernel,
        grid=(t // tm,),
        in_specs=[row(D_MODEL), row(ATTN_Q_W), row(DN_W), row(2 * D_MODEL),
                  pl.BlockSpec((None, 1, 6 * D_MODEL), lambda i: (group_of_tile(i), 0, 0)),
                  _resident((1, D_MODEL))] + [_resident(w.shape) for w in weights],
        out_specs=row(D_MODEL),
        out_shape=jax.ShapeDtypeStruct((t, D_MODEL), F32),
        compiler_params=_params(),
        name="post_block",
    )(x, oa, od, gab, mod, lw["norm2_g"], *weights)


def _rope_tables(n_tokens):
    quarter = HEAD_DIM // 4
    lane = jnp.arange(LANES)
    d = lane % HEAD_DIM
    inv = ROPE_THETA ** (-(d % quarter).astype(F32) / quarter)
    t = jnp.arange(n_tokens)
    pos = jnp.where(d[None, :] < HEAD_DIM // 2, (t // GRID_W)[:, None], (t % GRID_W)[:, None]).astype(F32)
    ang = pos * inv[None, :]
    sign = jnp.where((d % (2 * quarter)) < quarter, -1.0, 1.0)
    return jnp.cos(ang), jnp.sin(ang) * sign[None, :]


def _pack_states(s):
    b = s.shape[0]
    return s.transpose(0, 2, 1, 3).reshape(b, DN_DK, DN_W)


def _layer_weights(l, w_in, norm1_g, q_norm_g, k_norm_g, a_log, dt_bias, dn_norm_g,
                   w_pa, w_pb, w_o, norm2_g, w_gate, w_up, w_down):
    pad_small = lambda a: jnp.pad(a.reshape(1, -1), ((0, 0), (0, LANES - a.size)))
    return dict(
        w_in=w_in,
        w_gab=w_in[l, :, _GATE_OFF:],
        norm1_g=norm1_g[l].reshape(1, D_MODEL),
        norm2_g=norm2_g[l].reshape(1, D_MODEL),
        gq=jnp.tile(q_norm_g[l], N_Q_HEADS).reshape(1, ATTN_Q_W),
        gk=jnp.tile(k_norm_g[l], N_KV_HEADS).reshape(1, ATTN_KV_W),
        a_log=pad_small(jnp.concatenate([jnp.zeros((2 * DN_HEADS,), F32), a_log[l].reshape(-1)])),
        dt_bias=pad_small(jnp.concatenate([jnp.zeros((2 * DN_HEADS,), F32), dt_bias[l].reshape(-1)])),
        gn=jnp.tile(dn_norm_g[l], DN_HEADS).reshape(1, DN_W),
        w_pa=w_pa[l].astype(BF16), w_pb=w_pb[l].astype(BF16), w_o=w_o[l].astype(BF16),
        w_gate=w_gate[l].astype(BF16), w_up=w_up[l].astype(BF16), w_down=w_down[l].astype(BF16),
    )


def kernel(x_prompt, x_sample, cache_k, cache_v, state_fwd, state_bwd, c, c_ctx, w_ada, b_ada, norm1_g, w_in,
           q_norm_g, k_norm_g, conv_w, a_log, dt_bias, dn_norm_g, w_pa, w_pb, w_o, norm2_g, w_gate, w_up, w_down):
    batch, seq, _ = x_prompt.shape
    dec_batch, dec_seq, _ = x_sample.shape
    depth = w_in.shape[0]
    past = cache_k.shape[2]
    assert dec_batch + 1 <= SUBLANES and seq % MXU_DIM == 0 and dec_seq % MXU_DIM == 0
    assert batch % 2 == 0 and dec_batch % 2 == 0

    cvecs = jnp.zeros((SUBLANES, D_MODEL), F32).at[0].set(c_ctx).at[1:1 + dec_batch].set(c)
    rope_tables = _rope_tables(dec_seq)
    tm_in = 512
    tm_post = 256
    ctx_group = lambda i: 0

    def lat_group(tm):
        return lambda i: 1 + i // (dec_seq // tm)

    yp = x_prompt.reshape(batch * seq, D_MODEL)
    ys = x_sample.reshape(dec_batch * dec_seq, D_MODEL)
    ks_out, vs_out, sf_out, sb_out = [], [], [], []
    w_in = w_in.astype(BF16)
    for l in range(depth):
        lw = _layer_weights(l, w_in, norm1_g, q_norm_g, k_norm_g, a_log, dt_bias, dn_norm_g,
                            w_pa, w_pb, w_o, norm2_g, w_gate, w_up, w_down)
        mod = _modulation(cvecs, w_ada[l], b_ada[l])[:1 + dec_batch].reshape(1 + dec_batch, 1, 6 * D_MODEL)

        q, kt, vt, cqkv, dz, ba, gab = _in_proj(yp, mod, ctx_group, lw, conv_w, l, tm_in, seq)
        oa = _attention(q, [(kt, vt, seq)], seq, seq, 2)
        od, sf, sb = _delta(cqkv, dz, ba, lw, seq, 2)
        yp = _post(yp, oa, od, gab, mod, ctx_group, lw, tm_post)
        ks_out.append(kt.transpose(0, 3, 1, 2))
        vs_out.append(vt.transpose(0, 3, 1, 2))
        sf_out.append(sf)
        sb_out.append(sb)

        q, kr, v, cqkv, dz, ba, gab = _in_proj(ys, mod, lat_group(tm_in), lw, conv_w, l, tm_in, dec_seq,
                                               rope_tables)
        ck = cache_k[:, l].reshape(dec_batch * past, ATTN_KV_W)
        cv = cache_v[:, l].reshape(dec_batch * past, ATTN_KV_W)
        oa = _attention(q, [(ck, cv, past), (kr, v, dec_seq)], dec_seq, 256)
        init = (_pack_states(state_fwd[:, l]), _pack_states(state_bwd[:, l]))
        od, _, _ = _delta(cqkv, dz, ba, lw, dec_seq, 2, init)
        ys = _post(ys, oa, od, gab, mod, lat_group(tm_post), lw, tm_post)

    return (yp.reshape(batch, seq, D_MODEL), ys.reshape(dec_batch, dec_seq, D_MODEL),
            jnp.stack(ks_out, axis=1), jnp.stack(vs_out, axis=1),
            jnp.stack(sf_out, axis=1), jnp.stack(sb_out, axis=1))
```

```python
import functools
import math

import jax
import jax.numpy as jnp
from jax import lax
from jax.experimental import pallas as pl
from jax.experimental.pallas import tpu as pltpu

D_MODEL = 1024
GRID_W = 64
HEAD_DIM = 64
N_Q_HEADS = 8
N_KV_HEADS = 2
Q_GROUP = N_Q_HEADS // N_KV_HEADS
ATTN_Q_W = N_Q_HEADS * HEAD_DIM
ATTN_KV_W = N_KV_HEADS * HEAD_DIM
ROPE_THETA = 10000.0
DN_HEADS = 8
DN_DK = 64
DN_DV = 64
DN_W = DN_HEADS * DN_DK
DN_CONV = 5
CHUNK = 64
EPS = 1e-6
LOG2_E = math.log2(math.e)

LANES = 128
SUBLANES = 8
MXU_DIM = 256
VMEM_LIMIT_BYTES = 56 * 1024 * 1024

HEADS_PER_GROUP = MXU_DIM // DN_DK
N_GROUPS = DN_HEADS // HEADS_PER_GROUP
GW = HEADS_PER_GROUP * DN_DK
INVERSE_BASE = 8

_QKV_W = ATTN_Q_W + 2 * ATTN_KV_W
_DN_OFF = _QKV_W
_BA_OFF = _DN_OFF + 4 * DN_W
_GATE_OFF = _BA_OFF + 4 * DN_HEADS
_BA_END = _BA_OFF + LANES

F32 = jnp.float32
BF16 = jnp.bfloat16


def _dot(a, b):
    return jnp.dot(a, b, preferred_element_type=F32)


def _dot_nt(a, b):
    return lax.dot_general(a, b, (((1,), (1,)), ((), ())), preferred_element_type=F32)


def _split2(x):
    hi = x.astype(BF16)
    lo = (x - hi.astype(F32)).astype(BF16)
    return hi, lo


def _split3(x):
    hi = x.astype(BF16)
    r = x - hi.astype(F32)
    mid = r.astype(BF16)
    lo = (r - mid.astype(F32)).astype(BF16)
    return hi, mid, lo


def _dot_exact_lhs(a01, b):
    n = b.shape[1]
    b1, b2, b3 = _split3(b)
    r = _dot(a01, jnp.concatenate([b1, b2, b3], axis=1))
    return r[:, :n] + r[:, n:2 * n] + r[:, 2 * n:]


def _iota(shape, dim):
    return lax.broadcasted_iota(jnp.int32, shape, dim)


def _same_block(shape, width):
    return (_iota(shape, 0) // width) == (_iota(shape, 1) // width)


def _head_sumsq(x, width):
    m, n = x.shape
    slab = min(n, MXU_DIM)
    sel = _same_block((slab, slab), width).astype(BF16)
    outs = []
    for s in range(n // slab):
        xs = x[:, s * slab:(s + 1) * slab]
        outs.append(_dot((xs * xs).astype(BF16), sel))
    return outs[0] if len(outs) == 1 else jnp.concatenate(outs, axis=1)


def _sigmoid(x):
    return 1.0 / (1.0 + jnp.exp(-x))


def _silu(x):
    return x * _sigmoid(x)


def _softplus(x):
    return jnp.maximum(x, 0.0) + jnp.log(1.0 + jnp.exp(-jnp.abs(x)))


def _resident(shape):
    nd = len(shape)
    return pl.BlockSpec(shape, lambda *_: (0,) * nd, pipeline_mode=pl.Buffered(1))


def _params(n_axes=1):
    return pltpu.CompilerParams(dimension_semantics=("arbitrary",) * n_axes,
                                vmem_limit_bytes=VMEM_LIMIT_BYTES)


def _mod_kernel(c_ref, w_ref, b_ref, o_ref):
    m = c_ref.shape[0]
    hi, lo = _split2(_silu(c_ref[...]))
    r = _dot(jnp.concatenate([hi, lo], axis=0), w_ref[...].astype(BF16))
    o_ref[...] = r[:m] + r[m:] + b_ref[...]


def _modulation(cvecs, w_ada, b_ada):
    n = w_ada.shape[1]
    tn = 1024
    return pl.pallas_call(
        _mod_kernel,
        grid=(n // tn,),
        in_specs=[pl.BlockSpec((SUBLANES, D_MODEL), lambda j: (0, 0)),
                  pl.BlockSpec((D_MODEL, tn), lambda j: (0, j)),
                  pl.BlockSpec((1, tn), lambda j: (0, j))],
        out_specs=pl.BlockSpec((SUBLANES, tn), lambda j: (0, j)),
        out_shape=jax.ShapeDtypeStruct((SUBLANES, n), F32),
        compiler_params=_params(),
        name="adaln_modulation",
    )(cvecs, w_ada, b_ada.reshape(1, n))


def _mod_parts(mod_ref, first):
    m = mod_ref[...]
    base = 0 if first else 3 * D_MODEL
    return (m[:, base:base + D_MODEL], m[:, base + D_MODEL:base + 2 * D_MODEL],
            m[:, base + 2 * D_MODEL:base + 3 * D_MODEL])


def _rms_mod(x, g, shift, scale):
    ms = jnp.mean(x * x, axis=-1, keepdims=True)
    return (x * lax.rsqrt(ms + EPS) * g) * (1.0 + scale) + shift


def _rope(x, cos, sin_signed):
    outs = []
    half = HEAD_DIM // 4
    first_half = (_iota((1, LANES), 1) % (2 * half)) < half
    for s in range(x.shape[1] // LANES):
        xs = x[:, s * LANES:(s + 1) * LANES]
        partner = jnp.where(first_half, pltpu.roll(xs, LANES - half, axis=1), pltpu.roll(xs, half, axis=1))
        outs.append(xs * cos + partner * sin_signed)
    return outs[0] if len(outs) == 1 else jnp.concatenate(outs, axis=1)


def _conv_silu(prev, x, nxt, taps):
    rows = x.shape[0]
    xe = jnp.concatenate([prev, x, nxt], axis=0)
    ne = rows + 2 * SUBLANES
    half = (DN_CONV - 1) // 2
    y = jnp.zeros(x.shape, F32)
    for tap in range(DN_CONV):
        d = tap - half
        sh = xe if d == 0 else pltpu.roll(xe, (ne - d) % ne, axis=0)
        y = y + sh[SUBLANES:SUBLANES + rows] * taps[tap:tap + 1]
    return _silu(y)


def _in_proj_kernel(*refs, rope, halo, seq):
    refs = list(refs)
    x_ref = refs.pop(0)
    xp_ref, xn_ref = (refs.pop(0), refs.pop(0)) if halo else (None, None)
    mod_ref, g_ref, w_ref, wgab_ref, gq_ref, gk_ref, cw_ref = (refs.pop(0) for _ in range(7))
    cos_ref, sin_ref = (refs.pop(0), refs.pop(0)) if rope else (None, None)
    q_ref, k_ref, v_ref, cqkv_ref, dz_ref, ba_ref, gab_ref = refs
    tm = x_ref.shape[0]
    shift, scale, _ = _mod_parts(mod_ref, True)
    h = _rms_mod(x_ref[...], g_ref[...], shift, scale).astype(BF16)

    qkv = _dot(h, w_ref[:, 0:_QKV_W])
    aq = qkv[:, :ATTN_Q_W]
    ak = qkv[:, ATTN_Q_W:ATTN_Q_W + ATTN_KV_W]
    av = qkv[:, ATTN_Q_W + ATTN_KV_W:]
    qn = aq * lax.rsqrt(_head_sumsq(aq, HEAD_DIM) * (1.0 / HEAD_DIM) + EPS) * gq_ref[...]
    kn = ak * lax.rsqrt(_head_sumsq(ak, HEAD_DIM) * (1.0 / HEAD_DIM) + EPS) * gk_ref[...]
    if rope:
        cos, sin = cos_ref[...], sin_ref[...]
        qn = _rope(qn, cos, sin)
        k_ref[...] = _rope(kn, cos, sin)
        v_ref[...] = av
    else:
        kt = kn.T
        vt = av.T
        for s in range(tm // seq):
            for hd in range(N_KV_HEADS):
                k_ref[s, hd] = kt[hd * HEAD_DIM:(hd + 1) * HEAD_DIM, s * seq:(s + 1) * seq]
                v_ref[s, hd] = vt[hd * HEAD_DIM:(hd + 1) * HEAD_DIM, s * seq:(s + 1) * seq]
    q_ref[...] = (qn * (HEAD_DIM ** -0.5 * LOG2_E)).astype(BF16)

    dn = _dot(h, w_ref[:, _DN_OFF:_BA_OFF])
    dz_ref[...] = dn[:, 3 * DN_W:]
    ba_ref[...] = _dot(h, w_ref[:, _BA_OFF:_BA_END])
    gab_ref[...] = _dot(h, wgab_ref[...]).astype(BF16)

    zeros = jnp.zeros((SUBLANES, 3 * DN_W), F32)
    if halo:
        tiles_per_seq = seq // tm
        pos = pl.program_id(0) % tiles_per_seq
        xh = jnp.concatenate([xp_ref[...], xn_ref[...]], axis=0)
        hh = _rms_mod(xh, g_ref[...], shift, scale).astype(BF16)
        dh = _dot(hh, w_ref[:, _DN_OFF:_DN_OFF + 3 * DN_W])
        edges = [(jnp.where(pos > 0, dh[:SUBLANES], 0.0), jnp.where(pos < tiles_per_seq - 1, dh[SUBLANES:], 0.0))]
        sub = tm
    else:
        sub = seq
        edges = [(zeros, zeros)] * (tm // seq)
    for s, (prev, nxt) in enumerate(edges):
        rows = slice(s * sub, (s + 1) * sub)
        for part in range(3):
            cols = slice(part * DN_W, (part + 1) * DN_W)
            y = _conv_silu(prev[:, cols], dn[rows, cols], nxt[:, cols], cw_ref[:, cols])
            if part == 0:
                y = y * lax.rsqrt(_head_sumsq(y, DN_DK) + EPS) * (DN_DK ** -0.5)
            elif part == 1:
                y = y * lax.rsqrt(_head_sumsq(y, DN_DK) + EPS)
            cqkv_ref[rows, cols] = y


def _in_proj(x, mod, group_of_tile, lw, conv_w, layer, tm, seq, rope_tables=None):
    t = x.shape[0]
    rope = rope_tables is not None
    halo = tm < seq
    assert tm % seq == 0 or seq % tm == 0
    row = lambda w: pl.BlockSpec((tm, w), lambda i: (i, 0))
    in_specs = [row(D_MODEL)]
    args = [x]
    if halo:
        blocks_per_tile = tm // SUBLANES
        last_block = t // SUBLANES - 1
        in_specs += [pl.BlockSpec((SUBLANES, D_MODEL), lambda i: (jnp.maximum(i * blocks_per_tile - 1, 0), 0)),
                     pl.BlockSpec((SUBLANES, D_MODEL),
                                  lambda i: (jnp.minimum((i + 1) * blocks_per_tile, last_block), 0))]
        args += [x, x]
    in_specs += [pl.BlockSpec((None, 1, 6 * D_MODEL), lambda i: (group_of_tile(i), 0, 0)),
                 _resident((1, D_MODEL)),
                 pl.BlockSpec((None, D_MODEL, _BA_END), lambda i: (layer, 0, 0), pipeline_mode=pl.Buffered(1)),
                 _resident(lw["w_gab"].shape),
                 _resident((1, ATTN_Q_W)), _resident((1, ATTN_KV_W)),
                 pl.BlockSpec((None, DN_CONV, 3 * DN_W), lambda i: (layer, 0, 0), pipeline_mode=pl.Buffered(1))]
    args += [mod, lw["norm1_g"], lw["w_in"], lw["w_gab"], lw["gq"], lw["gk"], conv_w]
    if rope:
        cos, sin = rope_tables
        tiles_per_seq = cos.shape[0] // tm
        in_specs += [pl.BlockSpec((tm, LANES), lambda i: (i % tiles_per_seq, 0))] * 2
        args += [cos, sin]
        kv_spec = row(ATTN_KV_W)
        kv_shape = jax.ShapeDtypeStruct((t, ATTN_KV_W), F32)
    else:
        kv_spec = pl.BlockSpec((tm // seq, N_KV_HEADS, HEAD_DIM, seq), lambda i: (i, 0, 0, 0))
        kv_shape = jax.ShapeDtypeStruct((t // seq, N_KV_HEADS, HEAD_DIM, seq), F32)
    widths = (3 * DN_W, DN_W, LANES)
    return pl.pallas_call(
        functools.partial(_in_proj_kernel, rope=rope, halo=halo, seq=seq),
        grid=(t // tm,),
        in_specs=in_specs,
        out_specs=[row(ATTN_Q_W), kv_spec, kv_spec] + [row(w) for w in widths] + [row(2 * D_MODEL)],
        out_shape=[jax.ShapeDtypeStruct((t, ATTN_Q_W), BF16), kv_shape, kv_shape]
                  + [jax.ShapeDtypeStruct((t, w), F32) for w in widths]
                  + [jax.ShapeDtypeStruct((t, 2 * D_MODEL), BF16)],
        compiler_params=_params(),
        name="in_proj_rope" if rope else "in_proj",
    )(*args)


def _attn_kernel(*refs, transposed, n_seq):
    n_parts = len(transposed)
    q_ref = refs[0]
    kv_refs = refs[1:1 + 2 * n_parts]
    o_ref = refs[1 + 2 * n_parts]
    tq = q_ref.shape[0] // n_seq
    qt = q_ref[...].astype(F32).T.astype(BF16)
    chains = [(s, kvh) for s in range(n_seq) for kvh in range(N_KV_HEADS)]
    qgs = [jnp.concatenate([qt[j * HEAD_DIM:(j + 1) * HEAD_DIM, s * tq:(s + 1) * tq]
                            for j in range(kvh * Q_GROUP, (kvh + 1) * Q_GROUP)], axis=1) for s, kvh in chains]

    def keys(p, s, kvh):
        k_ref = kv_refs[2 * p]
        k = k_ref[s, kvh].T if transposed[p] else k_ref[:, kvh * HEAD_DIM:(kvh + 1) * HEAD_DIM]
        return k.astype(BF16)

    def values_t(p, s, kvh):
        v_ref = kv_refs[2 * p + 1]
        v = v_ref[s, kvh] if transposed[p] else v_ref[:, kvh * HEAD_DIM:(kvh + 1) * HEAD_DIM].T
        return v.astype(BF16)

    ss = [[_dot(keys(p, s, kvh), qg) for p in range(n_parts)] for (s, kvh), qg in zip(chains, qgs)]
    ms = [functools.reduce(jnp.maximum, [jnp.max(sc, axis=0, keepdims=True) for sc in sp]) for sp in ss]
    ps = [[jnp.exp2(sc - m) for sc in sp] for sp, m in zip(ss, ms)]
    dens = [functools.reduce(jnp.add, [jnp.sum(p, axis=0, keepdims=True) for p in pp]) for pp in ps]
    accs = [functools.reduce(jnp.add, [_dot(values_t(p, s, kvh), pr.astype(BF16)) for p, pr in enumerate(pp)])
            for (s, kvh), pp in zip(chains, ps)]
    for s in range(n_seq):
        outs = []
        for kvh in range(N_KV_HEADS):
            o = accs[s * N_KV_HEADS + kvh] / dens[s * N_KV_HEADS + kvh]
            outs += [o[:, g * tq:(g + 1) * tq] for g in range(Q_GROUP)]
        o_ref[s * tq:(s + 1) * tq, :] = jnp.concatenate(outs, axis=0).T.astype(BF16)


def _attention(q, parts, seq_q, tq, n_seq=1):
    t = q.shape[0]
    nq = seq_q // tq
    assert n_seq == 1 or (nq == 1 and all(k.ndim == 4 for k, _, _ in parts))
    in_specs = [pl.BlockSpec((n_seq * tq, ATTN_Q_W), lambda b, i: (b * nq + i, 0))]
    args = [q]
    transposed = []
    for k, v, seq_k in parts:
        transposed.append(k.ndim == 4)
        if k.ndim == 4:
            in_specs += [pl.BlockSpec((n_seq, N_KV_HEADS, HEAD_DIM, seq_k), lambda b, i: (b, 0, 0, 0))] * 2
        else:
            in_specs += [pl.BlockSpec((seq_k, ATTN_KV_W), lambda b, i: (b, 0))] * 2
        args += [k, v]
    return pl.pallas_call(
        functools.partial(_attn_kernel, transposed=tuple(transposed), n_seq=n_seq),
        grid=(t // (seq_q * n_seq), nq),
        in_specs=in_specs,
        out_specs=pl.BlockSpec((n_seq * tq, ATTN_Q_W), lambda b, i: (b * nq + i, 0)),
        out_shape=jax.ShapeDtypeStruct((t, ATTN_Q_W), BF16),
        compiler_params=_params(2),
        name="attention_%dparts" % len(parts),
    )(*args)


def _block_diag(x, mask):
    reps = LANES // DN_DK
    return [jnp.where(mask, jnp.concatenate([x[:, c * LANES:(c + 1) * LANES]] * reps, axis=0),
                      jnp.zeros((), x.dtype)) for c in range(x.shape[1] // LANES)]


def _bdot(x, tiles, nt=False):
    op = _dot_nt if nt else _dot
    return jnp.concatenate([op(x[:, c * LANES:(c + 1) * LANES], t) for c, t in enumerate(tiles)], axis=1)


def _heads_transposed(x):
    xt = x.T
    return jnp.concatenate([xt[hb * DN_DK:(hb + 1) * DN_DK] for hb in range(HEADS_PER_GROUP)], axis=1)


def _packed_unit_inverse(lms, eye, row, col, mask):
    def mm1(x, y):
        return _bdot(x, _block_diag(y, mask))

    b = INVERSE_BASE
    ns = [-jnp.where((row // b) == (col // b), lm, 0.0) for lm in lms]
    ts = [eye + n for n in ns]
    ps = [mm1(nb, nb) for nb in (n.astype(BF16) for n in ns)]
    steps = int(math.log2(b)) - 1
    for s in range(steps):
        last = s == steps - 1
        pbs = [p.astype(BF16) for p in ps]
        prods = [mm1(t.astype(BF16) if last else jnp.concatenate([t.astype(BF16), pb], axis=0), pb)
                 for t, pb in zip(ts, pbs)]
        ts = [t + prod[:CHUNK] for t, prod in zip(ts, prods)]
        if not last:
            ps = [prod[CHUNK:] for prod in prods]
    while b < CHUNK:
        between = ((row // (2 * b)) == (col // (2 * b))) & ((row // b) != (col // b))
        tbs = [t.astype(BF16) for t in ts]
        ys = [mm1(tb, jnp.where(between, lm, 0.0).astype(BF16)) for lm, tb in zip(lms, tbs)]
        ts = [t - mm1(y.astype(BF16), tb) for t, tb, y in zip(ts, tbs, ys)]
        b *= 2
    return ts


def _delta_kernel(*refs, seq, n_seq, has_init):
    if has_init:
        (x_ref, z_ref, ba_ref, alog_ref, dtb_ref, gn_ref, s0f_ref, s0b_ref,
         o_ref, sf_ref, sb_ref, gate_s, o_s, st_s) = refs
    else:
        (x_ref, z_ref, ba_ref, alog_ref, dtb_ref, gn_ref,
         o_ref, sf_ref, sb_ref, gate_s, o_s, st_s) = refs
    n_chunks = seq // CHUNK
    rb = MXU_DIM
    n_gate = 4 * DN_HEADS

    exp_r = _iota((LANES, 4 * DN_W), 0)
    expand = ((exp_r < 3 * n_gate) & ((_iota((LANES, 4 * DN_W), 1) // DN_DK) == exp_r % n_gate)).astype(BF16)
    lane = _iota((1, LANES), 1)
    blk_r = _iota((rb, rb), 0)
    blk_c = _iota((rb, rb), 1)
    same_chunk = (blk_r // CHUNK) == (blk_c // CHUNK)
    cum_f = (same_chunk & (blk_c <= blk_r)).astype(BF16)
    cum_b = (same_chunk & (blk_c >= blk_r)).astype(BF16)
    for blk in range(n_seq * seq // rb):
        rows = slice(blk * rb, (blk + 1) * rb)
        ba = ba_ref[rows, :]
        decay = -jnp.exp(alog_ref[...]) * _softplus(ba + dtb_ref[...])
        vals = jnp.where(lane < 2 * DN_HEADS, _sigmoid(ba), jnp.where(lane < n_gate, decay, 0.0))
        narrow = jnp.where(lane < 2 * DN_HEADS, vals,
                           jnp.where(lane < 3 * DN_HEADS, _dot_exact_lhs(cum_f, vals),
                                     jnp.where(lane < n_gate, _dot_exact_lhs(cum_b, vals), 0.0)))
        t1, t2, t3 = (t.astype(F32) for t in _split3(narrow))
        stacked = t1 + pltpu.roll(t2, n_gate, axis=1) + pltpu.roll(t3, 2 * n_gate, axis=1)
        gate_s[rows, :] = _dot(stacked.astype(BF16), expand)

    bd_mask = _same_block((LANES, LANES), DN_DK)
    row = _iota((CHUNK, GW), 0)
    col = _iota((CHUNK, GW), 1) % CHUNK
    diag = row == col
    eye = diag.astype(F32)
    dirs = ((col <= row, col < row, CHUNK - 1), (col >= row, col > row, 0))
    chains = [(s, d, g) for s in range(n_seq) for d in range(2) for g in range(N_GROUPS)]
    for ci, (s, d, g) in enumerate(chains):
        if has_init:
            st_s[ci] = (s0f_ref, s0b_ref)[d][s, :, g * GW:(g + 1) * GW]
        else:
            st_s[ci] = jnp.zeros((DN_DK, GW), F32)

    def bd(x):
        return _block_diag(x.astype(BF16), bd_mask)

    def chunk_step(n, carry):
        where = []
        for s, d, g in chains:
            c = n if d == 0 else n_chunks - 1 - n
            where.append((pl.ds(pl.multiple_of(s * seq + c * CHUNK, CHUNK), CHUNK), d, g))

        def load(ci, what):
            rows, d, g = where[ci]
            off = {"q": g * GW, "k": DN_W + g * GW, "v": 2 * DN_W + g * GW}
            if what in off:
                return x_ref[rows, pl.ds(off[what], GW)]
            return gate_s[rows, pl.ds((0 if what == "beta" else 2 * DN_W) + d * DN_W + g * GW, GW)]

        n_ch = len(chains)
        grams, decs = [], []
        for ci, (rows, d, g) in enumerate(where):
            incl = dirs[d][0]
            gc = load(ci, "gc")
            gc_col = jnp.sum(jnp.where(diag, gc, 0.0), axis=0, keepdims=True)
            decs.append(jnp.where(incl, jnp.exp(jnp.minimum(gc - gc_col, 0.0)), 0.0))
            k = load(ci, "k")
            lhs = jnp.concatenate([k * load(ci, "beta"), load(ci, "q")], axis=0).astype(BF16)
            grams.append(_bdot(lhs, bd(k), nt=True))
        lms = [jnp.where(dirs[d][1], gm[:CHUNK] * dec, 0.0) for (_, d, _), gm, dec in zip(where, grams, decs)]
        attn = [(gm[CHUNK:] * dec).astype(BF16) for gm, dec in zip(grams, decs)]
        ts = [t.astype(BF16) for t in _packed_unit_inverse(lms, eye, row, col, bd_mask)]
        us, ws = [], []
        for ci, t in enumerate(ts):
            beta = load(ci, "beta")
            us.append(_bdot(t, bd(load(ci, "v") * beta)))
            ws.append(_bdot(t, bd(load(ci, "k") * beta * jnp.exp(load(ci, "gc")))))
        states = [st_s[ci] for ci in range(n_ch)]
        ws_qs = [_bdot(jnp.concatenate([w, load(ci, "q") * jnp.exp(load(ci, "gc"))], axis=0).astype(BF16), bd(st))
                 for ci, (w, st) in enumerate(zip(ws, states))]
        g_lasts, lhs2 = [], []
        for ci, (rows, d, g) in enumerate(where):
            gc = load(ci, "gc")
            last_row = dirs[d][2]
            g_last = gc[last_row:last_row + 1, :]
            g_lasts.append(g_last)
            k_dec = load(ci, "k") * jnp.exp(g_last - gc)
            lhs2.append(jnp.concatenate([attn[ci], _heads_transposed(k_dec).astype(BF16)], axis=0))
        v_bds = [bd(u - x[:CHUNK]) for u, x in zip(us, ws_qs)]
        avs = [_bdot(l2, vb) for l2, vb in zip(lhs2, v_bds)]
        for ci, (rows, d, g) in enumerate(where):
            st_s[ci] = states[ci] * jnp.exp(g_lasts[ci]) + avs[ci][CHUNK:]
            o_s[d, rows, g * GW:(g + 1) * GW] = ws_qs[ci][CHUNK:] + avs[ci][:CHUNK]
        return carry

    lax.fori_loop(0, n_chunks, chunk_step, 0)

    for blk in range(n_seq * seq // rb):
        rows = slice(blk * rb, (blk + 1) * rb)
        o = o_s[0, rows, :] + o_s[1, rows, :]
        o = o * lax.rsqrt(_head_sumsq(o, DN_DV) * (1.0 / DN_DV) + EPS) * gn_ref[...]
        o_ref[rows, :] = (o * _silu(z_ref[rows, :])).astype(BF16)
    for ci, (s, d, g) in enumerate(chains):
        st = st_s[ci]
        for hb in range(HEADS_PER_GROUP):
            (sf_ref, sb_ref)[d][s, g * HEADS_PER_GROUP + hb] = st[:, hb * DN_DV:(hb + 1) * DN_DV]


def _delta(cqkv, dz, ba, lw, seq, n_seq, init=None):
    t = cqkv.shape[0]
    nb = t // seq
    rows = n_seq * seq
    has_init = init is not None
    seq_block = lambda w: pl.BlockSpec((rows, w), lambda b: (b, 0))
    state_block = pl.BlockSpec((n_seq, DN_DK, DN_W), lambda b: (b, 0, 0))
    in_specs = [seq_block(3 * DN_W), seq_block(DN_W), seq_block(LANES),
                _resident((1, LANES)), _resident((1, LANES)), _resident((1, DN_W))]
    args = [cqkv, dz, ba, lw["a_log"], lw["dt_bias"], lw["gn"]]
    if has_init:
        in_specs += [state_block, state_block]
        args += list(init)
    final_block = pl.BlockSpec((n_seq, DN_HEADS, DN_DK, DN_DV), lambda b: (b, 0, 0, 0))
    final_shape = jax.ShapeDtypeStruct((nb, DN_HEADS, DN_DK, DN_DV), F32)
    return pl.pallas_call(
        functools.partial(_delta_kernel, seq=seq, n_seq=n_seq, has_init=has_init),
        grid=(nb // n_seq,),
        in_specs=in_specs,
        out_specs=[seq_block(DN_W), final_block, final_block],
        out_shape=[jax.ShapeDtypeStruct((t, DN_W), BF16), final_shape, final_shape],
        scratch_shapes=[pltpu.VMEM((rows, 4 * DN_W), F32),
                        pltpu.VMEM((2, rows, DN_W), F32),
                        pltpu.VMEM((n_seq * 2 * N_GROUPS, DN_DK, GW), F32)],
        compiler_params=_params(),
        name="delta_rule_init" if has_init else "delta_rule",
    )(*args)


def _post_kernel(x_ref, oa_ref, od_ref, gab_ref, mod_ref, g2_ref, wpa_ref, wpb_ref, wo_ref,
                 wg_ref, wu_ref, wd_ref, y_ref):
    _, _, gate1 = _mod_parts(mod_ref, True)
    shift2, scale2, gate2 = _mod_parts(mod_ref, False)
    gab = gab_ref[...].astype(F32)
    merged = (_sigmoid(gab[:, :D_MODEL]) * _dot(oa_ref[...], wpa_ref[...])
              + _sigmoid(gab[:, D_MODEL:]) * _dot(od_ref[...], wpb_ref[...]))
    x1 = x_ref[...] + gate1 * _dot(merged.astype(BF16), wo_ref[...])
    h2 = _rms_mod(x1, g2_ref[...], shift2, scale2).astype(BF16)
    act = _silu(_dot(h2, wg_ref[...])) * _dot(h2, wu_ref[...])
    y_ref[...] = x1 + gate2 * _dot(act.astype(BF16), wd_ref[...])


def _post(x, oa, od, gab, mod, group_of_tile, lw, tm):
    t = x.shape[0]
    row = lambda w: pl.BlockSpec((tm, w), lambda i: (i, 0))
    weights = [lw[k] for k in ("w_pa", "w_pb", "w_o", "w_gate", "w_up", "w_down")]
    return pl.pallas_call(
        _post_kernel,
        grid=(t // tm,),
        in_specs=[row(D_MODEL), row(ATTN_Q_W), row(DN_W), row(2 * D_MODEL),
                  pl.BlockSpec((None, 1, 6 * D_MODEL), lambda i: (group_of_tile(i), 0, 0)),
                  _resident((1, D_MODEL))] + [_resident(w.shape) for w in weights],
        out_specs=row(D_MODEL),
        out_shape=jax.ShapeDtypeStruct((t, D_MODEL), F32),
        compiler_params=_params(),
        name="post_block",
    )(x, oa, od, gab, mod, lw["norm2_g"], *weights)


def _rope_tables(n_tokens):
    quarter = HEAD_DIM // 4
    lane = jnp.arange(LANES)
    d = lane % HEAD_DIM
    inv = ROPE_THETA ** (-(d % quarter).astype(F32) / quarter)
    t = jnp.arange(n_tokens)
    pos = jnp.where(d[None, :] < HEAD_DIM // 2, (t // GRID_W)[:, None], (t % GRID_W)[:, None]).astype(F32)
    ang = pos * inv[None, :]
    sign = jnp.where((d % (2 * quarter)) < quarter, -1.0, 1.0)
    return jnp.cos(ang), jnp.sin(ang) * sign[None, :]


def _pack_states(s):
    b = s.shape[0]
    return s.transpose(0, 2, 1, 3).reshape(b, DN_DK, DN_W)


def _layer_weights(l, w_in, norm1_g, q_norm_g, k_norm_g, a_log, dt_bias, dn_norm_g,
                   w_pa, w_pb, w_o, norm2_g, w_gate, w_up, w_down):
    pad_small = lambda a: jnp.pad(a.reshape(1, -1), ((0, 0), (0, LANES - a.size)))
    return dict(
        w_in=w_in,
        w_gab=w_in[l, :, _GATE_OFF:],
        norm1_g=norm1_g[l].reshape(1, D_MODEL),
        norm2_g=norm2_g[l].reshape(1, D_MODEL),
        gq=jnp.tile(q_norm_g[l], N_Q_HEADS).reshape(1, ATTN_Q_W),
        gk=jnp.tile(k_norm_g[l], N_KV_HEADS).reshape(1, ATTN_KV_W),
        a_log=pad_small(jnp.concatenate([jnp.zeros((2 * DN_HEADS,), F32), a_log[l].reshape(-1)])),
        dt_bias=pad_small(jnp.concatenate([jnp.zeros((2 * DN_HEADS,), F32), dt_bias[l].reshape(-1)])),
        gn=jnp.tile(dn_norm_g[l], DN_HEADS).reshape(1, DN_W),
        w_pa=w_pa[l].astype(BF16), w_pb=w_pb[l].astype(BF16), w_o=w_o[l].astype(BF16),
        w_gate=w_gate[l].astype(BF16), w_up=w_up[l].astype(BF16), w_down=w_down[l].astype(BF16),
    )


def kernel(x_prompt, x_sample, cache_k, cache_v, state_fwd, state_bwd, c, c_ctx, w_ada, b_ada, norm1_g, w_in,
           q_norm_g, k_norm_g, conv_w, a_log, dt_bias, dn_norm_g, w_pa, w_pb, w_o, norm2_g, w_gate, w_up, w_down):
    batch, seq, _ = x_prompt.shape
    dec_batch, dec_seq, _ = x_sample.shape
    depth = w_in.shape[0]
    past = cache_k.shape[2]
    assert dec_batch + 1 <= SUBLANES and seq % MXU_DIM == 0 and dec_seq % MXU_DIM == 0
    assert batch % 2 == 0 and dec_batch % 2 == 0

    cvecs = jnp.zeros((SUBLANES, D_MODEL), F32).at[0].set(c_ctx).at[1:1 + dec_batch].set(c)
    rope_tables = _rope_tables(dec_seq)
    tm_in = 512
    tm_post = 256
    ctx_group = lambda i: 0

    def lat_group(tm):
        return lambda i: 1 + i // (dec_seq // tm)

    yp = x_prompt.reshape(batch * seq, D_MODEL)
    ys = x_sample.reshape(dec_batch * dec_seq, D_MODEL)
    ks_out, vs_out, sf_out, sb_out = [], [], [], []
    w_in = w_in.astype(BF16)
    for l in range(depth):
        lw = _layer_weights(l, w_in, norm1_g, q_norm_g, k_norm_g, a_log, dt_bias, dn_norm_g,
                            w_pa, w_pb, w_o, norm2_g, w_gate, w_up, w_down)
        mod = _modulation(cvecs, w_ada[l], b_ada[l])[:1 + dec_batch].reshape(1 + dec_batch, 1, 6 * D_MODEL)

        q, kt, vt, cqkv, dz, ba, gab = _in_proj(yp, mod, ctx_group, lw, conv_w, l, tm_in, seq)
        oa = _attention(q, [(kt, vt, seq)], seq, seq, 2)
        od, sf, sb = _delta(cqkv, dz, ba, lw, seq, 2)
        yp = _post(yp, oa, od, gab, mod, ctx_group, lw, tm_post)
        ks_out.append(kt.transpose(0, 3, 1, 2))
        vs_out.append(vt.transpose(0, 3, 1, 2))
        sf_out.append(sf)
        sb_out.append(sb)

        q, kr, v, cqkv, dz, ba, gab = _in_proj(ys, mod, lat_group(tm_in), lw, conv_w, l, tm_in, dec_seq,
                                               rope_tables)
        ck = cache_k[:, l].reshape(dec_batch * past, ATTN_KV_W)
        cv = cache_v[:, l].reshape(dec_batch * past, ATTN_KV_W)
        oa = _attention(q, [(ck, cv, past), (kr, v, dec_seq)], dec_seq, 256)
        init = (_pack_states(state_fwd[:, l]), _pack_states(state_bwd[:, l]))
        od, _, _ = _delta(cqkv, dz, ba, lw, dec_seq, 2, init)
        ys = _post(ys, oa, od, gab, mod, lat_group(tm_post), lw, tm_post)

    return (yp.reshape(batch, seq, D_MODEL), ys.reshape(dec_batch, dec_seq, D_MODEL),
            jnp.stack(ks_out, axis=1), jnp.stack(vs_out, axis=1),
            jnp.stack(sf_out, axis=1), jnp.stack(sb_out, axis=1))
```

```python
import functools
import math

import jax
import jax.numpy as jnp
from jax import lax
from jax.experimental import pallas as pl
from jax.experimental.pallas import tpu as pltpu

D_MODEL = 1024
GRID_W = 64
HEAD_DIM = 64
N_Q_HEADS = 8
N_KV_HEADS = 2
Q_GROUP = N_Q_HEADS // N_KV_HEADS
ATTN_Q_W = N_Q_HEADS * HEAD_DIM
ATTN_KV_W = N_KV_HEADS * HEAD_DIM
ROPE_THETA = 10000.0
DN_HEADS = 8
DN_DK = 64
DN_DV = 64
DN_W = DN_HEADS * DN_DK
DN_CONV = 5
CHUNK = 64
EPS = 1e-6
LOG2_E = math.log2(math.e)

LANES = 128
SUBLANES = 8
MXU_DIM = 256
VMEM_LIMIT_BYTES = 56 * 1024 * 1024

HEADS_PER_GROUP = MXU_DIM // DN_DK
N_GROUPS = DN_HEADS // HEADS_PER_GROUP
GW = HEADS_PER_GROUP * DN_DK
INVERSE_BASE = 8

IN_PROJ_TILE = 512
POST_TILE = 256
ATTN_Q_TILE = 256
SEQS_PER_STEP = 2
ATTN_SEQS_PER_STEP = 4
ADALN_TILE = 2048

_QKV_W = ATTN_Q_W + 2 * ATTN_KV_W
_DN_OFF = _QKV_W
_BA_OFF = _DN_OFF + 4 * DN_W
_GATE_OFF = _BA_OFF + 4 * DN_HEADS
_BA_END = _BA_OFF + LANES

F32 = jnp.float32
BF16 = jnp.bfloat16


def _dot(a, b):
    return jnp.dot(a, b, preferred_element_type=F32)


def _dot_nt(a, b):
    return lax.dot_general(a, b, (((1,), (1,)), ((), ())), preferred_element_type=F32)


def _split2(x):
    hi = x.astype(BF16)
    lo = (x - hi.astype(F32)).astype(BF16)
    return hi, lo


def _split3(x):
    hi = x.astype(BF16)
    r = x - hi.astype(F32)
    mid = r.astype(BF16)
    lo = (r - mid.astype(F32)).astype(BF16)
    return hi, mid, lo


def _dot_exact_lhs(a01, b):
    n = b.shape[1]
    b1, b2, b3 = _split3(b)
    r = _dot(a01, jnp.concatenate([b1, b2, b3], axis=1))
    return r[:, :n] + r[:, n:2 * n] + r[:, 2 * n:]


def _iota(shape, dim):
    return lax.broadcasted_iota(jnp.int32, shape, dim)


def _same_block(shape, width):
    return (_iota(shape, 0) // width) == (_iota(shape, 1) // width)


def _head_sumsq(x, width):
    m, n = x.shape
    slab = min(n, MXU_DIM)
    sel = _same_block((slab, slab), width).astype(BF16)
    outs = []
    for s in range(n // slab):
        xs = x[:, s * slab:(s + 1) * slab]
        outs.append(_dot((xs * xs).astype(BF16), sel))
    return outs[0] if len(outs) == 1 else jnp.concatenate(outs, axis=1)


def _sigmoid(x):
    return 1.0 / (1.0 + jnp.exp(-x))


def _silu(x):
    return x * _sigmoid(x)


def _softplus(x):
    return jnp.maximum(x, 0.0) + jnp.log(1.0 + jnp.exp(-jnp.abs(x)))


def _resident(shape):
    nd = len(shape)
    return pl.BlockSpec(shape, lambda *_: (0,) * nd, pipeline_mode=pl.Buffered(1))


def _params(n_axes=1):
    return pltpu.CompilerParams(dimension_semantics=("arbitrary",) * n_axes,
                                vmem_limit_bytes=VMEM_LIMIT_BYTES)


def _mod_kernel(c_ref, w_ref, b_ref, o_ref):
    m = c_ref.shape[0]
    hi, lo = _split2(_silu(c_ref[...]))
    r = _dot(jnp.concatenate([hi, lo], axis=0), w_ref[...].astype(BF16))
    o_ref[...] = r[:m] + r[m:] + b_ref[...]


def _modulation(cvecs, w_ada, b_ada):
    n = w_ada.shape[1]
    tn = ADALN_TILE
    return pl.pallas_call(
        _mod_kernel,
        grid=(n // tn,),
        in_specs=[pl.BlockSpec((SUBLANES, D_MODEL), lambda j: (0, 0)),
                  pl.BlockSpec((D_MODEL, tn), lambda j: (0, j)),
                  pl.BlockSpec((1, tn), lambda j: (0, j))],
        out_specs=pl.BlockSpec((SUBLANES, tn), lambda j: (0, j)),
        out_shape=jax.ShapeDtypeStruct((SUBLANES, n), F32),
        compiler_params=_params(),
        name="adaln_modulation",
    )(cvecs, w_ada, b_ada.reshape(1, n))


def _mod_parts(mod_ref, first):
    m = mod_ref[...]
    base = 0 if first else 3 * D_MODEL
    return (m[:, base:base + D_MODEL], m[:, base + D_MODEL:base + 2 * D_MODEL],
            m[:, base + 2 * D_MODEL:base + 3 * D_MODEL])


def _rms_mod(x, g, shift, scale):
    ms = jnp.mean(x * x, axis=-1, keepdims=True)
    return (x * lax.rsqrt(ms + EPS) * g) * (1.0 + scale) + shift


def _rope(x, cos, sin_signed):
    outs = []
    half = HEAD_DIM // 4
    first_half = (_iota((1, LANES), 1) % (2 * half)) < half
    for s in range(x.shape[1] // LANES):
        xs = x[:, s * LANES:(s + 1) * LANES]
        partner = jnp.where(first_half, pltpu.roll(xs, LANES - half, axis=1), pltpu.roll(xs, half, axis=1))
        outs.append(xs * cos + partner * sin_signed)
    return outs[0] if len(outs) == 1 else jnp.concatenate(outs, axis=1)


def _conv_silu(prev, x, nxt, taps):
    rows = x.shape[0]
    xe = jnp.concatenate([prev, x, nxt], axis=0)
    ne = rows + 2 * SUBLANES
    half = (DN_CONV - 1) // 2
    y = jnp.zeros(x.shape, F32)
    for tap in range(DN_CONV):
        d = tap - half
        sh = xe if d == 0 else pltpu.roll(xe, (ne - d) % ne, axis=0)
        y = y + sh[SUBLANES:SUBLANES + rows] * taps[tap:tap + 1]
    return _silu(y)


def _in_proj_kernel(*refs, rope, halo, seq):
    refs = list(refs)
    x_ref = refs.pop(0)
    xp_ref, xn_ref = (refs.pop(0), refs.pop(0)) if halo else (None, None)
    mod_ref, g_ref, w_ref, wgab_ref, gq_ref, gk_ref, cw_ref = (refs.pop(0) for _ in range(7))
    cos_ref, sin_ref = (refs.pop(0), refs.pop(0)) if rope else (None, None)
    q_ref, k_ref, v_ref, cqkv_ref, dz_ref, ba_ref, gab_ref = refs
    tm = x_ref.shape[0]
    shift, scale, _ = _mod_parts(mod_ref, True)
    h = _rms_mod(x_ref[...], g_ref[...], shift, scale).astype(BF16)

    qkv = _dot(h, w_ref[:, 0:_QKV_W])
    aq = qkv[:, :ATTN_Q_W]
    ak = qkv[:, ATTN_Q_W:ATTN_Q_W + ATTN_KV_W]
    av = qkv[:, ATTN_Q_W + ATTN_KV_W:]
    qn = aq * lax.rsqrt(_head_sumsq(aq, HEAD_DIM) * (1.0 / HEAD_DIM) + EPS) * gq_ref[...]
    kn = ak * lax.rsqrt(_head_sumsq(ak, HEAD_DIM) * (1.0 / HEAD_DIM) + EPS) * gk_ref[...]
    if rope:
        cos, sin = cos_ref[...], sin_ref[...]
        qn = _rope(qn, cos, sin)
        k_ref[...] = _rope(kn, cos, sin)
        v_ref[...] = av
    else:
        kt = kn.T
        vt = av.T
        for s in range(tm // seq):
            for hd in range(N_KV_HEADS):
                k_ref[s, hd] = kt[hd * HEAD_DIM:(hd + 1) * HEAD_DIM, s * seq:(s + 1) * seq]
                v_ref[s, hd] = vt[hd * HEAD_DIM:(hd + 1) * HEAD_DIM, s * seq:(s + 1) * seq]
    q_ref[...] = (qn * (HEAD_DIM ** -0.5 * LOG2_E)).astype(BF16)

    dn = _dot(h, w_ref[:, _DN_OFF:_BA_OFF])
    dz_ref[...] = dn[:, 3 * DN_W:]
    ba_ref[...] = _dot(h, w_ref[:, _BA_OFF:_BA_END])
    gab_ref[...] = _dot(h, wgab_ref[...]).astype(BF16)

    zeros = jnp.zeros((SUBLANES, 3 * DN_W), F32)
    if halo:
        tiles_per_seq = seq // tm
        pos = pl.program_id(0) % tiles_per_seq
        xh = jnp.concatenate([xp_ref[...], xn_ref[...]], axis=0)
        hh = _rms_mod(xh, g_ref[...], shift, scale).astype(BF16)
        dh = _dot(hh, w_ref[:, _DN_OFF:_DN_OFF + 3 * DN_W])
        edges = [(jnp.where(pos > 0, dh[:SUBLANES], 0.0), jnp.where(pos < tiles_per_seq - 1, dh[SUBLANES:], 0.0))]
        sub = tm
    else:
        sub = seq
        edges = [(zeros, zeros)] * (tm // seq)
    for s, (prev, nxt) in enumerate(edges):
        rows = slice(s * sub, (s + 1) * sub)
        for part in range(3):
            cols = slice(part * DN_W, (part + 1) * DN_W)
            y = _conv_silu(prev[:, cols], dn[rows, cols], nxt[:, cols], cw_ref[:, cols])
            if part == 0:
                y = y * lax.rsqrt(_head_sumsq(y, DN_DK) + EPS) * (DN_DK ** -0.5)
            elif part == 1:
                y = y * lax.rsqrt(_head_sumsq(y, DN_DK) + EPS)
            cqkv_ref[rows, cols] = y


def _in_proj(x, mod, group_of_tile, lw, conv_w, layer, tm, seq, rope_tables=None):
    t = x.shape[0]
    rope = rope_tables is not None
    halo = tm < seq
    assert tm % seq == 0 or seq % tm == 0
    row = lambda w: pl.BlockSpec((tm, w), lambda i: (i, 0))
    in_specs = [row(D_MODEL)]
    args = [x]
    if halo:
        blocks_per_tile = tm // SUBLANES
        last_block = t // SUBLANES - 1
        in_specs += [pl.BlockSpec((SUBLANES, D_MODEL), lambda i: (jnp.maximum(i * blocks_per_tile - 1, 0), 0)),
                     pl.BlockSpec((SUBLANES, D_MODEL),
                                  lambda i: (jnp.minimum((i + 1) * blocks_per_tile, last_block), 0))]
        args += [x, x]
    in_specs += [pl.BlockSpec((None, 1, 6 * D_MODEL), lambda i: (group_of_tile(i), 0, 0)),
                 _resident((1, D_MODEL)),
                 pl.BlockSpec((None, D_MODEL, _BA_END), lambda i: (layer, 0, 0), pipeline_mode=pl.Buffered(1)),
                 _resident(lw["w_gab"].shape),
                 _resident((1, ATTN_Q_W)), _resident((1, ATTN_KV_W)),
                 pl.BlockSpec((None, DN_CONV, 3 * DN_W), lambda i: (layer, 0, 0), pipeline_mode=pl.Buffered(1))]
    args += [mod, lw["norm1_g"], lw["w_in"], lw["w_gab"], lw["gq"], lw["gk"], conv_w]
    if rope:
        cos, sin = rope_tables
        tiles_per_seq = cos.shape[0] // tm
        in_specs += [pl.BlockSpec((tm, LANES), lambda i: (i % tiles_per_seq, 0))] * 2
        args += [cos, sin]
        kv_spec = row(ATTN_KV_W)
        kv_shape = jax.ShapeDtypeStruct((t, ATTN_KV_W), F32)
    else:
        kv_spec = pl.BlockSpec((tm // seq, N_KV_HEADS, HEAD_DIM, seq), lambda i: (i, 0, 0, 0))
        kv_shape = jax.ShapeDtypeStruct((t // seq, N_KV_HEADS, HEAD_DIM, seq), F32)
    widths = (3 * DN_W, DN_W, LANES)
    return pl.pallas_call(
        functools.partial(_in_proj_kernel, rope=rope, halo=halo, seq=seq),
        grid=(t // tm,),
        in_specs=in_specs,
        out_specs=[row(ATTN_Q_W), kv_spec, kv_spec] + [row(w) for w in widths] + [row(2 * D_MODEL)],
        out_shape=[jax.ShapeDtypeStruct((t, ATTN_Q_W), BF16), kv_shape, kv_shape]
                  + [jax.ShapeDtypeStruct((t, w), F32) for w in widths]
                  + [jax.ShapeDtypeStruct((t, 2 * D_MODEL), BF16)],
        compiler_params=_params(),
        name="in_proj_rope" if rope else "in_proj",
    )(*args)


def _attn_kernel(*refs, transposed, n_seq):
    n_parts = len(transposed)
    q_ref = refs[0]
    kv_refs = refs[1:1 + 2 * n_parts]
    o_ref = refs[1 + 2 * n_parts]
    tq = q_ref.shape[0] // n_seq
    qt = q_ref[...].astype(F32).T.astype(BF16)
    chains = [(s, kvh) for s in range(n_seq) for kvh in range(N_KV_HEADS)]
    qgs = [jnp.concatenate([qt[j * HEAD_DIM:(j + 1) * HEAD_DIM, s * tq:(s + 1) * tq]
                            for j in range(kvh * Q_GROUP, (kvh + 1) * Q_GROUP)], axis=1) for s, kvh in chains]

    def keys(p, s, kvh):
        k_ref = kv_refs[2 * p]
        k = k_ref[s, kvh].T if transposed[p] else k_ref[:, kvh * HEAD_DIM:(kvh + 1) * HEAD_DIM]
        return k.astype(BF16)

    def values_t(p, s, kvh):
        v_ref = kv_refs[2 * p + 1]
        v = v_ref[s, kvh] if transposed[p] else v_ref[:, kvh * HEAD_DIM:(kvh + 1) * HEAD_DIM].T
        return v.astype(BF16)

    ss = [[_dot(keys(p, s, kvh), qg) for p in range(n_parts)] for (s, kvh), qg in zip(chains, qgs)]
    ms = [functools.reduce(jnp.maximum, [jnp.max(sc, axis=0, keepdims=True) for sc in sp]) for sp in ss]
    ps = [[jnp.exp2(sc - m) for sc in sp] for sp, m in zip(ss, ms)]
    dens = [functools.reduce(jnp.add, [jnp.sum(p, axis=0, keepdims=True) for p in pp]) for pp in ps]
    accs = [functools.reduce(jnp.add, [_dot(values_t(p, s, kvh), pr.astype(BF16)) for p, pr in enumerate(pp)])
            for (s, kvh), pp in zip(chains, ps)]
    for s in range(n_seq):
        outs = []
        for kvh in range(N_KV_HEADS):
            o = accs[s * N_KV_HEADS + kvh] / dens[s * N_KV_HEADS + kvh]
            outs += [o[:, g * tq:(g + 1) * tq] for g in range(Q_GROUP)]
        o_ref[s * tq:(s + 1) * tq, :] = jnp.concatenate(outs, axis=0).T.astype(BF16)


def _attention(q, parts, seq_q, tq, n_seq=1):
    t = q.shape[0]
    nq = seq_q // tq
    assert n_seq == 1 or (nq == 1 and all(k.ndim == 4 for k, _, _ in parts))
    in_specs = [pl.BlockSpec((n_seq * tq, ATTN_Q_W), lambda b, i: (b * nq + i, 0))]
    args = [q]
    transposed = []
    for k, v, seq_k in parts:
        transposed.append(k.ndim == 4)
        if k.ndim == 4:
            in_specs += [pl.BlockSpec((n_seq, N_KV_HEADS, HEAD_DIM, seq_k), lambda b, i: (b, 0, 0, 0))] * 2
        else:
            in_specs += [pl.BlockSpec((seq_k, ATTN_KV_W), lambda b, i: (b, 0))] * 2
        args += [k, v]
    return pl.pallas_call(
        functools.partial(_attn_kernel, transposed=tuple(transposed), n_seq=n_seq),
        grid=(t // (seq_q * n_seq), nq),
        in_specs=in_specs,
        out_specs=pl.BlockSpec((n_seq * tq, ATTN_Q_W), lambda b, i: (b * nq + i, 0)),
        out_shape=jax.ShapeDtypeStruct((t, ATTN_Q_W), BF16),
        compiler_params=_params(2),
        name="attention_%dparts" % len(parts),
    )(*args)


def _block_diag(x, mask):
    reps = LANES // DN_DK
    return [jnp.where(mask, jnp.concatenate([x[:, c * LANES:(c + 1) * LANES]] * reps, axis=0),
                      jnp.zeros((), x.dtype)) for c in range(x.shape[1] // LANES)]


def _bdot(x, tiles, nt=False):
    op = _dot_nt if nt else _dot
    return jnp.concatenate([op(x[:, c * LANES:(c + 1) * LANES], t) for c, t in enumerate(tiles)], axis=1)


def _heads_transposed(x):
    xt = x.T
    return jnp.concatenate([xt[hb * DN_DK:(hb + 1) * DN_DK] for hb in range(HEADS_PER_GROUP)], axis=1)


def _packed_unit_inverse(lms, eye, row, col, mask):
    def mm1(x, y):
        return _bdot(x, _block_diag(y, mask))

    b = INVERSE_BASE
    ns = [-jnp.where((row // b) == (col // b), lm, 0.0) for lm in lms]
    ts = [eye + n for n in ns]
    ps = [mm1(nb, nb) for nb in (n.astype(BF16) for n in ns)]
    steps = int(math.log2(b)) - 1
    for s in range(steps):
        last = s == steps - 1
        pbs = [p.astype(BF16) for p in ps]
        prods = [mm1(t.astype(BF16) if last else jnp.concatenate([t.astype(BF16), pb], axis=0), pb)
                 for t, pb in zip(ts, pbs)]
        ts = [t + prod[:CHUNK] for t, prod in zip(ts, prods)]
        if not last:
            ps = [prod[CHUNK:] for prod in prods]
    while b < CHUNK:
        between = ((row // (2 * b)) == (col // (2 * b))) & ((row // b) != (col // b))
        tbs = [t.astype(BF16) for t in ts]
        ys = [mm1(tb, jnp.where(between, lm, 0.0).astype(BF16)) for lm, tb in zip(lms, tbs)]
        ts = [t - mm1(y.astype(BF16), tb) for t, tb, y in zip(ts, tbs, ys)]
        b *= 2
    return ts


def _delta_kernel(*refs, seq, n_seq, has_init):
    if has_init:
        (x_ref, z_ref, ba_ref, alog_ref, dtb_ref, gn_ref, s0f_ref, s0b_ref,
         o_ref, sf_ref, sb_ref, gate_s, o_s, st_s) = refs
    else:
        (x_ref, z_ref, ba_ref, alog_ref, dtb_ref, gn_ref,
         o_ref, sf_ref, sb_ref, gate_s, o_s, st_s) = refs
    n_chunks = seq // CHUNK
    rb = MXU_DIM
    n_gate = 4 * DN_HEADS

    exp_r = _iota((LANES, 4 * DN_W), 0)
    expand = ((exp_r < 3 * n_gate) & ((_iota((LANES, 4 * DN_W), 1) // DN_DK) == exp_r % n_gate)).astype(BF16)
    lane = _iota((1, LANES), 1)
    blk_r = _iota((rb, rb), 0)
    blk_c = _iota((rb, rb), 1)
    same_chunk = (blk_r // CHUNK) == (blk_c // CHUNK)
    cum_f = (same_chunk & (blk_c <= blk_r)).astype(BF16)
    cum_b = (same_chunk & (blk_c >= blk_r)).astype(BF16)
    for blk in range(n_seq * seq // rb):
        rows = slice(blk * rb, (blk + 1) * rb)
        ba = ba_ref[rows, :]
        decay = -jnp.exp(alog_ref[...]) * _softplus(ba + dtb_ref[...])
        vals = jnp.where(lane < 2 * DN_HEADS, _sigmoid(ba), jnp.where(lane < n_gate, decay, 0.0))
        narrow = jnp.where(lane < 2 * DN_HEADS, vals,
                           jnp.where(lane < 3 * DN_HEADS, _dot_exact_lhs(cum_f, vals),
                                     jnp.where(lane < n_gate, _dot_exact_lhs(cum_b, vals), 0.0)))
        t1, t2, t3 = (t.astype(F32) for t in _split3(narrow))
        stacked = t1 + pltpu.roll(t2, n_gate, axis=1) + pltpu.roll(t3, 2 * n_gate, axis=1)
        gate_s[rows, :] = _dot(stacked.astype(BF16), expand)

    bd_mask = _same_block((LANES, LANES), DN_DK)
    row = _iota((CHUNK, GW), 0)
    col = _iota((CHUNK, GW), 1) % CHUNK
    diag = row == col
    eye = diag.astype(F32)
    dirs = ((col <= row, col < row, CHUNK - 1), (col >= row, col > row, 0))
    chains = [(s, d, g) for s in range(n_seq) for d in range(2) for g in range(N_GROUPS)]
    for ci, (s, d, g) in enumerate(chains):
        if has_init:
            st_s[ci] = (s0f_ref, s0b_ref)[d][s, :, g * GW:(g + 1) * GW]
        else:
            st_s[ci] = jnp.zeros((DN_DK, GW), F32)

    def bd(x):
        return _block_diag(x.astype(BF16), bd_mask)

    def chunk_step(n, carry):
        where = []
        for s, d, g in chains:
            c = n if d == 0 else n_chunks - 1 - n
            where.append((pl.ds(pl.multiple_of(s * seq + c * CHUNK, CHUNK), CHUNK), d, g))

        def load(ci, what):
            rows, d, g = where[ci]
            off = {"q": g * GW, "k": DN_W + g * GW, "v": 2 * DN_W + g * GW}
            if what in off:
                return x_ref[rows, pl.ds(off[what], GW)]
            return gate_s[rows, pl.ds((0 if what == "beta" else 2 * DN_W) + d * DN_W + g * GW, GW)]

        n_ch = len(chains)
        grams, decs = [], []
        for ci, (rows, d, g) in enumerate(where):
            incl = dirs[d][0]
            gc = load(ci, "gc")
            gc_col = jnp.sum(jnp.where(diag, gc, 0.0), axis=0, keepdims=True)
            decs.append(jnp.where(incl, jnp.exp(jnp.minimum(gc - gc_col, 0.0)), 0.0))
            k = load(ci, "k")
            lhs = jnp.concatenate([k * load(ci, "beta"), load(ci, "q")], axis=0).astype(BF16)
            grams.append(_bdot(lhs, bd(k), nt=True))
        lms = [jnp.where(dirs[d][1], gm[:CHUNK] * dec, 0.0) for (_, d, _), gm, dec in zip(where, grams, decs)]
        attn = [(gm[CHUNK:] * dec).astype(BF16) for gm, dec in zip(grams, decs)]
        ts = [t.astype(BF16) for t in _packed_unit_inverse(lms, eye, row, col, bd_mask)]
        us, ws = [], []
        for ci, t in enumerate(ts):
            beta = load(ci, "beta")
            us.append(_bdot(t, bd(load(ci, "v") * beta)))
            ws.append(_bdot(t, bd(load(ci, "k") * beta * jnp.exp(load(ci, "gc")))))
        states = [st_s[ci] for ci in range(n_ch)]
        ws_qs = [_bdot(jnp.concatenate([w, load(ci, "q") * jnp.exp(load(ci, "gc"))], axis=0).astype(BF16), bd(st))
                 for ci, (w, st) in enumerate(zip(ws, states))]
        g_lasts, lhs2 = [], []
        for ci, (rows, d, g) in enumerate(where):
            gc = load(ci, "gc")
            last_row = dirs[d][2]
            g_last = gc[last_row:last_row + 1, :]
            g_lasts.append(g_last)
            k_dec = load(ci, "k") * jnp.exp(g_last - gc)
            lhs2.append(jnp.concatenate([attn[ci], _heads_transposed(k_dec).astype(BF16)], axis=0))
        v_bds = [bd(u - x[:CHUNK]) for u, x in zip(us, ws_qs)]
        avs = [_bdot(l2, vb) for l2, vb in zip(lhs2, v_bds)]
        for ci, (rows, d, g) in enumerate(where):
            st_s[ci] = states[ci] * jnp.exp(g_lasts[ci]) + avs[ci][CHUNK:]
            o_s[d, rows, g * GW:(g + 1) * GW] = ws_qs[ci][CHUNK:] + avs[ci][:CHUNK]
        return carry

    lax.fori_loop(0, n_chunks, chunk_step, 0)

    for blk in range(n_seq * seq // rb):
        rows = slice(blk * rb, (blk + 1) * rb)
        o = o_s[0, rows, :] + o_s[1, rows, :]
        o = o * lax.rsqrt(_head_sumsq(o, DN_DV) * (1.0 / DN_DV) + EPS) * gn_ref[...]
        o_ref[rows, :] = (o * _silu(z_ref[rows, :])).astype(BF16)
    for ci, (s, d, g) in enumerate(chains):
        st = st_s[ci]
        for hb in range(HEADS_PER_GROUP):
            (sf_ref, sb_ref)[d][s, g * HEADS_PER_GROUP + hb] = st[:, hb * DN_DV:(hb + 1) * DN_DV]


def _delta(cqkv, dz, ba, lw, seq, n_seq, init=None):
    t = cqkv.shape[0]
    nb = t // seq
    rows = n_seq * seq
    has_init = init is not None
    seq_block = lambda w: pl.BlockSpec((rows, w), lambda b: (b, 0))
    state_block = pl.BlockSpec((n_seq, DN_DK, DN_W), lambda b: (b, 0, 0))
    in_specs = [seq_block(3 * DN_W), seq_block(DN_W), seq_block(LANES),
                _resident((1, LANES)), _resident((1, LANES)), _resident((1, DN_W))]
    args = [cqkv, dz, ba, lw["a_log"], lw["dt_bias"], lw["gn"]]
    if has_init:
        in_specs += [state_block, state_block]
        args += list(init)
    final_block = pl.BlockSpec((n_seq, DN_HEADS, DN_DK, DN_DV), lambda b: (b, 0, 0, 0))
    final_shape = jax.ShapeDtypeStruct((nb, DN_HEADS, DN_DK, DN_DV), F32)
    return pl.pallas_call(
        functools.partial(_delta_kernel, seq=seq, n_seq=n_seq, has_init=has_init),
        grid=(nb // n_seq,),
        in_specs=in_specs,
        out_specs=[seq_block(DN_W), final_block, final_block],
        out_shape=[jax.ShapeDtypeStruct((t, DN_W), BF16), final_shape, final_shape],
        scratch_shapes=[pltpu.VMEM((rows, 4 * DN_W), F32),
                        pltpu.VMEM((2, rows, DN_W), F32),
                        pltpu.VMEM((n_seq * 2 * N_GROUPS, DN_DK, GW), F32)],
        compiler_params=_params(),
        name="delta_rule_init" if has_init else "delta_rule",
    )(*args)


def _post_kernel(x_ref, oa_ref, od_ref, gab_ref, mod_ref, g2_ref, wpa_ref, wpb_ref, wo_ref,
                 wg_ref, wu_ref, wd_ref, y_ref):
    _, _, gate1 = _mod_parts(mod_ref, True)
    shift2, scale2, gate2 = _mod_parts(mod_ref, False)
    gab = gab_ref[...].astype(F32)
    merged = (_sigmoid(gab[:, :D_MODEL]) * _dot(oa_ref[...], wpa_ref[...])
              + _sigmoid(gab[:, D_MODEL:]) * _dot(od_ref[...], wpb_ref[...]))
    x1 = x_ref[...] + gate1 * _dot(merged.astype(BF16), wo_ref[...])
    h2 = _rms_mod(x1, g2_ref[...], shift2, scale2).astype(BF16)
    act = _silu(_dot(h2, wg_ref[...])) * _dot(h2, wu_ref[...])
    y_ref[...] = x1 + gate2 * _dot(act.astype(BF16), wd_ref[...])


def _post(x, oa, od, gab, mod, group_of_tile, lw, tm):
    t = x.shape[0]
    row = lambda w: pl.BlockSpec((tm, w), lambda i: (i, 0))
    weights = [lw[k] for k in ("w_pa", "w_pb", "w_o", "w_gate", "w_up", "w_down")]
    return pl.pallas_call(
        _post_kernel,
        grid=(t // tm,),
        in_specs=[row(D_MODEL), row(ATTN_Q_W), row(DN_W), row(2 * D_MODEL),
                  pl.BlockSpec((None, 1, 6 * D_MODEL), lambda i: (group_of_tile(i), 0, 0)),
                  _resident((1, D_MODEL))] + [_resident(w.shape) for w in weights],
        out_specs=row(D_MODEL),
        out_shape=jax.ShapeDtypeStruct((t, D_MODEL), F32),
        compiler_params=_params(),
        name="post_block",
    )(x, oa, od, gab, mod, lw["norm2_g"], *weights)


def _rope_tables(n_tokens):
    quarter = HEAD_DIM // 4
    lane = jnp.arange(LANES)
    d = lane % HEAD_DIM
    inv = ROPE_THETA ** (-(d % quarter).astype(F32) / quarter)
    t = jnp.arange(n_tokens)
    pos = jnp.where(d[None, :] < HEAD_DIM // 2, (t // GRID_W)[:, None], (t % GRID_W)[:, None]).astype(F32)
    ang = pos * inv[None, :]
    sign = jnp.where((d % (2 * quarter)) < quarter, -1.0, 1.0)
    return jnp.cos(ang), jnp.sin(ang) * sign[None, :]


def _pack_states(s):
    b = s.shape[0]
    return s.transpose(0, 2, 1, 3).reshape(b, DN_DK, DN_W)


def _layer_weights(l, w_in, norm1_g, q_norm_g, k_norm_g, a_log, dt_bias, dn_norm_g,
                   w_pa, w_pb, w_o, norm2_g, w_gate, w_up, w_down):
    pad_small = lambda a: jnp.pad(a.reshape(1, -1), ((0, 0), (0, LANES - a.size)))
    return dict(
        w_in=w_in,
        w_gab=w_in[l, :, _GATE_OFF:],
        norm1_g=norm1_g[l].reshape(1, D_MODEL),
        norm2_g=norm2_g[l].reshape(1, D_MODEL),
        gq=jnp.tile(q_norm_g[l], N_Q_HEADS).reshape(1, ATTN_Q_W),
        gk=jnp.tile(k_norm_g[l], N_KV_HEADS).reshape(1, ATTN_KV_W),
        a_log=pad_small(jnp.concatenate([jnp.zeros((2 * DN_HEADS,), F32), a_log[l].reshape(-1)])),
        dt_bias=pad_small(jnp.concatenate([jnp.zeros((2 * DN_HEADS,), F32), dt_bias[l].reshape(-1)])),
        gn=jnp.tile(dn_norm_g[l], DN_HEADS).reshape(1, DN_W),
        w_pa=w_pa[l].astype(BF16), w_pb=w_pb[l].astype(BF16), w_o=w_o[l].astype(BF16),
        w_gate=w_gate[l].astype(BF16), w_up=w_up[l].astype(BF16), w_down=w_down[l].astype(BF16),
    )


def kernel(x_prompt, x_sample, cache_k, cache_v, state_fwd, state_bwd, c, c_ctx, w_ada, b_ada, norm1_g, w_in,
           q_norm_g, k_norm_g, conv_w, a_log, dt_bias, dn_norm_g, w_pa, w_pb, w_o, norm2_g, w_gate, w_up, w_down):
    batch, seq, _ = x_prompt.shape
    dec_batch, dec_seq, _ = x_sample.shape
    depth = w_in.shape[0]
    past = cache_k.shape[2]
    assert dec_batch + 1 <= SUBLANES and seq % MXU_DIM == 0 and dec_seq % MXU_DIM == 0
    assert batch % SEQS_PER_STEP == 0 and dec_batch % SEQS_PER_STEP == 0 and batch % ATTN_SEQS_PER_STEP == 0

    cvecs = jnp.zeros((SUBLANES, D_MODEL), F32).at[0].set(c_ctx).at[1:1 + dec_batch].set(c)
    rope_tables = _rope_tables(dec_seq)
    tm_in = IN_PROJ_TILE
    tm_post = POST_TILE
    ctx_group = lambda i: 0

    def lat_group(tm):
        return lambda i: 1 + i // (dec_seq // tm)

    yp = x_prompt.reshape(batch * seq, D_MODEL)
    ys = x_sample.reshape(dec_batch * dec_seq, D_MODEL)
    ks_out, vs_out, sf_out, sb_out = [], [], [], []
    w_in = w_in.astype(BF16)
    for l in range(depth):
        lw = _layer_weights(l, w_in, norm1_g, q_norm_g, k_norm_g, a_log, dt_bias, dn_norm_g,
                            w_pa, w_pb, w_o, norm2_g, w_gate, w_up, w_down)
        mod = _modulation(cvecs, w_ada[l], b_ada[l])[:1 + dec_batch].reshape(1 + dec_batch, 1, 6 * D_MODEL)

        q, kt, vt, cqkv, dz, ba, gab = _in_proj(yp, mod, ctx_group, lw, conv_w, l, tm_in, seq)
        oa = _attention(q, [(kt, vt, seq)], seq, seq, ATTN_SEQS_PER_STEP)
        od, sf, sb = _delta(cqkv, dz, ba, lw, seq, SEQS_PER_STEP)
        yp = _post(yp, oa, od, gab, mod, ctx_group, lw, tm_post)
        ks_out.append(kt.transpose(0, 3, 1, 2))
        vs_out.append(vt.transpose(0, 3, 1, 2))
        sf_out.append(sf)
        sb_out.append(sb)

        q, kr, v, cqkv, dz, ba, gab = _in_proj(ys, mod, lat_group(tm_in), lw, conv_w, l, tm_in, dec_seq,
                                               rope_tables)
        ck = cache_k[:, l].reshape(dec_batch * past, ATTN_KV_W)
        cv = cache_v[:, l].reshape(dec_batch * past, ATTN_KV_W)
        oa = _attention(q, [(ck, cv, past), (kr, v, dec_seq)], dec_seq, ATTN_Q_TILE)
        init = (_pack_states(state_fwd[:, l]), _pack_states(state_bwd[:, l]))
        od, _, _ = _delta(cqkv, dz, ba, lw, dec_seq, SEQS_PER_STEP, init)
        ys = _post(ys, oa, od, gab, mod, lat_group(tm_post), lw, tm_post)

    return (yp.reshape(batch, seq, D_MODEL), ys.reshape(dec_batch, dec_seq, D_MODEL),
            jnp.stack(ks_out, axis=1), jnp.stack(vs_out, axis=1),
            jnp.stack(sf_out, axis=1), jnp.stack(sb_out, axis=1))
```

```python
import functools
import math

import jax
import jax.numpy as jnp
from jax import lax
from jax.experimental import pallas as pl
from jax.experimental.pallas import tpu as pltpu

D_MODEL = 1024
GRID_W = 64
HEAD_DIM = 64
N_Q_HEADS = 8
N_KV_HEADS = 2
Q_GROUP = N_Q_HEADS // N_KV_HEADS
ATTN_Q_W = N_Q_HEADS * HEAD_DIM
ATTN_KV_W = N_KV_HEADS * HEAD_DIM
ROPE_THETA = 10000.0
DN_HEADS = 8
DN_DK = 64
DN_DV = 64
DN_W = DN_HEADS * DN_DK
DN_CONV = 5
CHUNK = 64
EPS = 1e-6
LOG2_E = math.log2(math.e)

LANES = 128
SUBLANES = 8
MXU_DIM = 256
VMEM_LIMIT_BYTES = 56 * 1024 * 1024

HEADS_PER_GROUP = MXU_DIM // DN_DK
N_GROUPS = DN_HEADS // HEADS_PER_GROUP
GW = HEADS_PER_GROUP * DN_DK
INVERSE_BASE = 8

IN_PROJ_TILE = 512
POST_TILE = 256
ATTN_Q_TILE = 256
SEQS_PER_STEP = 2
ATTN_SEQS_PER_STEP = 4
ADALN_TILE = 2048
CAST_ROWS = 256

_QKV_W = ATTN_Q_W + 2 * ATTN_KV_W
_DN_OFF = _QKV_W
_BA_OFF = _DN_OFF + 4 * DN_W
_GATE_OFF = _BA_OFF + 4 * DN_HEADS
_BA_END = _BA_OFF + LANES

F32 = jnp.float32
BF16 = jnp.bfloat16


def _dot(a, b):
    return jnp.dot(a, b, preferred_element_type=F32)


def _dot_nt(a, b):
    return lax.dot_general(a, b, (((1,), (1,)), ((), ())), preferred_element_type=F32)


def _split2(x):
    hi = x.astype(BF16)
    lo = (x - hi.astype(F32)).astype(BF16)
    return hi, lo


def _split3(x):
    hi = x.astype(BF16)
    r = x - hi.astype(F32)
    mid = r.astype(BF16)
    lo = (r - mid.astype(F32)).astype(BF16)
    return hi, mid, lo


def _dot_exact_lhs(a01, b):
    n = b.shape[1]
    b1, b2, b3 = _split3(b)
    r = _dot(a01, jnp.concatenate([b1, b2, b3], axis=1))
    return r[:, :n] + r[:, n:2 * n] + r[:, 2 * n:]


def _iota(shape, dim):
    return lax.broadcasted_iota(jnp.int32, shape, dim)


def _same_block(shape, width):
    return (_iota(shape, 0) // width) == (_iota(shape, 1) // width)


def _head_sumsq(x, width):
    m, n = x.shape
    slab = min(n, MXU_DIM)
    sel = _same_block((slab, slab), width).astype(BF16)
    outs = []
    for s in range(n // slab):
        xs = x[:, s * slab:(s + 1) * slab]
        outs.append(_dot((xs * xs).astype(BF16), sel))
    return outs[0] if len(outs) == 1 else jnp.concatenate(outs, axis=1)


def _sigmoid(x):
    return 1.0 / (1.0 + jnp.exp(-x))


def _silu(x):
    return x * _sigmoid(x)


def _softplus(x):
    return jnp.maximum(x, 0.0) + jnp.log(1.0 + jnp.exp(-jnp.abs(x)))


def _resident(shape):
    nd = len(shape)
    return pl.BlockSpec(shape, lambda *_: (0,) * nd, pipeline_mode=pl.Buffered(1))


def _params(n_axes=1):
    return pltpu.CompilerParams(dimension_semantics=("arbitrary",) * n_axes,
                                vmem_limit_bytes=VMEM_LIMIT_BYTES)


def _mod_kernel(c_ref, w_ref, b_ref, o_ref):
    m = c_ref.shape[0]
    hi, lo = _split2(_silu(c_ref[...]))
    r = _dot(jnp.concatenate([hi, lo], axis=0), w_ref[...].astype(BF16))
    o_ref[...] = r[:m] + r[m:] + b_ref[...]


def _modulation(cvecs, w_ada, b_ada):
    n = w_ada.shape[1]
    tn = ADALN_TILE
    return pl.pallas_call(
        _mod_kernel,
        grid=(n // tn,),
        in_specs=[pl.BlockSpec((SUBLANES, D_MODEL), lambda j: (0, 0)),
                  pl.BlockSpec((D_MODEL, tn), lambda j: (0, j)),
                  pl.BlockSpec((1, tn), lambda j: (0, j))],
        out_specs=pl.BlockSpec((SUBLANES, tn), lambda j: (0, j)),
        out_shape=jax.ShapeDtypeStruct((SUBLANES, n), F32),
        compiler_params=_params(),
        name="adaln_modulation",
    )(cvecs, w_ada, b_ada.reshape(1, n))


def _mod_parts(mod_ref, first):
    m = mod_ref[...]
    base = 0 if first else 3 * D_MODEL
    return (m[:, base:base + D_MODEL], m[:, base + D_MODEL:base + 2 * D_MODEL],
            m[:, base + 2 * D_MODEL:base + 3 * D_MODEL])


def _rms_mod(x, g, shift, scale):
    ms = jnp.mean(x * x, axis=-1, keepdims=True)
    return (x * lax.rsqrt(ms + EPS) * g) * (1.0 + scale) + shift


def _rope(x, cos, sin_signed):
    outs = []
    half = HEAD_DIM // 4
    first_half = (_iota((1, LANES), 1) % (2 * half)) < half
    for s in range(x.shape[1] // LANES):
        xs = x[:, s * LANES:(s + 1) * LANES]
        partner = jnp.where(first_half, pltpu.roll(xs, LANES - half, axis=1), pltpu.roll(xs, half, axis=1))
        outs.append(xs * cos + partner * sin_signed)
    return outs[0] if len(outs) == 1 else jnp.concatenate(outs, axis=1)


def _conv_silu(prev, x, nxt, taps):
    rows = x.shape[0]
    xe = jnp.concatenate([prev, x, nxt], axis=0)
    ne = rows + 2 * SUBLANES
    half = (DN_CONV - 1) // 2
    y = jnp.zeros(x.shape, F32)
    for tap in range(DN_CONV):
        d = tap - half
        sh = xe if d == 0 else pltpu.roll(xe, (ne - d) % ne, axis=0)
        y = y + sh[SUBLANES:SUBLANES + rows] * taps[tap:tap + 1]
    return _silu(y)


def _in_proj_kernel(*refs, rope, halo, seq):
    refs = list(refs)
    x_ref = refs.pop(0)
    xp_ref, xn_ref = (refs.pop(0), refs.pop(0)) if halo else (None, None)
    mod_ref, g_ref, w_ref, wgab_ref, gq_ref, gk_ref, cw_ref = (refs.pop(0) for _ in range(7))
    cos_ref, sin_ref = (refs.pop(0), refs.pop(0)) if rope else (None, None)
    q_ref, k_ref, v_ref, cqkv_ref, dz_ref, ba_ref, gab_ref = refs
    tm = x_ref.shape[0]
    shift, scale, _ = _mod_parts(mod_ref, True)
    h = _rms_mod(x_ref[...], g_ref[...], shift, scale).astype(BF16)

    qkv = _dot(h, w_ref[:, 0:_QKV_W])
    aq = qkv[:, :ATTN_Q_W]
    ak = qkv[:, ATTN_Q_W:ATTN_Q_W + ATTN_KV_W]
    av = qkv[:, ATTN_Q_W + ATTN_KV_W:]
    qn = aq * lax.rsqrt(_head_sumsq(aq, HEAD_DIM) * (1.0 / HEAD_DIM) + EPS) * gq_ref[...]
    kn = ak * lax.rsqrt(_head_sumsq(ak, HEAD_DIM) * (1.0 / HEAD_DIM) + EPS) * gk_ref[...]
    if rope:
        cos, sin = cos_ref[...], sin_ref[...]
        qn = _rope(qn, cos, sin)
        k_ref[...] = _rope(kn, cos, sin)
        v_ref[...] = av
    else:
        kt = kn.T
        vt = av.T
        for s in range(tm // seq):
            for hd in range(N_KV_HEADS):
                k_ref[s, hd] = kt[hd * HEAD_DIM:(hd + 1) * HEAD_DIM, s * seq:(s + 1) * seq]
                v_ref[s, hd] = vt[hd * HEAD_DIM:(hd + 1) * HEAD_DIM, s * seq:(s + 1) * seq]
    q_ref[...] = (qn * (HEAD_DIM ** -0.5 * LOG2_E)).astype(BF16)

    dn = _dot(h, w_ref[:, _DN_OFF:_BA_OFF])
    dz_ref[...] = dn[:, 3 * DN_W:]
    ba_ref[...] = _dot(h, w_ref[:, _BA_OFF:_BA_END])
    gab_ref[...] = _dot(h, wgab_ref[...]).astype(BF16)

    zeros = jnp.zeros((SUBLANES, 3 * DN_W), F32)
    if halo:
        tiles_per_seq = seq // tm
        pos = pl.program_id(0) % tiles_per_seq
        xh = jnp.concatenate([xp_ref[...], xn_ref[...]], axis=0)
        hh = _rms_mod(xh, g_ref[...], shift, scale).astype(BF16)
        dh = _dot(hh, w_ref[:, _DN_OFF:_DN_OFF + 3 * DN_W])
        edges = [(jnp.where(pos > 0, dh[:SUBLANES], 0.0), jnp.where(pos < tiles_per_seq - 1, dh[SUBLANES:], 0.0))]
        sub = tm
    else:
        sub = seq
        edges = [(zeros, zeros)] * (tm // seq)
    for s, (prev, nxt) in enumerate(edges):
        rows = slice(s * sub, (s + 1) * sub)
        for part in range(3):
            cols = slice(part * DN_W, (part + 1) * DN_W)
            y = _conv_silu(prev[:, cols], dn[rows, cols], nxt[:, cols], cw_ref[:, cols])
            if part == 0:
                y = y * lax.rsqrt(_head_sumsq(y, DN_DK) + EPS) * (DN_DK ** -0.5)
            elif part == 1:
                y = y * lax.rsqrt(_head_sumsq(y, DN_DK) + EPS)
            cqkv_ref[rows, cols] = y


def _in_proj(x, mod, group_of_tile, lw, conv_w, layer, tm, seq, rope_tables=None):
    t = x.shape[0]
    rope = rope_tables is not None
    halo = tm < seq
    assert tm % seq == 0 or seq % tm == 0
    row = lambda w: pl.BlockSpec((tm, w), lambda i: (i, 0))
    in_specs = [row(D_MODEL)]
    args = [x]
    if halo:
        blocks_per_tile = tm // SUBLANES
        last_block = t // SUBLANES - 1
        in_specs += [pl.BlockSpec((SUBLANES, D_MODEL), lambda i: (jnp.maximum(i * blocks_per_tile - 1, 0), 0)),
                     pl.BlockSpec((SUBLANES, D_MODEL),
                                  lambda i: (jnp.minimum((i + 1) * blocks_per_tile, last_block), 0))]
        args += [x, x]
    in_specs += [pl.BlockSpec((None, 1, 6 * D_MODEL), lambda i: (group_of_tile(i), 0, 0)),
                 _resident((1, D_MODEL)),
                 pl.BlockSpec((None, D_MODEL, _BA_END), lambda i: (layer, 0, 0), pipeline_mode=pl.Buffered(1)),
                 _resident(lw["w_gab"].shape),
                 _resident((1, ATTN_Q_W)), _resident((1, ATTN_KV_W)),
                 pl.BlockSpec((None, DN_CONV, 3 * DN_W), lambda i: (layer, 0, 0), pipeline_mode=pl.Buffered(1))]
    args += [mod, lw["norm1_g"], lw["w_in"], lw["w_gab"], lw["gq"], lw["gk"], conv_w]
    if rope:
        cos, sin = rope_tables
        tiles_per_seq = cos.shape[0] // tm
        in_specs += [pl.BlockSpec((tm, LANES), lambda i: (i % tiles_per_seq, 0))] * 2
        args += [cos, sin]
        kv_spec = row(ATTN_KV_W)
        kv_shape = jax.ShapeDtypeStruct((t, ATTN_KV_W), F32)
    else:
        kv_spec = pl.BlockSpec((tm // seq, N_KV_HEADS, HEAD_DIM, seq), lambda i: (i, 0, 0, 0))
        kv_shape = jax.ShapeDtypeStruct((t // seq, N_KV_HEADS, HEAD_DIM, seq), F32)
    widths = (3 * DN_W, DN_W, LANES)
    return pl.pallas_call(
        functools.partial(_in_proj_kernel, rope=rope, halo=halo, seq=seq),
        grid=(t // tm,),
        in_specs=in_specs,
        out_specs=[row(ATTN_Q_W), kv_spec, kv_spec] + [row(w) for w in widths] + [row(2 * D_MODEL)],
        out_shape=[jax.ShapeDtypeStruct((t, ATTN_Q_W), BF16), kv_shape, kv_shape]
                  + [jax.ShapeDtypeStruct((t, w), F32) for w in widths]
                  + [jax.ShapeDtypeStruct((t, 2 * D_MODEL), BF16)],
        compiler_params=_params(),
        name="in_proj_rope" if rope else "in_proj",
    )(*args)


def _attn_kernel(*refs, transposed, n_seq):
    n_parts = len(transposed)
    q_ref = refs[0]
    kv_refs = refs[1:1 + 2 * n_parts]
    o_ref = refs[1 + 2 * n_parts]
    tq = q_ref.shape[0] // n_seq
    qt = q_ref[...].astype(F32).T.astype(BF16)
    chains = [(s, kvh) for s in range(n_seq) for kvh in range(N_KV_HEADS)]
    qgs = [jnp.concatenate([qt[j * HEAD_DIM:(j + 1) * HEAD_DIM, s * tq:(s + 1) * tq]
                            for j in range(kvh * Q_GROUP, (kvh + 1) * Q_GROUP)], axis=1) for s, kvh in chains]

    def keys(p, s, kvh):
        k_ref = kv_refs[2 * p]
        k = k_ref[s, kvh].T if transposed[p] else k_ref[:, kvh * HEAD_DIM:(kvh + 1) * HEAD_DIM]
        return k.astype(BF16)

    def values_t(p, s, kvh):
        v_ref = kv_refs[2 * p + 1]
        v = v_ref[s, kvh] if transposed[p] else v_ref[:, kvh * HEAD_DIM:(kvh + 1) * HEAD_DIM].T
        return v.astype(BF16)

    ss = [[_dot(keys(p, s, kvh), qg) for p in range(n_parts)] for (s, kvh), qg in zip(chains, qgs)]
    ms = [functools.reduce(jnp.maximum, [jnp.max(sc, axis=0, keepdims=True) for sc in sp]) for sp in ss]
    ps = [[jnp.exp2(sc - m) for sc in sp] for sp, m in zip(ss, ms)]
    dens = [functools.reduce(jnp.add, [jnp.sum(p, axis=0, keepdims=True) for p in pp]) for pp in ps]
    accs = [functools.reduce(jnp.add, [_dot(values_t(p, s, kvh), pr.astype(BF16)) for p, pr in enumerate(pp)])
            for (s, kvh), pp in zip(chains, ps)]
    for s in range(n_seq):
        outs = []
        for kvh in range(N_KV_HEADS):
            o = accs[s * N_KV_HEADS + kvh] / dens[s * N_KV_HEADS + kvh]
            outs += [o[:, g * tq:(g + 1) * tq] for g in range(Q_GROUP)]
        o_ref[s * tq:(s + 1) * tq, :] = jnp.concatenate(outs, axis=0).T.astype(BF16)


def _attention(q, parts, seq_q, tq, n_seq=1):
    t = q.shape[0]
    nq = seq_q // tq
    assert n_seq == 1 or (nq == 1 and all(k.ndim == 4 for k, _, _ in parts))
    in_specs = [pl.BlockSpec((n_seq * tq, ATTN_Q_W), lambda b, i: (b * nq + i, 0))]
    args = [q]
    transposed = []
    for k, v, seq_k in parts:
        transposed.append(k.ndim == 4)
        if k.ndim == 4:
            in_specs += [pl.BlockSpec((n_seq, N_KV_HEADS, HEAD_DIM, seq_k), lambda b, i: (b, 0, 0, 0))] * 2
        else:
            in_specs += [pl.BlockSpec((seq_k, ATTN_KV_W), lambda b, i: (b, 0))] * 2
        args += [k, v]
    return pl.pallas_call(
        functools.partial(_attn_kernel, transposed=tuple(transposed), n_seq=n_seq),
        grid=(t // (seq_q * n_seq), nq),
        in_specs=in_specs,
        out_specs=pl.BlockSpec((n_seq * tq, ATTN_Q_W), lambda b, i: (b * nq + i, 0)),
        out_shape=jax.ShapeDtypeStruct((t, ATTN_Q_W), BF16),
        compiler_params=_params(2),
        name="attention_%dparts" % len(parts),
    )(*args)


def _block_diag(x, mask):
    reps = LANES // DN_DK
    return [jnp.where(mask, jnp.concatenate([x[:, c * LANES:(c + 1) * LANES]] * reps, axis=0),
                      jnp.zeros((), x.dtype)) for c in range(x.shape[1] // LANES)]


def _bdot(x, tiles, nt=False):
    op = _dot_nt if nt else _dot
    return jnp.concatenate([op(x[:, c * LANES:(c + 1) * LANES], t) for c, t in enumerate(tiles)], axis=1)


def _heads_transposed(x):
    xt = x.T
    return jnp.concatenate([xt[hb * DN_DK:(hb + 1) * DN_DK] for hb in range(HEADS_PER_GROUP)], axis=1)


def _packed_unit_inverse(lms, eye, row, col, mask):
    def mm1(x, y):
        return _bdot(x, _block_diag(y, mask))

    b = INVERSE_BASE
    ns = [-jnp.where((row // b) == (col // b), lm, 0.0) for lm in lms]
    ts = [eye + n for n in ns]
    ps = [mm1(nb, nb) for nb in (n.astype(BF16) for n in ns)]
    steps = int(math.log2(b)) - 1
    for s in range(steps):
        last = s == steps - 1
        pbs = [p.astype(BF16) for p in ps]
        prods = [mm1(t.astype(BF16) if last else jnp.concatenate([t.astype(BF16), pb], axis=0), pb)
                 for t, pb in zip(ts, pbs)]
        ts = [t + prod[:CHUNK] for t, prod in zip(ts, prods)]
        if not last:
            ps = [prod[CHUNK:] for prod in prods]
    while b < CHUNK:
        between = ((row // (2 * b)) == (col // (2 * b))) & ((row // b) != (col // b))
        tbs = [t.astype(BF16) for t in ts]
        ys = [mm1(tb, jnp.where(between, lm, 0.0).astype(BF16)) for lm, tb in zip(lms, tbs)]
        ts = [t - mm1(y.astype(BF16), tb) for t, tb, y in zip(ts, tbs, ys)]
        b *= 2
    return ts


def _delta_kernel(*refs, seq, n_seq, has_init):
    if has_init:
        (x_ref, z_ref, ba_ref, alog_ref, dtb_ref, gn_ref, s0f_ref, s0b_ref,
         o_ref, sf_ref, sb_ref, gate_s, o_s, st_s) = refs
    else:
        (x_ref, z_ref, ba_ref, alog_ref, dtb_ref, gn_ref,
         o_ref, sf_ref, sb_ref, gate_s, o_s, st_s) = refs
    n_chunks = seq // CHUNK
    rb = MXU_DIM
    n_gate = 4 * DN_HEADS

    exp_r = _iota((LANES, 4 * DN_W), 0)
    expand = ((exp_r < 3 * n_gate) & ((_iota((LANES, 4 * DN_W), 1) // DN_DK) == exp_r % n_gate)).astype(BF16)
    lane = _iota((1, LANES), 1)
    blk_r = _iota((rb, rb), 0)
    blk_c = _iota((rb, rb), 1)
    same_chunk = (blk_r // CHUNK) == (blk_c // CHUNK)
    cum_f = (same_chunk & (blk_c <= blk_r)).astype(BF16)
    cum_b = (same_chunk & (blk_c >= blk_r)).astype(BF16)
    for blk in range(n_seq * seq // rb):
        rows = slice(blk * rb, (blk + 1) * rb)
        ba = ba_ref[rows, :]
        decay = -jnp.exp(alog_ref[...]) * _softplus(ba + dtb_ref[...])
        vals = jnp.where(lane < 2 * DN_HEADS, _sigmoid(ba), jnp.where(lane < n_gate, decay, 0.0))
        narrow = jnp.where(lane < 2 * DN_HEADS, vals,
                           jnp.where(lane < 3 * DN_HEADS, _dot_exact_lhs(cum_f, vals),
                                     jnp.where(lane < n_gate, _dot_exact_lhs(cum_b, vals), 0.0)))
        t1, t2, t3 = (t.astype(F32) for t in _split3(narrow))
        stacked = t1 + pltpu.roll(t2, n_gate, axis=1) + pltpu.roll(t3, 2 * n_gate, axis=1)
        gate_s[rows, :] = _dot(stacked.astype(BF16), expand)

    bd_mask = _same_block((LANES, LANES), DN_DK)
    row = _iota((CHUNK, GW), 0)
    col = _iota((CHUNK, GW), 1) % CHUNK
    diag = row == col
    eye = diag.astype(F32)
    dirs = ((col <= row, col < row, CHUNK - 1), (col >= row, col > row, 0))
    chains = [(s, d, g) for s in range(n_seq) for d in range(2) for g in range(N_GROUPS)]
    for ci, (s, d, g) in enumerate(chains):
        if has_init:
            st_s[ci] = (s0f_ref, s0b_ref)[d][s, :, g * GW:(g + 1) * GW]
        else:
            st_s[ci] = jnp.zeros((DN_DK, GW), F32)

    def bd(x):
        return _block_diag(x.astype(BF16), bd_mask)

    def chunk_step(n, carry):
        where = []
        for s, d, g in chains:
            c = n if d == 0 else n_chunks - 1 - n
            where.append((pl.ds(pl.multiple_of(s * seq + c * CHUNK, CHUNK), CHUNK), d, g))

        def load(ci, what):
            rows, d, g = where[ci]
            off = {"q": g * GW, "k": DN_W + g * GW, "v": 2 * DN_W + g * GW}
            if what in off:
                return x_ref[rows, pl.ds(off[what], GW)]
            return gate_s[rows, pl.ds((0 if what == "beta" else 2 * DN_W) + d * DN_W + g * GW, GW)]

        n_ch = len(chains)
        grams, decs = [], []
        for ci, (rows, d, g) in enumerate(where):
            incl = dirs[d][0]
            gc = load(ci, "gc")
            gc_col = jnp.sum(jnp.where(diag, gc, 0.0), axis=0, keepdims=True)
            decs.append(jnp.where(incl, jnp.exp(jnp.minimum(gc - gc_col, 0.0)), 0.0))
            k = load(ci, "k")
            lhs = jnp.concatenate([k * load(ci, "beta"), load(ci, "q")], axis=0).astype(BF16)
            grams.append(_bdot(lhs, bd(k), nt=True))
        lms = [jnp.where(dirs[d][1], gm[:CHUNK] * dec, 0.0) for (_, d, _), gm, dec in zip(where, grams, decs)]
        attn = [(gm[CHUNK:] * dec).astype(BF16) for gm, dec in zip(grams, decs)]
        ts = [t.astype(BF16) for t in _packed_unit_inverse(lms, eye, row, col, bd_mask)]
        us, ws = [], []
        for ci, t in enumerate(ts):
            beta = load(ci, "beta")
            us.append(_bdot(t, bd(load(ci, "v") * beta)))
            ws.append(_bdot(t, bd(load(ci, "k") * beta * jnp.exp(load(ci, "gc")))))
        states = [st_s[ci] for ci in range(n_ch)]
        ws_qs = [_bdot(jnp.concatenate([w, load(ci, "q") * jnp.exp(load(ci, "gc"))], axis=0).astype(BF16), bd(st))
                 for ci, (w, st) in enumerate(zip(ws, states))]
        g_lasts, lhs2 = [], []
        for ci, (rows, d, g) in enumerate(where):
            gc = load(ci, "gc")
            last_row = dirs[d][2]
            g_last = gc[last_row:last_row + 1, :]
            g_lasts.append(g_last)
            k_dec = load(ci, "k") * jnp.exp(g_last - gc)
            lhs2.append(jnp.concatenate([attn[ci], _heads_transposed(k_dec).astype(BF16)], axis=0))
        v_bds = [bd(u - x[:CHUNK]) for u, x in zip(us, ws_qs)]
        avs = [_bdot(l2, vb) for l2, vb in zip(lhs2, v_bds)]
        for ci, (rows, d, g) in enumerate(where):
            st_s[ci] = states[ci] * jnp.exp(g_lasts[ci]) + avs[ci][CHUNK:]
            o_s[d, rows, g * GW:(g + 1) * GW] = ws_qs[ci][CHUNK:] + avs[ci][:CHUNK]
        return carry

    lax.fori_loop(0, n_chunks, chunk_step, 0)

    for blk in range(n_seq * seq // rb):
        rows = slice(blk * rb, (blk + 1) * rb)
        o = o_s[0, rows, :] + o_s[1, rows, :]
        o = o * lax.rsqrt(_head_sumsq(o, DN_DV) * (1.0 / DN_DV) + EPS) * gn_ref[...]
        o_ref[rows, :] = (o * _silu(z_ref[rows, :])).astype(BF16)
    for ci, (s, d, g) in enumerate(chains):
        st = st_s[ci]
        for hb in range(HEADS_PER_GROUP):
            (sf_ref, sb_ref)[d][s, g * HEADS_PER_GROUP + hb] = st[:, hb * DN_DV:(hb + 1) * DN_DV]


def _delta(cqkv, dz, ba, lw, seq, n_seq, init=None):
    t = cqkv.shape[0]
    nb = t // seq
    rows = n_seq * seq
    has_init = init is not None
    seq_block = lambda w: pl.BlockSpec((rows, w), lambda b: (b, 0))
    state_block = pl.BlockSpec((n_seq, DN_DK, DN_W), lambda b: (b, 0, 0))
    in_specs = [seq_block(3 * DN_W), seq_block(DN_W), seq_block(LANES),
                _resident((1, LANES)), _resident((1, LANES)), _resident((1, DN_W))]
    args = [cqkv, dz, ba, lw["a_log"], lw["dt_bias"], lw["gn"]]
    if has_init:
        in_specs += [state_block, state_block]
        args += list(init)
    final_block = pl.BlockSpec((n_seq, DN_HEADS, DN_DK, DN_DV), lambda b: (b, 0, 0, 0))
    final_shape = jax.ShapeDtypeStruct((nb, DN_HEADS, DN_DK, DN_DV), F32)
    return pl.pallas_call(
        functools.partial(_delta_kernel, seq=seq, n_seq=n_seq, has_init=has_init),
        grid=(nb // n_seq,),
        in_specs=in_specs,
        out_specs=[seq_block(DN_W), final_block, final_block],
        out_shape=[jax.ShapeDtypeStruct((t, DN_W), BF16), final_shape, final_shape],
        scratch_shapes=[pltpu.VMEM((rows, 4 * DN_W), F32),
                        pltpu.VMEM((2, rows, DN_W), F32),
                        pltpu.VMEM((n_seq * 2 * N_GROUPS, DN_DK, GW), F32)],
        compiler_params=_params(),
        name="delta_rule_init" if has_init else "delta_rule",
    )(*args)


_POST_WEIGHTS = ("w_pa", "w_pb", "w_o", "w_gate", "w_up", "w_down")


def _round_weights_once(layer, srcs, dsts, outs, stage, in_sems, out_sems):
    chunks = []
    for k, dst in enumerate(dsts):
        rows, width = dst.shape
        chunks += [(k, r0, min(CAST_ROWS, rows - r0), width) for r0 in range(0, rows, CAST_ROWS)]

    def copy_in(j):
        k, r0, n, width = chunks[j]
        return pltpu.make_async_copy(srcs[k].at[layer, pl.ds(r0, n), :],
                                     stage.at[j % 2, pl.ds(0, n), pl.ds(0, width)], in_sems.at[j % 2])

    def copy_out(k):
        return pltpu.make_async_copy(dsts[k], outs[k], out_sems.at[k])

    copy_in(0).start()
    for j, (k, r0, n, width) in enumerate(chunks):
        if j + 1 < len(chunks):
            copy_in(j + 1).start()
        copy_in(j).wait()
        dsts[k][r0:r0 + n, :] = stage[j % 2, 0:n, 0:width].astype(BF16)
        if j + 1 == len(chunks) or chunks[j + 1][0] != k:
            copy_out(k).start()
    for k in range(len(dsts)):
        copy_out(k).wait()


def _post_kernel(*refs, round_weights, layer):
    x_ref, oa_ref, od_ref, gab_ref, mod_ref, g2_ref = refs[:6]
    n_w = len(_POST_WEIGHTS)
    if round_weights:
        srcs = refs[6:6 + n_w]
        y_ref = refs[6 + n_w]
        outs = refs[7 + n_w:7 + 2 * n_w]
        weights = refs[7 + 2 * n_w:7 + 3 * n_w]
        stage, in_sems, out_sems = refs[7 + 3 * n_w:]

        @pl.when(pl.program_id(0) == 0)
        def _():
            _round_weights_once(layer, srcs, weights, outs, stage, in_sems, out_sems)
    else:
        weights = refs[6:6 + n_w]
        y_ref = refs[6 + n_w]
    wpa_ref, wpb_ref, wo_ref, wg_ref, wu_ref, wd_ref = weights
    _, _, gate1 = _mod_parts(mod_ref, True)
    shift2, scale2, gate2 = _mod_parts(mod_ref, False)
    gab = gab_ref[...].astype(F32)
    merged = (_sigmoid(gab[:, :D_MODEL]) * _dot(oa_ref[...], wpa_ref[...])
              + _sigmoid(gab[:, D_MODEL:]) * _dot(od_ref[...], wpb_ref[...]))
    x1 = x_ref[...] + gate1 * _dot(merged.astype(BF16), wo_ref[...])
    h2 = _rms_mod(x1, g2_ref[...], shift2, scale2).astype(BF16)
    act = _silu(_dot(h2, wg_ref[...])) * _dot(h2, wu_ref[...])
    y_ref[...] = x1 + gate2 * _dot(act.astype(BF16), wd_ref[...])


def _post(x, oa, od, gab, mod, group_of_tile, lw, tm, weights, layer=None):
    t = x.shape[0]
    round_weights = layer is not None
    row = lambda w: pl.BlockSpec((tm, w), lambda i: (i, 0))
    in_specs = [row(D_MODEL), row(ATTN_Q_W), row(DN_W), row(2 * D_MODEL),
                pl.BlockSpec((None, 1, 6 * D_MODEL), lambda i: (group_of_tile(i), 0, 0)),
                _resident((1, D_MODEL))]
    out_specs = [row(D_MODEL)]
    out_shape = [jax.ShapeDtypeStruct((t, D_MODEL), F32)]
    scratch = []
    if round_weights:
        shapes = [w.shape[1:] for w in weights]
        in_specs += [pl.BlockSpec(memory_space=pl.ANY)] * len(weights)
        out_specs += [pl.BlockSpec(memory_space=pl.ANY)] * len(weights)
        out_shape += [jax.ShapeDtypeStruct(sh, BF16) for sh in shapes]
        scratch = ([pltpu.VMEM(sh, BF16) for sh in shapes]
                   + [pltpu.VMEM((2, CAST_ROWS, max(sh[1] for sh in shapes)), F32),
                      pltpu.SemaphoreType.DMA((2,)), pltpu.SemaphoreType.DMA((len(weights),))])
    else:
        in_specs += [_resident(w.shape) for w in weights]
    res = pl.pallas_call(
        functools.partial(_post_kernel, round_weights=round_weights, layer=layer),
        grid=(t // tm,),
        in_specs=in_specs,
        out_specs=out_specs,
        out_shape=out_shape,
        scratch_shapes=scratch,
        compiler_params=_params(),
        name="post_block_rounding" if round_weights else "post_block",
    )(x, oa, od, gab, mod, lw["norm2_g"], *weights)
    return res[0], list(res[1:])


def _rope_tables(n_tokens):
    quarter = HEAD_DIM // 4
    lane = jnp.arange(LANES)
    d = lane % HEAD_DIM
    inv = ROPE_THETA ** (-(d % quarter).astype(F32) / quarter)
    t = jnp.arange(n_tokens)
    pos = jnp.where(d[None, :] < HEAD_DIM // 2, (t // GRID_W)[:, None], (t % GRID_W)[:, None]).astype(F32)
    ang = pos * inv[None, :]
    sign = jnp.where((d % (2 * quarter)) < quarter, -1.0, 1.0)
    return jnp.cos(ang), jnp.sin(ang) * sign[None, :]


def _pack_states(s):
    b = s.shape[0]
    return s.transpose(0, 2, 1, 3).reshape(b, DN_DK, DN_W)


def _layer_weights(l, w_in, norm1_g, q_norm_g, k_norm_g, a_log, dt_bias, dn_norm_g, norm2_g):
    pad_small = lambda a: jnp.pad(a.reshape(1, -1), ((0, 0), (0, LANES - a.size)))
    return dict(
        w_in=w_in,
        w_gab=w_in[l, :, _GATE_OFF:],
        norm1_g=norm1_g[l].reshape(1, D_MODEL),
        norm2_g=norm2_g[l].reshape(1, D_MODEL),
        gq=jnp.tile(q_norm_g[l], N_Q_HEADS).reshape(1, ATTN_Q_W),
        gk=jnp.tile(k_norm_g[l], N_KV_HEADS).reshape(1, ATTN_KV_W),
        a_log=pad_small(jnp.concatenate([jnp.zeros((2 * DN_HEADS,), F32), a_log[l].reshape(-1)])),
        dt_bias=pad_small(jnp.concatenate([jnp.zeros((2 * DN_HEADS,), F32), dt_bias[l].reshape(-1)])),
        gn=jnp.tile(dn_norm_g[l], DN_HEADS).reshape(1, DN_W),
    )


def kernel(x_prompt, x_sample, cache_k, cache_v, state_fwd, state_bwd, c, c_ctx, w_ada, b_ada, norm1_g, w_in,
           q_norm_g, k_norm_g, conv_w, a_log, dt_bias, dn_norm_g, w_pa, w_pb, w_o, norm2_g, w_gate, w_up, w_down):
    batch, seq, _ = x_prompt.shape
    dec_batch, dec_seq, _ = x_sample.shape
    depth = w_in.shape[0]
    past = cache_k.shape[2]
    assert dec_batch + 1 <= SUBLANES and seq % MXU_DIM == 0 and dec_seq % MXU_DIM == 0
    assert batch % SEQS_PER_STEP == 0 and dec_batch % SEQS_PER_STEP == 0 and batch % ATTN_SEQS_PER_STEP == 0

    cvecs = jnp.zeros((SUBLANES, D_MODEL), F32).at[0].set(c_ctx).at[1:1 + dec_batch].set(c)
    rope_tables = _rope_tables(dec_seq)
    tm_in = IN_PROJ_TILE
    tm_post = POST_TILE
    ctx_group = lambda i: 0

    def lat_group(tm):
        return lambda i: 1 + i // (dec_seq // tm)

    yp = x_prompt.reshape(batch * seq, D_MODEL)
    ys = x_sample.reshape(dec_batch * dec_seq, D_MODEL)
    ks_out, vs_out, sf_out, sb_out = [], [], [], []
    w_in = w_in.astype(BF16)
    for l in range(depth):
        lw = _layer_weights(l, w_in, norm1_g, q_norm_g, k_norm_g, a_log, dt_bias, dn_norm_g, norm2_g)
        mod = _modulation(cvecs, w_ada[l], b_ada[l])[:1 + dec_batch].reshape(1 + dec_batch, 1, 6 * D_MODEL)

        q, kt, vt, cqkv, dz, ba, gab = _in_proj(yp, mod, ctx_group, lw, conv_w, l, tm_in, seq)
        oa = _attention(q, [(kt, vt, seq)], seq, seq, ATTN_SEQS_PER_STEP)
        od, sf, sb = _delta(cqkv, dz, ba, lw, seq, SEQS_PER_STEP)
        yp, post_w = _post(yp, oa, od, gab, mod, ctx_group, lw, tm_post,
                           [w_pa, w_pb, w_o, w_gate, w_up, w_down], l)
        ks_out.append(kt.transpose(0, 3, 1, 2))
        vs_out.append(vt.transpose(0, 3, 1, 2))
        sf_out.append(sf)
        sb_out.append(sb)

        q, kr, v, cqkv, dz, ba, gab = _in_proj(ys, mod, lat_group(tm_in), lw, conv_w, l, tm_in, dec_seq,
                                               rope_tables)
        ck = cache_k[:, l].reshape(dec_batch * past, ATTN_KV_W)
        cv = cache_v[:, l].reshape(dec_batch * past, ATTN_KV_W)
        oa = _attention(q, [(ck, cv, past), (kr, v, dec_seq)], dec_seq, ATTN_Q_TILE)
        init = (_pack_states(state_fwd[:, l]), _pack_states(state_bwd[:, l]))
        od, _, _ = _delta(cqkv, dz, ba, lw, dec_seq, SEQS_PER_STEP, init)
        ys, _ = _post(ys, oa, od, gab, mod, lat_group(tm_post), lw, tm_post, post_w)

    return (yp.reshape(batch, seq, D_MODEL), ys.reshape(dec_batch, dec_seq, D_MODEL),
            jnp.stack(ks_out, axis=1), jnp.stack(vs_out, axis=1),
            jnp.stack(sf_out, axis=1), jnp.stack(sb_out, axis=1))
```

```python
import functools
import math

import jax
import jax.numpy as jnp
from jax import lax
from jax.experimental import pallas as pl
from jax.experimental.pallas import tpu as pltpu

D_MODEL = 1024
GRID_W = 64
HEAD_DIM = 64
N_Q_HEADS = 8
N_KV_HEADS = 2
Q_GROUP = N_Q_HEADS // N_KV_HEADS
ATTN_Q_W = N_Q_HEADS * HEAD_DIM
ATTN_KV_W = N_KV_HEADS * HEAD_DIM
ROPE_THETA = 10000.0
DN_HEADS = 8
DN_DK = 64
DN_DV = 64
DN_W = DN_HEADS * DN_DK
DN_CONV = 5
CHUNK = 64
EPS = 1e-6
LOG2_E = math.log2(math.e)

LANES = 128
SUBLANES = 8
MXU_DIM = 256
VMEM_LIMIT_BYTES = 56 * 1024 * 1024

HEADS_PER_GROUP = MXU_DIM // DN_DK
N_GROUPS = DN_HEADS // HEADS_PER_GROUP
GW = HEADS_PER_GROUP * DN_DK
INVERSE_BASE = 8

IN_PROJ_TILE = 512
POST_TILE = 256
ATTN_Q_TILE = 256
SEQS_PER_STEP = 2
ATTN_SEQS_PER_STEP = 4
ADALN_TILE = 2048
CAST_ROWS = 256

_QKV_W = ATTN_Q_W + 2 * ATTN_KV_W
_DN_OFF = _QKV_W
_BA_OFF = _DN_OFF + 4 * DN_W
_GATE_OFF = _BA_OFF + 4 * DN_HEADS
_BA_END = _BA_OFF + LANES

F32 = jnp.float32
BF16 = jnp.bfloat16


def _dot(a, b):
    return jnp.dot(a, b, preferred_element_type=F32)


def _dot_nt(a, b):
    return lax.dot_general(a, b, (((1,), (1,)), ((), ())), preferred_element_type=F32)


def _split2(x):
    hi = x.astype(BF16)
    lo = (x - hi.astype(F32)).astype(BF16)
    return hi, lo


def _split3(x):
    hi = x.astype(BF16)
    r = x - hi.astype(F32)
    mid = r.astype(BF16)
    lo = (r - mid.astype(F32)).astype(BF16)
    return hi, mid, lo


def _dot_exact_lhs(a01, b):
    n = b.shape[1]
    b1, b2, b3 = _split3(b)
    r = _dot(a01, jnp.concatenate([b1, b2, b3], axis=1))
    return r[:, :n] + r[:, n:2 * n] + r[:, 2 * n:]


def _iota(shape, dim):
    return lax.broadcasted_iota(jnp.int32, shape, dim)


def _same_block(shape, width):
    return (_iota(shape, 0) // width) == (_iota(shape, 1) // width)


def _head_sumsq(x, width):
    m, n = x.shape
    slab = min(n, MXU_DIM)
    sel = _same_block((slab, slab), width).astype(BF16)
    outs = []
    for s in range(n // slab):
        xs = x[:, s * slab:(s + 1) * slab]
        outs.append(_dot((xs * xs).astype(BF16), sel))
    return outs[0] if len(outs) == 1 else jnp.concatenate(outs, axis=1)


def _sigmoid(x):
    return 1.0 / (1.0 + jnp.exp(-x))


def _silu(x):
    return x * _sigmoid(x)


def _softplus(x):
    return jnp.maximum(x, 0.0) + jnp.log(1.0 + jnp.exp(-jnp.abs(x)))


def _resident(shape):
    nd = len(shape)
    return pl.BlockSpec(shape, lambda *_: (0,) * nd, pipeline_mode=pl.Buffered(1))


def _params(n_axes=1):
    return pltpu.CompilerParams(dimension_semantics=("arbitrary",) * n_axes,
                                vmem_limit_bytes=VMEM_LIMIT_BYTES)


def _mod_kernel(c_ref, w_ref, b_ref, o_ref):
    m = c_ref.shape[0]
    hi, lo = _split2(_silu(c_ref[...]))
    r = _dot(jnp.concatenate([hi, lo], axis=0), w_ref[...].astype(BF16))
    o_ref[...] = r[:m] + r[m:] + b_ref[...]


def _modulation(cvecs, w_ada, b_ada):
    n = w_ada.shape[1]
    tn = ADALN_TILE
    return pl.pallas_call(
        _mod_kernel,
        grid=(n // tn,),
        in_specs=[pl.BlockSpec((SUBLANES, D_MODEL), lambda j: (0, 0)),
                  pl.BlockSpec((D_MODEL, tn), lambda j: (0, j)),
                  pl.BlockSpec((1, tn), lambda j: (0, j))],
        out_specs=pl.BlockSpec((SUBLANES, tn), lambda j: (0, j)),
        out_shape=jax.ShapeDtypeStruct((SUBLANES, n), F32),
        compiler_params=_params(),
        name="adaln_modulation",
    )(cvecs, w_ada, b_ada.reshape(1, n))


def _mod_parts(mod_ref, first):
    m = mod_ref[...]
    base = 0 if first else 3 * D_MODEL
    return (m[:, base:base + D_MODEL], m[:, base + D_MODEL:base + 2 * D_MODEL],
            m[:, base + 2 * D_MODEL:base + 3 * D_MODEL])


def _rms_mod(x, g, shift, scale):
    ms = jnp.mean(x * x, axis=-1, keepdims=True)
    return (x * lax.rsqrt(ms + EPS) * g) * (1.0 + scale) + shift


def _rope(x, cos, sin_signed):
    outs = []
    half = HEAD_DIM // 4
    first_half = (_iota((1, LANES), 1) % (2 * half)) < half
    for s in range(x.shape[1] // LANES):
        xs = x[:, s * LANES:(s + 1) * LANES]
        partner = jnp.where(first_half, pltpu.roll(xs, LANES - half, axis=1), pltpu.roll(xs, half, axis=1))
        outs.append(xs * cos + partner * sin_signed)
    return outs[0] if len(outs) == 1 else jnp.concatenate(outs, axis=1)


def _conv_silu(prev, x, nxt, taps):
    rows = x.shape[0]
    xe = jnp.concatenate([prev, x, nxt], axis=0)
    ne = rows + 2 * SUBLANES
    half = (DN_CONV - 1) // 2
    y = jnp.zeros(x.shape, F32)
    for tap in range(DN_CONV):
        d = tap - half
        sh = xe if d == 0 else pltpu.roll(xe, (ne - d) % ne, axis=0)
        y = y + sh[SUBLANES:SUBLANES + rows] * taps[tap:tap + 1]
    return _silu(y)


def _in_proj_kernel(*refs, rope, halo, seq):
    refs = list(refs)
    x_ref = refs.pop(0)
    xp_ref, xn_ref = (refs.pop(0), refs.pop(0)) if halo else (None, None)
    mod_ref, g_ref, w_ref, wgab_ref, gq_ref, gk_ref, cw_ref = (refs.pop(0) for _ in range(7))
    cos_ref, sin_ref = (refs.pop(0), refs.pop(0)) if rope else (None, None)
    q_ref, k_ref, v_ref, cqkv_ref, dz_ref, ba_ref, gab_ref = refs
    tm = x_ref.shape[0]
    shift, scale, _ = _mod_parts(mod_ref, True)
    h = _rms_mod(x_ref[...], g_ref[...], shift, scale).astype(BF16)

    qkv = _dot(h, w_ref[:, 0:_QKV_W])
    aq = qkv[:, :ATTN_Q_W]
    ak = qkv[:, ATTN_Q_W:ATTN_Q_W + ATTN_KV_W]
    av = qkv[:, ATTN_Q_W + ATTN_KV_W:]
    qn = aq * lax.rsqrt(_head_sumsq(aq, HEAD_DIM) * (1.0 / HEAD_DIM) + EPS) * gq_ref[...]
    kn = ak * lax.rsqrt(_head_sumsq(ak, HEAD_DIM) * (1.0 / HEAD_DIM) + EPS) * gk_ref[...]
    if rope:
        cos, sin = cos_ref[...], sin_ref[...]
        qn = _rope(qn, cos, sin)
        k_ref[...] = _rope(kn, cos, sin)
        v_ref[...] = av
    else:
        kt = kn.T
        vt = av.T
        for s in range(tm // seq):
            for hd in range(N_KV_HEADS):
                k_ref[s, hd] = kt[hd * HEAD_DIM:(hd + 1) * HEAD_DIM, s * seq:(s + 1) * seq]
                v_ref[s, hd] = vt[hd * HEAD_DIM:(hd + 1) * HEAD_DIM, s * seq:(s + 1) * seq]
    q_ref[...] = (qn * (HEAD_DIM ** -0.5 * LOG2_E)).astype(BF16)

    dn = _dot(h, w_ref[:, _DN_OFF:_BA_OFF])
    dz_ref[...] = dn[:, 3 * DN_W:]
    ba_ref[...] = _dot(h, w_ref[:, _BA_OFF:_BA_END])
    gab_ref[...] = _dot(h, wgab_ref[...]).astype(BF16)

    zeros = jnp.zeros((SUBLANES, 3 * DN_W), F32)
    if halo:
        tiles_per_seq = seq // tm
        pos = pl.program_id(0) % tiles_per_seq
        xh = jnp.concatenate([xp_ref[...], xn_ref[...]], axis=0)
        hh = _rms_mod(xh, g_ref[...], shift, scale).astype(BF16)
        dh = _dot(hh, w_ref[:, _DN_OFF:_DN_OFF + 3 * DN_W])
        edges = [(jnp.where(pos > 0, dh[:SUBLANES], 0.0), jnp.where(pos < tiles_per_seq - 1, dh[SUBLANES:], 0.0))]
        sub = tm
    else:
        sub = seq
        edges = [(zeros, zeros)] * (tm // seq)
    for s, (prev, nxt) in enumerate(edges):
        rows = slice(s * sub, (s + 1) * sub)
        for part in range(3):
            cols = slice(part * DN_W, (part + 1) * DN_W)
            y = _conv_silu(prev[:, cols], dn[rows, cols], nxt[:, cols], cw_ref[:, cols])
            if part == 0:
                y = y * lax.rsqrt(_head_sumsq(y, DN_DK) + EPS) * (DN_DK ** -0.5)
            elif part == 1:
                y = y * lax.rsqrt(_head_sumsq(y, DN_DK) + EPS)
            cqkv_ref[rows, cols] = y


def _in_proj(x, mod, group_of_tile, lw, conv_w, layer, tm, seq, rope_tables=None):
    t = x.shape[0]
    rope = rope_tables is not None
    halo = tm < seq
    assert tm % seq == 0 or seq % tm == 0
    row = lambda w: pl.BlockSpec((tm, w), lambda i: (i, 0))
    in_specs = [row(D_MODEL)]
    args = [x]
    if halo:
        blocks_per_tile = tm // SUBLANES
        last_block = t // SUBLANES - 1
        in_specs += [pl.BlockSpec((SUBLANES, D_MODEL), lambda i: (jnp.maximum(i * blocks_per_tile - 1, 0), 0)),
                     pl.BlockSpec((SUBLANES, D_MODEL),
                                  lambda i: (jnp.minimum((i + 1) * blocks_per_tile, last_block), 0))]
        args += [x, x]
    in_specs += [pl.BlockSpec((None, 1, 6 * D_MODEL), lambda i: (group_of_tile(i), 0, 0)),
                 _resident((1, D_MODEL)),
                 pl.BlockSpec((None, D_MODEL, _BA_END), lambda i: (layer, 0, 0), pipeline_mode=pl.Buffered(1)),
                 _resident(lw["w_gab"].shape),
                 _resident((1, ATTN_Q_W)), _resident((1, ATTN_KV_W)),
                 pl.BlockSpec((None, DN_CONV, 3 * DN_W), lambda i: (layer, 0, 0), pipeline_mode=pl.Buffered(1))]
    args += [mod, lw["norm1_g"], lw["w_in"], lw["w_gab"], lw["gq"], lw["gk"], conv_w]
    if rope:
        cos, sin = rope_tables
        tiles_per_seq = cos.shape[0] // tm
        in_specs += [pl.BlockSpec((tm, LANES), lambda i: (i % tiles_per_seq, 0))] * 2
        args += [cos, sin]
        kv_spec = row(ATTN_KV_W)
        kv_shape = jax.ShapeDtypeStruct((t, ATTN_KV_W), F32)
    else:
        kv_spec = pl.BlockSpec((tm // seq, N_KV_HEADS, HEAD_DIM, seq), lambda i: (i, 0, 0, 0))
        kv_shape = jax.ShapeDtypeStruct((t // seq, N_KV_HEADS, HEAD_DIM, seq), F32)
    widths = (3 * DN_W, DN_W, LANES)
    return pl.pallas_call(
        functools.partial(_in_proj_kernel, rope=rope, halo=halo, seq=seq),
        grid=(t // tm,),
        in_specs=in_specs,
        out_specs=[row(ATTN_Q_W), kv_spec, kv_spec] + [row(w) for w in widths] + [row(2 * D_MODEL)],
        out_shape=[jax.ShapeDtypeStruct((t, ATTN_Q_W), BF16), kv_shape, kv_shape]
                  + [jax.ShapeDtypeStruct((t, w), F32) for w in widths]
                  + [jax.ShapeDtypeStruct((t, 2 * D_MODEL), BF16)],
        compiler_params=_params(),
        name="in_proj_rope" if rope else "in_proj",
    )(*args)


def _attn_kernel(*refs, transposed, n_seq):
    n_parts = len(transposed)
    q_ref = refs[0]
    kv_refs = refs[1:1 + 2 * n_parts]
    o_ref = refs[1 + 2 * n_parts]
    tq = q_ref.shape[0] // n_seq
    qt = q_ref[...].astype(F32).T.astype(BF16)
    chains = [(s, kvh) for s in range(n_seq) for kvh in range(N_KV_HEADS)]
    qgs = [jnp.concatenate([qt[j * HEAD_DIM:(j + 1) * HEAD_DIM, s * tq:(s + 1) * tq]
                            for j in range(kvh * Q_GROUP, (kvh + 1) * Q_GROUP)], axis=1) for s, kvh in chains]

    def keys(p, s, kvh):
        k_ref = kv_refs[2 * p]
        k = k_ref[s, kvh].T if transposed[p] else k_ref[:, kvh * HEAD_DIM:(kvh + 1) * HEAD_DIM]
        return k.astype(BF16)

    def values_t(p, s, kvh):
        v_ref = kv_refs[2 * p + 1]
        v = v_ref[s, kvh] if transposed[p] else v_ref[:, kvh * HEAD_DIM:(kvh + 1) * HEAD_DIM].T
        return v.astype(BF16)

    ss = [[_dot(keys(p, s, kvh), qg) for p in range(n_parts)] for (s, kvh), qg in zip(chains, qgs)]
    ms = [functools.reduce(jnp.maximum, [jnp.max(sc, axis=0, keepdims=True) for sc in sp]) for sp in ss]
    ps = [[jnp.exp2(sc - m) for sc in sp] for sp, m in zip(ss, ms)]
    dens = [functools.reduce(jnp.add, [jnp.sum(p, axis=0, keepdims=True) for p in pp]) for pp in ps]
    accs = [functools.reduce(jnp.add, [_dot(values_t(p, s, kvh), pr.astype(BF16)) for p, pr in enumerate(pp)])
            for (s, kvh), pp in zip(chains, ps)]
    for s in range(n_seq):
        outs = []
        for kvh in range(N_KV_HEADS):
            o = accs[s * N_KV_HEADS + kvh] / dens[s * N_KV_HEADS + kvh]
            outs += [o[:, g * tq:(g + 1) * tq] for g in range(Q_GROUP)]
        o_ref[s * tq:(s + 1) * tq, :] = jnp.concatenate(outs, axis=0).T.astype(BF16)


def _attention(q, parts, seq_q, tq, n_seq=1):
    t = q.shape[0]
    nq = seq_q // tq
    assert n_seq == 1 or (nq == 1 and all(k.ndim == 4 for k, _, _ in parts))
    in_specs = [pl.BlockSpec((n_seq * tq, ATTN_Q_W), lambda b, i: (b * nq + i, 0))]
    args = [q]
    transposed = []
    for k, v, seq_k in parts:
        transposed.append(k.ndim == 4)
        if k.ndim == 4:
            in_specs += [pl.BlockSpec((n_seq, N_KV_HEADS, HEAD_DIM, seq_k), lambda b, i: (b, 0, 0, 0))] * 2
        else:
            in_specs += [pl.BlockSpec((seq_k, ATTN_KV_W), lambda b, i: (b, 0))] * 2
        args += [k, v]
    return pl.pallas_call(
        functools.partial(_attn_kernel, transposed=tuple(transposed), n_seq=n_seq),
        grid=(t // (seq_q * n_seq), nq),
        in_specs=in_specs,
        out_specs=pl.BlockSpec((n_seq * tq, ATTN_Q_W), lambda b, i: (b * nq + i, 0)),
        out_shape=jax.ShapeDtypeStruct((t, ATTN_Q_W), BF16),
        compiler_params=_params(2),
        name="attention_%dparts" % len(parts),
    )(*args)


def _block_diag(x, mask):
    reps = LANES // DN_DK
    return [jnp.where(mask, jnp.concatenate([x[:, c * LANES:(c + 1) * LANES]] * reps, axis=0),
                      jnp.zeros((), x.dtype)) for c in range(x.shape[1] // LANES)]


def _bdot(x, tiles, nt=False):
    op = _dot_nt if nt else _dot
    return jnp.concatenate([op(x[:, c * LANES:(c + 1) * LANES], t) for c, t in enumerate(tiles)], axis=1)


def _heads_transposed(x):
    xt = x.T
    return jnp.concatenate([xt[hb * DN_DK:(hb + 1) * DN_DK] for hb in range(HEADS_PER_GROUP)], axis=1)


def _packed_unit_inverse(lms, eye, row, col, mask):
    def mm1(x, y):
        return _bdot(x, _block_diag(y, mask))

    b = INVERSE_BASE
    ns = [-jnp.where((row // b) == (col // b), lm, 0.0) for lm in lms]
    ts = [eye + n for n in ns]
    ps = [mm1(nb, nb) for nb in (n.astype(BF16) for n in ns)]
    steps = int(math.log2(b)) - 1
    for s in range(steps):
        last = s == steps - 1
        pbs = [p.astype(BF16) for p in ps]
        prods = [mm1(t.astype(BF16) if last else jnp.concatenate([t.astype(BF16), pb], axis=0), pb)
                 for t, pb in zip(ts, pbs)]
        ts = [t + prod[:CHUNK] for t, prod in zip(ts, prods)]
        if not last:
            ps = [prod[CHUNK:] for prod in prods]
    while b < CHUNK:
        between = ((row // (2 * b)) == (col // (2 * b))) & ((row // b) != (col // b))
        tbs = [t.astype(BF16) for t in ts]
        ys = [mm1(tb, jnp.where(between, lm, 0.0).astype(BF16)) for lm, tb in zip(lms, tbs)]
        ts = [t - mm1(y.astype(BF16), tb) for t, tb, y in zip(ts, tbs, ys)]
        b *= 2
    return ts


def _delta_kernel(*refs, seq, n_seq, has_init):
    if has_init:
        (x_ref, z_ref, ba_ref, alog_ref, dtb_ref, gn_ref, s0f_ref, s0b_ref,
         o_ref, sf_ref, sb_ref, gate_s, o_s, st_s) = refs
    else:
        (x_ref, z_ref, ba_ref, alog_ref, dtb_ref, gn_ref,
         o_ref, sf_ref, sb_ref, gate_s, o_s, st_s) = refs
    n_chunks = seq // CHUNK
    rb = MXU_DIM
    n_gate = 4 * DN_HEADS

    exp_r = _iota((LANES, 4 * DN_W), 0)
    expand = ((exp_r < 3 * n_gate) & ((_iota((LANES, 4 * DN_W), 1) // DN_DK) == exp_r % n_gate)).astype(BF16)
    lane = _iota((1, LANES), 1)
    blk_r = _iota((rb, rb), 0)
    blk_c = _iota((rb, rb), 1)
    same_chunk = (blk_r // CHUNK) == (blk_c // CHUNK)
    cum_f = (same_chunk & (blk_c <= blk_r)).astype(BF16)
    cum_b = (same_chunk & (blk_c >= blk_r)).astype(BF16)
    for blk in range(n_seq * seq // rb):
        rows = slice(blk * rb, (blk + 1) * rb)
        ba = ba_ref[rows, :]
        decay = -jnp.exp(alog_ref[...]) * _softplus(ba + dtb_ref[...])
        vals = jnp.where(lane < 2 * DN_HEADS, _sigmoid(ba), jnp.where(lane < n_gate, decay, 0.0))
        narrow = jnp.where(lane < 2 * DN_HEADS, vals,
                           jnp.where(lane < 3 * DN_HEADS, _dot_exact_lhs(cum_f, vals),
                                     jnp.where(lane < n_gate, _dot_exact_lhs(cum_b, vals), 0.0)))
        t1, t2, t3 = (t.astype(F32) for t in _split3(narrow))
        stacked = t1 + pltpu.roll(t2, n_gate, axis=1) + pltpu.roll(t3, 2 * n_gate, axis=1)
        gate_s[rows, :] = _dot(stacked.astype(BF16), expand)

    bd_mask = _same_block((LANES, LANES), DN_DK)
    row = _iota((CHUNK, GW), 0)
    col = _iota((CHUNK, GW), 1) % CHUNK
    diag = row == col
    eye = diag.astype(F32)
    dirs = ((col <= row, col < row, CHUNK - 1), (col >= row, col > row, 0))
    chains = [(s, d, g) for s in range(n_seq) for d in range(2) for g in range(N_GROUPS)]
    for ci, (s, d, g) in enumerate(chains):
        if has_init:
            st_s[ci] = (s0f_ref, s0b_ref)[d][s, :, g * GW:(g + 1) * GW]
        else:
            st_s[ci] = jnp.zeros((DN_DK, GW), F32)

    def bd(x):
        return _block_diag(x.astype(BF16), bd_mask)

    def chunk_step(n, carry):
        where = []
        for s, d, g in chains:
            c = n if d == 0 else n_chunks - 1 - n
            where.append((pl.ds(pl.multiple_of(s * seq + c * CHUNK, CHUNK), CHUNK), d, g))

        def load(ci, what):
            rows, d, g = where[ci]
            off = {"q": g * GW, "k": DN_W + g * GW, "v": 2 * DN_W + g * GW}
            if what in off:
                return x_ref[rows, pl.ds(off[what], GW)]
            return gate_s[rows, pl.ds((0 if what == "beta" else 2 * DN_W) + d * DN_W + g * GW, GW)]

        n_ch = len(chains)
        grams, decs = [], []
        for ci, (rows, d, g) in enumerate(where):
            incl = dirs[d][0]
            gc = load(ci, "gc")
            gc_col = jnp.sum(jnp.where(diag, gc, 0.0), axis=0, keepdims=True)
            decs.append(jnp.where(incl, jnp.exp(jnp.minimum(gc - gc_col, 0.0)), 0.0))
            k = load(ci, "k")
            lhs = jnp.concatenate([k * load(ci, "beta"), load(ci, "q")], axis=0).astype(BF16)
            grams.append(_bdot(lhs, bd(k), nt=True))
        lms = [jnp.where(dirs[d][1], gm[:CHUNK] * dec, 0.0) for (_, d, _), gm, dec in zip(where, grams, decs)]
        attn = [(gm[CHUNK:] * dec).astype(BF16) for gm, dec in zip(grams, decs)]
        ts = [t.astype(BF16) for t in _packed_unit_inverse(lms, eye, row, col, bd_mask)]
        us, ws = [], []
        for ci, t in enumerate(ts):
            beta = load(ci, "beta")
            us.append(_bdot(t, bd(load(ci, "v") * beta)))
            ws.append(_bdot(t, bd(load(ci, "k") * beta * jnp.exp(load(ci, "gc")))))
        states = [st_s[ci] for ci in range(n_ch)]
        ws_qs = [_bdot(jnp.concatenate([w, load(ci, "q") * jnp.exp(load(ci, "gc"))], axis=0).astype(BF16), bd(st))
                 for ci, (w, st) in enumerate(zip(ws, states))]
        g_lasts, lhs2 = [], []
        for ci, (rows, d, g) in enumerate(where):
            gc = load(ci, "gc")
            last_row = dirs[d][2]
            g_last = gc[last_row:last_row + 1, :]
            g_lasts.append(g_last)
            k_dec = load(ci, "k") * jnp.exp(g_last - gc)
            lhs2.append(jnp.concatenate([attn[ci], _heads_transposed(k_dec).astype(BF16)], axis=0))
        v_bds = [bd(u - x[:CHUNK]) for u, x in zip(us, ws_qs)]
        avs = [_bdot(l2, vb) for l2, vb in zip(lhs2, v_bds)]
        for ci, (rows, d, g) in enumerate(where):
            st_s[ci] = states[ci] * jnp.exp(g_lasts[ci]) + avs[ci][CHUNK:]
            o_s[d, rows, g * GW:(g + 1) * GW] = ws_qs[ci][CHUNK:] + avs[ci][:CHUNK]
        return carry

    lax.fori_loop(0, n_chunks, chunk_step, 0)

    for blk in range(n_seq * seq // rb):
        rows = slice(blk * rb, (blk + 1) * rb)
        o = o_s[0, rows, :] + o_s[1, rows, :]
        o = o * lax.rsqrt(_head_sumsq(o, DN_DV) * (1.0 / DN_DV) + EPS) * gn_ref[...]
        o_ref[rows, :] = (o * _silu(z_ref[rows, :])).astype(BF16)
    for ci, (s, d, g) in enumerate(chains):
        st = st_s[ci]
        for hb in range(HEADS_PER_GROUP):
            (sf_ref, sb_ref)[d][s, g * HEADS_PER_GROUP + hb] = st[:, hb * DN_DV:(hb + 1) * DN_DV]


def _delta(cqkv, dz, ba, lw, seq, n_seq, init=None):
    t = cqkv.shape[0]
    nb = t // seq
    rows = n_seq * seq
    has_init = init is not None
    seq_block = lambda w: pl.BlockSpec((rows, w), lambda b: (b, 0))
    state_block = pl.BlockSpec((n_seq, DN_DK, DN_W), lambda b: (b, 0, 0))
    in_specs = [seq_block(3 * DN_W), seq_block(DN_W), seq_block(LANES),
                _resident((1, LANES)), _resident((1, LANES)), _resident((1, DN_W))]
    args = [cqkv, dz, ba, lw["a_log"], lw["dt_bias"], lw["gn"]]
    if has_init:
        in_specs += [state_block, state_block]
        args += list(init)
    final_block = pl.BlockSpec((n_seq, DN_HEADS, DN_DK, DN_DV), lambda b: (b, 0, 0, 0))
    final_shape = jax.ShapeDtypeStruct((nb, DN_HEADS, DN_DK, DN_DV), F32)
    return pl.pallas_call(
        functools.partial(_delta_kernel, seq=seq, n_seq=n_seq, has_init=has_init),
        grid=(nb // n_seq,),
        in_specs=in_specs,
        out_specs=[seq_block(DN_W), final_block, final_block],
        out_shape=[jax.ShapeDtypeStruct((t, DN_W), BF16), final_shape, final_shape],
        scratch_shapes=[pltpu.VMEM((rows, 4 * DN_W), F32),
                        pltpu.VMEM((2, rows, DN_W), F32),
                        pltpu.VMEM((n_seq * 2 * N_GROUPS, DN_DK, GW), F32)],
        compiler_params=_params(),
        name="delta_rule_init" if has_init else "delta_rule",
    )(*args)


def _round_weights_once(layer, srcs, dsts, stage, sems):
    chunks = []
    for k, dst in enumerate(dsts):
        rows, width = dst.shape
        chunks += [(k, r0, min(CAST_ROWS, rows - r0), width) for r0 in range(0, rows, CAST_ROWS)]

    def copy_in(j):
        k, r0, n, width = chunks[j]
        return pltpu.make_async_copy(srcs[k].at[layer, pl.ds(r0, n), :],
                                     stage.at[j % 2, pl.ds(0, n), pl.ds(0, width)], sems.at[j % 2])

    copy_in(0).start()
    for j, (k, r0, n, width) in enumerate(chunks):
        if j + 1 < len(chunks):
            copy_in(j + 1).start()
        copy_in(j).wait()
        dsts[k][r0:r0 + n, :] = stage[j % 2, 0:n, 0:width].astype(BF16)


def _post_kernel(*refs, layer, n_ctx_tiles):
    (xp_ref, xs_ref, oap_ref, oas_ref, odp_ref, ods_ref, gabp_ref, gabs_ref, mod_ref, g2_ref) = refs[:10]
    n_w = 6
    srcs = refs[10:10 + n_w]
    yp_ref, ys_ref = refs[10 + n_w:12 + n_w]
    weights = refs[12 + n_w:12 + 2 * n_w]
    stage, sems = refs[12 + 2 * n_w:]
    step = pl.program_id(0)

    @pl.when(step == 0)
    def _():
        _round_weights_once(layer, srcs, weights, stage, sems)

    wpa_ref, wpb_ref, wo_ref, wg_ref, wu_ref, wd_ref = weights
    is_lat = step >= n_ctx_tiles
    pick = lambda p_ref, s_ref: jnp.where(is_lat, s_ref[...], p_ref[...])
    _, _, gate1 = _mod_parts(mod_ref, True)
    shift2, scale2, gate2 = _mod_parts(mod_ref, False)
    gab = pick(gabp_ref, gabs_ref).astype(F32)
    merged = (_sigmoid(gab[:, :D_MODEL]) * _dot(pick(oap_ref, oas_ref), wpa_ref[...])
              + _sigmoid(gab[:, D_MODEL:]) * _dot(pick(odp_ref, ods_ref), wpb_ref[...]))
    x1 = pick(xp_ref, xs_ref) + gate1 * _dot(merged.astype(BF16), wo_ref[...])
    h2 = _rms_mod(x1, g2_ref[...], shift2, scale2).astype(BF16)
    act = _silu(_dot(h2, wg_ref[...])) * _dot(h2, wu_ref[...])
    y = x1 + gate2 * _dot(act.astype(BF16), wd_ref[...])

    @pl.when(jnp.logical_not(is_lat))
    def _():
        yp_ref[...] = y

    @pl.when(is_lat)
    def _():
        ys_ref[...] = y


def _post(ctx, lat, mod, lat_tiles_per_group, norm2_g, weights, layer, tm):
    n_ctx = ctx[0].shape[0] // tm
    n_lat = lat[0].shape[0] // tm
    widths = (D_MODEL, ATTN_Q_W, DN_W, 2 * D_MODEL)
    ctx_block = lambda w: pl.BlockSpec((tm, w), lambda i: (jnp.minimum(i, n_ctx - 1), 0))
    lat_block = lambda w: pl.BlockSpec((tm, w), lambda i: (jnp.maximum(i - n_ctx, 0), 0))
    group = lambda i: jnp.where(i < n_ctx, 0, 1 + jnp.maximum(i - n_ctx, 0) // lat_tiles_per_group)
    in_specs, args = [], []
    for w, a, b in zip(widths, ctx, lat):
        in_specs += [ctx_block(w), lat_block(w)]
        args += [a, b]
    in_specs += [pl.BlockSpec((None, 1, 6 * D_MODEL), lambda i: (group(i), 0, 0)), _resident((1, D_MODEL))]
    in_specs += [pl.BlockSpec(memory_space=pl.ANY)] * len(weights)
    shapes = [w.shape[1:] for w in weights]
    return pl.pallas_call(
        functools.partial(_post_kernel, layer=layer, n_ctx_tiles=n_ctx),
        grid=(n_ctx + n_lat,),
        in_specs=in_specs,
        out_specs=[ctx_block(D_MODEL), lat_block(D_MODEL)],
        out_shape=[jax.ShapeDtypeStruct(ctx[0].shape, F32), jax.ShapeDtypeStruct(lat[0].shape, F32)],
        scratch_shapes=([pltpu.VMEM(sh, BF16) for sh in shapes]
                        + [pltpu.VMEM((2, CAST_ROWS, max(sh[1] for sh in shapes)), F32),
                           pltpu.SemaphoreType.DMA((2,))]),
        compiler_params=_params(),
        name="post_block",
    )(*args, mod, norm2_g, *weights)


def _rope_tables(n_tokens):
    quarter = HEAD_DIM // 4
    lane = jnp.arange(LANES)
    d = lane % HEAD_DIM
    inv = ROPE_THETA ** (-(d % quarter).astype(F32) / quarter)
    t = jnp.arange(n_tokens)
    pos = jnp.where(d[None, :] < HEAD_DIM // 2, (t // GRID_W)[:, None], (t % GRID_W)[:, None]).astype(F32)
    ang = pos * inv[None, :]
    sign = jnp.where((d % (2 * quarter)) < quarter, -1.0, 1.0)
    return jnp.cos(ang), jnp.sin(ang) * sign[None, :]


def _pack_states(s):
    b = s.shape[0]
    return s.transpose(0, 2, 1, 3).reshape(b, DN_DK, DN_W)


def _layer_weights(l, w_in, norm1_g, q_norm_g, k_norm_g, a_log, dt_bias, dn_norm_g, norm2_g):
    pad_small = lambda a: jnp.pad(a.reshape(1, -1), ((0, 0), (0, LANES - a.size)))
    return dict(
        w_in=w_in,
        w_gab=w_in[l, :, _GATE_OFF:],
        norm1_g=norm1_g[l].reshape(1, D_MODEL),
        norm2_g=norm2_g[l].reshape(1, D_MODEL),
        gq=jnp.tile(q_norm_g[l], N_Q_HEADS).reshape(1, ATTN_Q_W),
        gk=jnp.tile(k_norm_g[l], N_KV_HEADS).reshape(1, ATTN_KV_W),
        a_log=pad_small(jnp.concatenate([jnp.zeros((2 * DN_HEADS,), F32), a_log[l].reshape(-1)])),
        dt_bias=pad_small(jnp.concatenate([jnp.zeros((2 * DN_HEADS,), F32), dt_bias[l].reshape(-1)])),
        gn=jnp.tile(dn_norm_g[l], DN_HEADS).reshape(1, DN_W),
    )


def kernel(x_prompt, x_sample, cache_k, cache_v, state_fwd, state_bwd, c, c_ctx, w_ada, b_ada, norm1_g, w_in,
           q_norm_g, k_norm_g, conv_w, a_log, dt_bias, dn_norm_g, w_pa, w_pb, w_o, norm2_g, w_gate, w_up, w_down):
    batch, seq, _ = x_prompt.shape
    dec_batch, dec_seq, _ = x_sample.shape
    depth = w_in.shape[0]
    past = cache_k.shape[2]
    assert dec_batch + 1 <= SUBLANES and seq % MXU_DIM == 0 and dec_seq % MXU_DIM == 0
    assert batch % SEQS_PER_STEP == 0 and dec_batch % SEQS_PER_STEP == 0 and batch % ATTN_SEQS_PER_STEP == 0

    cvecs = jnp.zeros((SUBLANES, D_MODEL), F32).at[0].set(c_ctx).at[1:1 + dec_batch].set(c)
    rope_tables = _rope_tables(dec_seq)
    tm_in = IN_PROJ_TILE
    tm_post = POST_TILE
    ctx_group = lambda i: 0

    def lat_group(tm):
        return lambda i: 1 + i // (dec_seq // tm)

    yp = x_prompt.reshape(batch * seq, D_MODEL)
    ys = x_sample.reshape(dec_batch * dec_seq, D_MODEL)
    ks_out, vs_out, sf_out, sb_out = [], [], [], []
    w_in = w_in.astype(BF16)
    for l in range(depth):
        lw = _layer_weights(l, w_in, norm1_g, q_norm_g, k_norm_g, a_log, dt_bias, dn_norm_g, norm2_g)
        mod = _modulation(cvecs, w_ada[l], b_ada[l])[:1 + dec_batch].reshape(1 + dec_batch, 1, 6 * D_MODEL)

        q, kt, vt, cqkv, dz, ba, gab = _in_proj(yp, mod, ctx_group, lw, conv_w, l, tm_in, seq)
        oa = _attention(q, [(kt, vt, seq)], seq, seq, ATTN_SEQS_PER_STEP)
        od, sf, sb = _delta(cqkv, dz, ba, lw, seq, SEQS_PER_STEP)
        ctx_parts = (yp, oa, od, gab)
        ks_out.append(kt.transpose(0, 3, 1, 2))
        vs_out.append(vt.transpose(0, 3, 1, 2))
        sf_out.append(sf)
        sb_out.append(sb)

        q, kr, v, cqkv, dz, ba, gab = _in_proj(ys, mod, lat_group(tm_in), lw, conv_w, l, tm_in, dec_seq,
                                               rope_tables)
        ck = cache_k[:, l].reshape(dec_batch * past, ATTN_KV_W)
        cv = cache_v[:, l].reshape(dec_batch * past, ATTN_KV_W)
        oa = _attention(q, [(ck, cv, past), (kr, v, dec_seq)], dec_seq, ATTN_Q_TILE)
        init = (_pack_states(state_fwd[:, l]), _pack_states(state_bwd[:, l]))
        od, _, _ = _delta(cqkv, dz, ba, lw, dec_seq, SEQS_PER_STEP, init)
        yp, ys = _post(ctx_parts, (ys, oa, od, gab), mod, dec_seq // tm_post, lw["norm2_g"],
                       [w_pa, w_pb, w_o, w_gate, w_up, w_down], l, tm_post)

    return (yp.reshape(batch, seq, D_MODEL), ys.reshape(dec_batch, dec_seq, D_MODEL),
            jnp.stack(ks_out, axis=1), jnp.stack(vs_out, axis=1),
            jnp.stack(sf_out, axis=1), jnp.stack(sb_out, axis=1))
```

```python
import functools
import math

import jax
import jax.numpy as jnp
from jax import lax
from jax.experimental import pallas as pl
from jax.experimental.pallas import tpu as pltpu

D_MODEL = 1024
GRID_W = 64
HEAD_DIM = 64
N_Q_HEADS = 8
N_KV_HEADS = 2
Q_GROUP = N_Q_HEADS // N_KV_HEADS
ATTN_Q_W = N_Q_HEADS * HEAD_DIM
ATTN_KV_W = N_KV_HEADS * HEAD_DIM
ROPE_THETA = 10000.0
DN_HEADS = 8
DN_DK = 64
DN_DV = 64
DN_W = DN_HEADS * DN_DK
DN_CONV = 5
CHUNK = 64
EPS = 1e-6
LOG2_E = math.log2(math.e)

LANES = 128
SUBLANES = 8
MXU_DIM = 256
VMEM_LIMIT_BYTES = 56 * 1024 * 1024

HEADS_PER_GROUP = MXU_DIM // DN_DK
N_GROUPS = DN_HEADS // HEADS_PER_GROUP
GW = HEADS_PER_GROUP * DN_DK
INVERSE_BASE = 8

IN_PROJ_TILE = 512
POST_TILE = 256
ATTN_Q_TILE = 256
SEQS_PER_STEP = 2
ATTN_SEQS_PER_STEP = 4
ADALN_TILE = 2048
CAST_ROWS = 256

_QKV_W = ATTN_Q_W + 2 * ATTN_KV_W
_DN_OFF = _QKV_W
_BA_OFF = _DN_OFF + 4 * DN_W
_GATE_OFF = _BA_OFF + 4 * DN_HEADS
_BA_END = _BA_OFF + LANES

F32 = jnp.float32
BF16 = jnp.bfloat16


def _dot(a, b):
    return jnp.dot(a, b, preferred_element_type=F32)


def _dot_nt(a, b):
    return lax.dot_general(a, b, (((1,), (1,)), ((), ())), preferred_element_type=F32)


def _split2(x):
    hi = x.astype(BF16)
    lo = (x - hi.astype(F32)).astype(BF16)
    return hi, lo


def _split3(x):
    hi = x.astype(BF16)
    r = x - hi.astype(F32)
    mid = r.astype(BF16)
    lo = (r - mid.astype(F32)).astype(BF16)
    return hi, mid, lo


def _dot_exact_lhs(a01, b):
    n = b.shape[1]
    b1, b2, b3 = _split3(b)
    r = _dot(a01, jnp.concatenate([b1, b2, b3], axis=1))
    return r[:, :n] + r[:, n:2 * n] + r[:, 2 * n:]


def _iota(shape, dim):
    return lax.broadcasted_iota(jnp.int32, shape, dim)


def _same_block(shape, width):
    return (_iota(shape, 0) // width) == (_iota(shape, 1) // width)


def _head_sumsq(x, width):
    m, n = x.shape
    slab = min(n, MXU_DIM)
    sel = _same_block((slab, slab), width).astype(BF16)
    outs = []
    for s in range(n // slab):
        xs = x[:, s * slab:(s + 1) * slab]
        outs.append(_dot((xs * xs).astype(BF16), sel))
    return outs[0] if len(outs) == 1 else jnp.concatenate(outs, axis=1)


def _sigmoid(x):
    return 1.0 / (1.0 + jnp.exp(-x))


def _silu(x):
    return x * _sigmoid(x)


def _softplus(x):
    return jnp.maximum(x, 0.0) + jnp.log(1.0 + jnp.exp(-jnp.abs(x)))


def _resident(shape):
    nd = len(shape)
    return pl.BlockSpec(shape, lambda *_: (0,) * nd, pipeline_mode=pl.Buffered(1))


def _params(n_axes=1):
    return pltpu.CompilerParams(dimension_semantics=("arbitrary",) * n_axes,
                                vmem_limit_bytes=VMEM_LIMIT_BYTES)


def _mod_kernel(c_ref, w_ref, b_ref, o_ref):
    m = c_ref.shape[0]
    hi, lo = _split2(_silu(c_ref[...]))
    r = _dot(jnp.concatenate([hi, lo], axis=0), w_ref[...].astype(BF16))
    o_ref[...] = r[:m] + r[m:] + b_ref[...]


def _modulation(cvecs, w_ada, b_ada):
    n = w_ada.shape[1]
    tn = ADALN_TILE
    return pl.pallas_call(
        _mod_kernel,
        grid=(n // tn,),
        in_specs=[pl.BlockSpec((SUBLANES, D_MODEL), lambda j: (0, 0)),
                  pl.BlockSpec((D_MODEL, tn), lambda j: (0, j)),
                  pl.BlockSpec((1, tn), lambda j: (0, j))],
        out_specs=pl.BlockSpec((SUBLANES, tn), lambda j: (0, j)),
        out_shape=jax.ShapeDtypeStruct((SUBLANES, n), F32),
        compiler_params=_params(),
        name="adaln_modulation",
    )(cvecs, w_ada, b_ada.reshape(1, n))


def _mod_parts(mod_ref, first):
    m = mod_ref[...]
    base = 0 if first else 3 * D_MODEL
    return (m[:, base:base + D_MODEL], m[:, base + D_MODEL:base + 2 * D_MODEL],
            m[:, base + 2 * D_MODEL:base + 3 * D_MODEL])


def _rms_mod(x, g, shift, scale):
    ms = jnp.mean(x * x, axis=-1, keepdims=True)
    return (x * lax.rsqrt(ms + EPS) * g) * (1.0 + scale) + shift


def _rope(x, cos, sin_signed):
    outs = []
    half = HEAD_DIM // 4
    first_half = (_iota((1, LANES), 1) % (2 * half)) < half
    for s in range(x.shape[1] // LANES):
        xs = x[:, s * LANES:(s + 1) * LANES]
        partner = jnp.where(first_half, pltpu.roll(xs, LANES - half, axis=1), pltpu.roll(xs, half, axis=1))
        outs.append(xs * cos + partner * sin_signed)
    return outs[0] if len(outs) == 1 else jnp.concatenate(outs, axis=1)


def _conv_silu(prev, x, nxt, taps):
    rows = x.shape[0]
    xe = jnp.concatenate([prev, x, nxt], axis=0)
    ne = rows + 2 * SUBLANES
    half = (DN_CONV - 1) // 2
    y = jnp.zeros(x.shape, F32)
    for tap in range(DN_CONV):
        d = tap - half
        sh = xe if d == 0 else pltpu.roll(xe, (ne - d) % ne, axis=0)
        y = y + sh[SUBLANES:SUBLANES + rows] * taps[tap:tap + 1]
    return _silu(y)


def _in_proj_kernel(*refs, rope, halo, seq):
    refs = list(refs)
    x_ref = refs.pop(0)
    xp_ref, xn_ref = (refs.pop(0), refs.pop(0)) if halo else (None, None)
    mod_ref, g_ref, w_ref, wgab_ref, gq_ref, gk_ref, cw_ref = (refs.pop(0) for _ in range(7))
    cos_ref, sin_ref = (refs.pop(0), refs.pop(0)) if rope else (None, None)
    q_ref, k_ref, v_ref, cqkv_ref, dz_ref, ba_ref, gab_ref = refs
    tm = x_ref.shape[0]
    shift, scale, _ = _mod_parts(mod_ref, True)
    h = _rms_mod(x_ref[...], g_ref[...], shift, scale).astype(BF16)

    qkv = _dot(h, w_ref[:, 0:_QKV_W])
    aq = qkv[:, :ATTN_Q_W]
    ak = qkv[:, ATTN_Q_W:ATTN_Q_W + ATTN_KV_W]
    av = qkv[:, ATTN_Q_W + ATTN_KV_W:]
    qn = aq * lax.rsqrt(_head_sumsq(aq, HEAD_DIM) * (1.0 / HEAD_DIM) + EPS) * gq_ref[...]
    kn = ak * lax.rsqrt(_head_sumsq(ak, HEAD_DIM) * (1.0 / HEAD_DIM) + EPS) * gk_ref[...]
    if rope:
        cos, sin = cos_ref[...], sin_ref[...]
        qn = _rope(qn, cos, sin)
        k_ref[...] = _rope(kn, cos, sin)
        v_ref[...] = av
    else:
        kt = kn.T
        vt = av.T
        for s in range(tm // seq):
            for hd in range(N_KV_HEADS):
                k_ref[s, hd] = kt[hd * HEAD_DIM:(hd + 1) * HEAD_DIM, s * seq:(s + 1) * seq]
                v_ref[s, hd] = vt[hd * HEAD_DIM:(hd + 1) * HEAD_DIM, s * seq:(s + 1) * seq]
    q_ref[...] = (qn * (HEAD_DIM ** -0.5 * LOG2_E)).astype(BF16)

    dn = _dot(h, w_ref[:, _DN_OFF:_BA_OFF])
    dz_ref[...] = dn[:, 3 * DN_W:]
    ba_ref[...] = _dot(h, w_ref[:, _BA_OFF:_BA_END])
    gab_ref[...] = _dot(h, wgab_ref[...]).astype(BF16)

    zeros = jnp.zeros((SUBLANES, 3 * DN_W), F32)
    if halo:
        tiles_per_seq = seq // tm
        pos = pl.program_id(0) % tiles_per_seq
        xh = jnp.concatenate([xp_ref[...], xn_ref[...]], axis=0)
        hh = _rms_mod(xh, g_ref[...], shift, scale).astype(BF16)
        dh = _dot(hh, w_ref[:, _DN_OFF:_DN_OFF + 3 * DN_W])
        edges = [(jnp.where(pos > 0, dh[:SUBLANES], 0.0), jnp.where(pos < tiles_per_seq - 1, dh[SUBLANES:], 0.0))]
        sub = tm
    else:
        sub = seq
        edges = [(zeros, zeros)] * (tm // seq)
    for s, (prev, nxt) in enumerate(edges):
        rows = slice(s * sub, (s + 1) * sub)
        for part in range(3):
            cols = slice(part * DN_W, (part + 1) * DN_W)
            y = _conv_silu(prev[:, cols], dn[rows, cols], nxt[:, cols], cw_ref[:, cols])
            if part == 0:
                y = y * lax.rsqrt(_head_sumsq(y, DN_DK) + EPS) * (DN_DK ** -0.5)
            elif part == 1:
                y = y * lax.rsqrt(_head_sumsq(y, DN_DK) + EPS)
            cqkv_ref[rows, cols] = y


def _in_proj(x, mod, group_of_tile, lw, conv_w, layer, tm, seq, rope_tables=None):
    t = x.shape[0]
    rope = rope_tables is not None
    halo = tm < seq
    assert tm % seq == 0 or seq % tm == 0
    row = lambda w: pl.BlockSpec((tm, w), lambda i: (i, 0))
    in_specs = [row(D_MODEL)]
    args = [x]
    if halo:
        blocks_per_tile = tm // SUBLANES
        last_block = t // SUBLANES - 1
        in_specs += [pl.BlockSpec((SUBLANES, D_MODEL), lambda i: (jnp.maximum(i * blocks_per_tile - 1, 0), 0)),
                     pl.BlockSpec((SUBLANES, D_MODEL),
                                  lambda i: (jnp.minimum((i + 1) * blocks_per_tile, last_block), 0))]
        args += [x, x]
    in_specs += [pl.BlockSpec((None, 1, 6 * D_MODEL), lambda i: (group_of_tile(i), 0, 0)),
                 _resident((1, D_MODEL)),
                 pl.BlockSpec((None, D_MODEL, _BA_END), lambda i: (layer, 0, 0), pipeline_mode=pl.Buffered(1)),
                 _resident(lw["w_gab"].shape),
                 _resident((1, ATTN_Q_W)), _resident((1, ATTN_KV_W)),
                 pl.BlockSpec((None, DN_CONV, 3 * DN_W), lambda i: (layer, 0, 0), pipeline_mode=pl.Buffered(1))]
    args += [mod, lw["norm1_g"], lw["w_in"], lw["w_gab"], lw["gq"], lw["gk"], conv_w]
    if rope:
        cos, sin = rope_tables
        tiles_per_seq = cos.shape[0] // tm
        in_specs += [pl.BlockSpec((tm, LANES), lambda i: (i % tiles_per_seq, 0))] * 2
        args += [cos, sin]
        kv_spec = row(ATTN_KV_W)
        kv_shape = jax.ShapeDtypeStruct((t, ATTN_KV_W), F32)
    else:
        kv_spec = pl.BlockSpec((tm // seq, N_KV_HEADS, HEAD_DIM, seq), lambda i: (i, 0, 0, 0))
        kv_shape = jax.ShapeDtypeStruct((t // seq, N_KV_HEADS, HEAD_DIM, seq), F32)
    widths = (3 * DN_W, DN_W, LANES)
    return pl.pallas_call(
        functools.partial(_in_proj_kernel, rope=rope, halo=halo, seq=seq),
        grid=(t // tm,),
        in_specs=in_specs,
        out_specs=[row(ATTN_Q_W), kv_spec, kv_spec] + [row(w) for w in widths] + [row(2 * D_MODEL)],
        out_shape=[jax.ShapeDtypeStruct((t, ATTN_Q_W), BF16), kv_shape, kv_shape]
                  + [jax.ShapeDtypeStruct((t, w), F32) for w in widths]
                  + [jax.ShapeDtypeStruct((t, 2 * D_MODEL), BF16)],
        compiler_params=_params(),
        name="in_proj_rope" if rope else "in_proj",
    )(*args)


def _attn_kernel(*refs, transposed, n_seq):
    n_parts = len(transposed)
    q_ref = refs[0]
    kv_refs = refs[1:1 + 2 * n_parts]
    o_ref = refs[1 + 2 * n_parts]
    tq = q_ref.shape[0] // n_seq
    qt = q_ref[...].astype(F32).T.astype(BF16)
    chains = [(s, kvh) for s in range(n_seq) for kvh in range(N_KV_HEADS)]
    qgs = [jnp.concatenate([qt[j * HEAD_DIM:(j + 1) * HEAD_DIM, s * tq:(s + 1) * tq]
                            for j in range(kvh * Q_GROUP, (kvh + 1) * Q_GROUP)], axis=1) for s, kvh in chains]

    def keys(p, s, kvh):
        k_ref = kv_refs[2 * p]
        k = k_ref[s, kvh].T if transposed[p] else k_ref[:, kvh * HEAD_DIM:(kvh + 1) * HEAD_DIM]
        return k.astype(BF16)

    def values_t(p, s, kvh):
        v_ref = kv_refs[2 * p + 1]
        v = v_ref[s, kvh] if transposed[p] else v_ref[:, kvh * HEAD_DIM:(kvh + 1) * HEAD_DIM].T
        return v.astype(BF16)

    ss = [[_dot(keys(p, s, kvh), qg) for p in range(n_parts)] for (s, kvh), qg in zip(chains, qgs)]
    ms = [functools.reduce(jnp.maximum, [jnp.max(sc, axis=0, keepdims=True) for sc in sp]) for sp in ss]
    ps = [[jnp.exp2(sc - m) for sc in sp] for sp, m in zip(ss, ms)]
    dens = [functools.reduce(jnp.add, [jnp.sum(p, axis=0, keepdims=True) for p in pp]) for pp in ps]
    accs = [functools.reduce(jnp.add, [_dot(values_t(p, s, kvh), pr.astype(BF16)) for p, pr in enumerate(pp)])
            for (s, kvh), pp in zip(chains, ps)]
    for s in range(n_seq):
        outs = []
        for kvh in range(N_KV_HEADS):
            o = accs[s * N_KV_HEADS + kvh] / dens[s * N_KV_HEADS + kvh]
            outs += [o[:, g * tq:(g + 1) * tq] for g in range(Q_GROUP)]
        o_ref[s * tq:(s + 1) * tq, :] = jnp.concatenate(outs, axis=0).T.astype(BF16)


def _attention(q, parts, seq_q, tq, n_seq=1):
    t = q.shape[0]
    nq = seq_q // tq
    assert n_seq == 1 or (nq == 1 and all(k.ndim == 4 for k, _, _ in parts))
    in_specs = [pl.BlockSpec((n_seq * tq, ATTN_Q_W), lambda b, i: (b * nq + i, 0))]
    args = [q]
    transposed = []
    for k, v, seq_k in parts:
        transposed.append(k.ndim == 4)
        if k.ndim == 4:
            in_specs += [pl.BlockSpec((n_seq, N_KV_HEADS, HEAD_DIM, seq_k), lambda b, i: (b, 0, 0, 0))] * 2
        else:
            in_specs += [pl.BlockSpec((seq_k, ATTN_KV_W), lambda b, i: (b, 0))] * 2
        args += [k, v]
    return pl.pallas_call(
        functools.partial(_attn_kernel, transposed=tuple(transposed), n_seq=n_seq),
        grid=(t // (seq_q * n_seq), nq),
        in_specs=in_specs,
        out_specs=pl.BlockSpec((n_seq * tq, ATTN_Q_W), lambda b, i: (b * nq + i, 0)),
        out_shape=jax.ShapeDtypeStruct((t, ATTN_Q_W), BF16),
        compiler_params=_params(2),
        name="attention_%dparts" % len(parts),
    )(*args)


def _block_diag(x, mask):
    reps = LANES // DN_DK
    return [jnp.where(mask, jnp.concatenate([x[:, c * LANES:(c + 1) * LANES]] * reps, axis=0),
                      jnp.zeros((), x.dtype)) for c in range(x.shape[1] // LANES)]


def _bdot(x, tiles, nt=False):
    op = _dot_nt if nt else _dot
    return jnp.concatenate([op(x[:, c * LANES:(c + 1) * LANES], t) for c, t in enumerate(tiles)], axis=1)


def _heads_transposed(x):
    xt = x.T
    return jnp.concatenate([xt[hb * DN_DK:(hb + 1) * DN_DK] for hb in range(HEADS_PER_GROUP)], axis=1)


def _packed_unit_inverse(lms, eye, row, col, mask):
    def mm1(x, y):
        return _bdot(x, _block_diag(y, mask))

    b = INVERSE_BASE
    ns = [-jnp.where((row // b) == (col // b), lm, 0.0) for lm in lms]
    ts = [eye + n for n in ns]
    ps = [mm1(nb, nb) for nb in (n.astype(BF16) for n in ns)]
    steps = int(math.log2(b)) - 1
    for s in range(steps):
        last = s == steps - 1
        pbs = [p.astype(BF16) for p in ps]
        prods = [mm1(t.astype(BF16) if last else jnp.concatenate([t.astype(BF16), pb], axis=0), pb)
                 for t, pb in zip(ts, pbs)]
        ts = [t + prod[:CHUNK] for t, prod in zip(ts, prods)]
        if not last:
            ps = [prod[CHUNK:] for prod in prods]
    while b < CHUNK:
        between = ((row // (2 * b)) == (col // (2 * b))) & ((row // b) != (col // b))
        tbs = [t.astype(BF16) for t in ts]
        ys = [mm1(tb, jnp.where(between, lm, 0.0).astype(BF16)) for lm, tb in zip(lms, tbs)]
        ts = [t - mm1(y.astype(BF16), tb) for t, tb, y in zip(ts, tbs, ys)]
        b *= 2
    return ts


def _delta_kernel(*refs, seq, n_seq, has_init):
    if has_init:
        (x_ref, z_ref, ba_ref, alog_ref, dtb_ref, gn_ref, s0f_ref, s0b_ref,
         o_ref, sf_ref, sb_ref, gate_s, o_s, st_s) = refs
    else:
        (x_ref, z_ref, ba_ref, alog_ref, dtb_ref, gn_ref,
         o_ref, sf_ref, sb_ref, gate_s, o_s, st_s) = refs
    n_chunks = seq // CHUNK
    rb = MXU_DIM
    n_gate = 4 * DN_HEADS

    exp_r = _iota((LANES, 4 * DN_W), 0)
    expand = ((exp_r < 3 * n_gate) & ((_iota((LANES, 4 * DN_W), 1) // DN_DK) == exp_r % n_gate)).astype(BF16)
    lane = _iota((1, LANES), 1)
    blk_r = _iota((rb, rb), 0)
    blk_c = _iota((rb, rb), 1)
    same_chunk = (blk_r // CHUNK) == (blk_c // CHUNK)
    cum_f = (same_chunk & (blk_c <= blk_r)).astype(BF16)
    cum_b = (same_chunk & (blk_c >= blk_r)).astype(BF16)
    for blk in range(n_seq * seq // rb):
        rows = slice(blk * rb, (blk + 1) * rb)
        ba = ba_ref[rows, :]
        decay = -jnp.exp(alog_ref[...]) * _softplus(ba + dtb_ref[...])
        vals = jnp.where(lane < 2 * DN_HEADS, _sigmoid(ba), jnp.where(lane < n_gate, decay, 0.0))
        narrow = jnp.where(lane < 2 * DN_HEADS, vals,
                           jnp.where(lane < 3 * DN_HEADS, _dot_exact_lhs(cum_f, vals),
                                     jnp.where(lane < n_gate, _dot_exact_lhs(cum_b, vals), 0.0)))
        t1, t2, t3 = (t.astype(F32) for t in _split3(narrow))
        stacked = t1 + pltpu.roll(t2, n_gate, axis=1) + pltpu.roll(t3, 2 * n_gate, axis=1)
        gate_s[rows, :] = _dot(stacked.astype(BF16), expand)

    bd_mask = _same_block((LANES, LANES), DN_DK)
    row = _iota((CHUNK, GW), 0)
    col = _iota((CHUNK, GW), 1) % CHUNK
    diag = row == col
    eye = diag.astype(F32)
    dirs = ((col <= row, col < row, CHUNK - 1), (col >= row, col > row, 0))
    chains = [(s, d, g) for s in range(n_seq) for d in range(2) for g in range(N_GROUPS)]
    for ci, (s, d, g) in enumerate(chains):
        if has_init:
            st_s[ci] = (s0f_ref, s0b_ref)[d][s, :, g * GW:(g + 1) * GW]
        else:
            st_s[ci] = jnp.zeros((DN_DK, GW), F32)

    def bd(x):
        return _block_diag(x.astype(BF16), bd_mask)

    def chunk_step(n, carry):
        where = []
        for s, d, g in chains:
            c = n if d == 0 else n_chunks - 1 - n
            where.append((pl.ds(pl.multiple_of(s * seq + c * CHUNK, CHUNK), CHUNK), d, g))

        def load(ci, what):
            rows, d, g = where[ci]
            off = {"q": g * GW, "k": DN_W + g * GW, "v": 2 * DN_W + g * GW}
            if what in off:
                return x_ref[rows, pl.ds(off[what], GW)]
            return gate_s[rows, pl.ds((0 if what == "beta" else 2 * DN_W) + d * DN_W + g * GW, GW)]

        n_ch = len(chains)
        grams, decs = [], []
        for ci, (rows, d, g) in enumerate(where):
            incl = dirs[d][0]
            gc = load(ci, "gc")
            gc_col = jnp.sum(jnp.where(diag, gc, 0.0), axis=0, keepdims=True)
            decs.append(jnp.where(incl, jnp.exp(jnp.minimum(gc - gc_col, 0.0)), 0.0))
            k = load(ci, "k")
            lhs = jnp.concatenate([k * load(ci, "beta"), load(ci, "q")], axis=0).astype(BF16)
            grams.append(_bdot(lhs, bd(k), nt=True))
        lms = [jnp.where(dirs[d][1], gm[:CHUNK] * dec, 0.0) for (_, d, _), gm, dec in zip(where, grams, decs)]
        attn = [(gm[CHUNK:] * dec).astype(BF16) for gm, dec in zip(grams, decs)]
        ts = [t.astype(BF16) for t in _packed_unit_inverse(lms, eye, row, col, bd_mask)]
        us, ws = [], []
        for ci, t in enumerate(ts):
            beta = load(ci, "beta")
            us.append(_bdot(t, bd(load(ci, "v") * beta)))
            ws.append(_bdot(t, bd(load(ci, "k") * beta * jnp.exp(load(ci, "gc")))))
        states = [st_s[ci] for ci in range(n_ch)]
        ws_qs = [_bdot(jnp.concatenate([w, load(ci, "q") * jnp.exp(load(ci, "gc"))], axis=0).astype(BF16), bd(st))
                 for ci, (w, st) in enumerate(zip(ws, states))]
        g_lasts, lhs2 = [], []
        for ci, (rows, d, g) in enumerate(where):
            gc = load(ci, "gc")
            last_row = dirs[d][2]
            g_last = gc[last_row:last_row + 1, :]
            g_lasts.append(g_last)
            k_dec = load(ci, "k") * jnp.exp(g_last - gc)
            lhs2.append(jnp.concatenate([attn[ci], _heads_transposed(k_dec).astype(BF16)], axis=0))
        v_bds = [bd(u - x[:CHUNK]) for u, x in zip(us, ws_qs)]
        avs = [_bdot(l2, vb) for l2, vb in zip(lhs2, v_bds)]
        for ci, (rows, d, g) in enumerate(where):
            st_s[ci] = states[ci] * jnp.exp(g_lasts[ci]) + avs[ci][CHUNK:]
            o_s[d, rows, g * GW:(g + 1) * GW] = ws_qs[ci][CHUNK:] + avs[ci][:CHUNK]
        return carry

    lax.fori_loop(0, n_chunks, chunk_step, 0)

    for blk in range(n_seq * seq // rb):
        rows = slice(blk * rb, (blk + 1) * rb)
        o = o_s[0, rows, :] + o_s[1, rows, :]
        o = o * lax.rsqrt(_head_sumsq(o, DN_DV) * (1.0 / DN_DV) + EPS) * gn_ref[...]
        o_ref[rows, :] = (o * _silu(z_ref[rows, :])).astype(BF16)
    for ci, (s, d, g) in enumerate(chains):
        st = st_s[ci]
        for hb in range(HEADS_PER_GROUP):
            (sf_ref, sb_ref)[d][s, g * HEADS_PER_GROUP + hb] = st[:, hb * DN_DV:(hb + 1) * DN_DV]


def _delta(cqkv, dz, ba, lw, seq, n_seq, init=None):
    t = cqkv.shape[0]
    nb = t // seq
    rows = n_seq * seq
    has_init = init is not None
    seq_block = lambda w: pl.BlockSpec((rows, w), lambda b: (b, 0))
    state_block = pl.BlockSpec((n_seq, DN_DK, DN_W), lambda b: (b, 0, 0))
    in_specs = [seq_block(3 * DN_W), seq_block(DN_W), seq_block(LANES),
                _resident((1, LANES)), _resident((1, LANES)), _resident((1, DN_W))]
    args = [cqkv, dz, ba, lw["a_log"], lw["dt_bias"], lw["gn"]]
    if has_init:
        in_specs += [state_block, state_block]
        args += list(init)
    final_block = pl.BlockSpec((n_seq, DN_HEADS, DN_DK, DN_DV), lambda b: (b, 0, 0, 0))
    final_shape = jax.ShapeDtypeStruct((nb, DN_HEADS, DN_DK, DN_DV), F32)
    return pl.pallas_call(
        functools.partial(_delta_kernel, seq=seq, n_seq=n_seq, has_init=has_init),
        grid=(nb // n_seq,),
        in_specs=in_specs,
        out_specs=[seq_block(DN_W), final_block, final_block],
        out_shape=[jax.ShapeDtypeStruct((t, DN_W), BF16), final_shape, final_shape],
        scratch_shapes=[pltpu.VMEM((rows, 4 * DN_W), F32),
                        pltpu.VMEM((2, rows, DN_W), F32),
                        pltpu.VMEM((n_seq * 2 * N_GROUPS, DN_DK, GW), F32)],
        compiler_params=_params(),
        name="delta_rule_init" if has_init else "delta_rule",
    )(*args)


def _round_weights_once(layer, srcs, dsts, stage, sems):
    chunks = []
    for k, dst in enumerate(dsts):
        rows, width = dst.shape
        chunks += [(k, r0, min(CAST_ROWS, rows - r0), width) for r0 in range(0, rows, CAST_ROWS)]

    def copy_in(j):
        k, r0, n, width = chunks[j]
        return pltpu.make_async_copy(srcs[k].at[layer, pl.ds(r0, n), :],
                                     stage.at[j % 2, pl.ds(0, n), pl.ds(0, width)], sems.at[j % 2])

    copy_in(0).start()
    for j, (k, r0, n, width) in enumerate(chunks):
        if j + 1 < len(chunks):
            copy_in(j + 1).start()
        copy_in(j).wait()
        dsts[k][r0:r0 + n, :] = stage[j % 2, 0:n, 0:width].astype(BF16)


def _post_kernel(*refs, layer, n_ctx_tiles):
    (xp_ref, xs_ref, oap_ref, oas_ref, odp_ref, ods_ref, gabp_ref, gabs_ref, mod_ref, g2_ref) = refs[:10]
    n_w = 6
    srcs = refs[10:10 + n_w]
    yp_ref, ys_ref = refs[10 + n_w:12 + n_w]
    weights = refs[12 + n_w:12 + 2 * n_w]
    stage, sems = refs[12 + 2 * n_w:]
    step = pl.program_id(0)

    @pl.when(step == 0)
    def _():
        _round_weights_once(layer, srcs, weights, stage, sems)

    wpa_ref, wpb_ref, wo_ref, wg_ref, wu_ref, wd_ref = weights

    def block(x_ref, oa_ref, od_ref, gab_ref, y_ref):
        _, _, gate1 = _mod_parts(mod_ref, True)
        shift2, scale2, gate2 = _mod_parts(mod_ref, False)
        gab = gab_ref[...].astype(F32)
        merged = (_sigmoid(gab[:, :D_MODEL]) * _dot(oa_ref[...], wpa_ref[...])
                  + _sigmoid(gab[:, D_MODEL:]) * _dot(od_ref[...], wpb_ref[...]))
        x1 = x_ref[...] + gate1 * _dot(merged.astype(BF16), wo_ref[...])
        h2 = _rms_mod(x1, g2_ref[...], shift2, scale2).astype(BF16)
        act = _silu(_dot(h2, wg_ref[...])) * _dot(h2, wu_ref[...])
        y_ref[...] = x1 + gate2 * _dot(act.astype(BF16), wd_ref[...])

    @pl.when(step < n_ctx_tiles)
    def _():
        block(xp_ref, oap_ref, odp_ref, gabp_ref, yp_ref)

    @pl.when(step >= n_ctx_tiles)
    def _():
        block(xs_ref, oas_ref, ods_ref, gabs_ref, ys_ref)


def _post(ctx, lat, mod, lat_tiles_per_group, norm2_g, weights, layer, tm):
    n_ctx = ctx[0].shape[0] // tm
    n_lat = lat[0].shape[0] // tm
    widths = (D_MODEL, ATTN_Q_W, DN_W, 2 * D_MODEL)
    ctx_block = lambda w: pl.BlockSpec((tm, w), lambda i: (jnp.minimum(i, n_ctx - 1), 0))
    lat_block = lambda w: pl.BlockSpec((tm, w), lambda i: (jnp.maximum(i - n_ctx, 0), 0))
    group = lambda i: jnp.where(i < n_ctx, 0, 1 + jnp.maximum(i - n_ctx, 0) // lat_tiles_per_group)
    in_specs, args = [], []
    for w, a, b in zip(widths, ctx, lat):
        in_specs += [ctx_block(w), lat_block(w)]
        args += [a, b]
    in_specs += [pl.BlockSpec((None, 1, 6 * D_MODEL), lambda i: (group(i), 0, 0)), _resident((1, D_MODEL))]
    in_specs += [pl.BlockSpec(memory_space=pl.ANY)] * len(weights)
    shapes = [w.shape[1:] for w in weights]
    return pl.pallas_call(
        functools.partial(_post_kernel, layer=layer, n_ctx_tiles=n_ctx),
        grid=(n_ctx + n_lat,),
        in_specs=in_specs,
        out_specs=[ctx_block(D_MODEL), lat_block(D_MODEL)],
        out_shape=[jax.ShapeDtypeStruct(ctx[0].shape, F32), jax.ShapeDtypeStruct(lat[0].shape, F32)],
        scratch_shapes=([pltpu.VMEM(sh, BF16) for sh in shapes]
                        + [pltpu.VMEM((2, CAST_ROWS, max(sh[1] for sh in shapes)), F32),
                           pltpu.SemaphoreType.DMA((2,))]),
        compiler_params=_params(),
        name="post_block",
    )(*args, mod, norm2_g, *weights)


def _rope_tables(n_tokens):
    quarter = HEAD_DIM // 4
    lane = jnp.arange(LANES)
    d = lane % HEAD_DIM
    inv = ROPE_THETA ** (-(d % quarter).astype(F32) / quarter)
    t = jnp.arange(n_tokens)
    pos = jnp.where(d[None, :] < HEAD_DIM // 2, (t // GRID_W)[:, None], (t % GRID_W)[:, None]).astype(F32)
    ang = pos * inv[None, :]
    sign = jnp.where((d % (2 * quarter)) < quarter, -1.0, 1.0)
    return jnp.cos(ang), jnp.sin(ang) * sign[None, :]


def _pack_states(s):
    b = s.shape[0]
    return s.transpose(0, 2, 1, 3).reshape(b, DN_DK, DN_W)


def _layer_weights(l, w_in, norm1_g, q_norm_g, k_norm_g, a_log, dt_bias, dn_norm_g, norm2_g):
    pad_small = lambda a: jnp.pad(a.reshape(1, -1), ((0, 0), (0, LANES - a.size)))
    return dict(
        w_in=w_in,
        w_gab=w_in[l, :, _GATE_OFF:],
        norm1_g=norm1_g[l].reshape(1, D_MODEL),
        norm2_g=norm2_g[l].reshape(1, D_MODEL),
        gq=jnp.tile(q_norm_g[l], N_Q_HEADS).reshape(1, ATTN_Q_W),
        gk=jnp.tile(k_norm_g[l], N_KV_HEADS).reshape(1, ATTN_KV_W),
        a_log=pad_small(jnp.concatenate([jnp.zeros((2 * DN_HEADS,), F32), a_log[l].reshape(-1)])),
        dt_bias=pad_small(jnp.concatenate([jnp.zeros((2 * DN_HEADS,), F32), dt_bias[l].reshape(-1)])),
        gn=jnp.tile(dn_norm_g[l], DN_HEADS).reshape(1, DN_W),
    )


def kernel(x_prompt, x_sample, cache_k, cache_v, state_fwd, state_bwd, c, c_ctx, w_ada, b_ada, norm1_g, w_in,
           q_norm_g, k_norm_g, conv_w, a_log, dt_bias, dn_norm_g, w_pa, w_pb, w_o, norm2_g, w_gate, w_up, w_down):
    batch, seq, _ = x_prompt.shape
    dec_batch, dec_seq, _ = x_sample.shape
    depth = w_in.shape[0]
    past = cache_k.shape[2]
    assert dec_batch + 1 <= SUBLANES and seq % MXU_DIM == 0 and dec_seq % MXU_DIM == 0
    assert batch % SEQS_PER_STEP == 0 and dec_batch % SEQS_PER_STEP == 0 and batch % ATTN_SEQS_PER_STEP == 0

    cvecs = jnp.zeros((SUBLANES, D_MODEL), F32).at[0].set(c_ctx).at[1:1 + dec_batch].set(c)
    rope_tables = _rope_tables(dec_seq)
    tm_in = IN_PROJ_TILE
    tm_post = POST_TILE
    ctx_group = lambda i: 0

    def lat_group(tm):
        return lambda i: 1 + i // (dec_seq // tm)

    yp = x_prompt.reshape(batch * seq, D_MODEL)
    ys = x_sample.reshape(dec_batch * dec_seq, D_MODEL)
    ks_out, vs_out, sf_out, sb_out = [], [], [], []
    w_in = w_in.astype(BF16)
    for l in range(depth):
        lw = _layer_weights(l, w_in, norm1_g, q_norm_g, k_norm_g, a_log, dt_bias, dn_norm_g, norm2_g)
        mod = _modulation(cvecs, w_ada[l], b_ada[l])[:1 + dec_batch].reshape(1 + dec_batch, 1, 6 * D_MODEL)

        q, kt, vt, cqkv, dz, ba, gab = _in_proj(yp, mod, ctx_group, lw, conv_w, l, tm_in, seq)
        oa = _attention(q, [(kt, vt, seq)], seq, seq, ATTN_SEQS_PER_STEP)
        od, sf, sb = _delta(cqkv, dz, ba, lw, seq, SEQS_PER_STEP)
        ctx_parts = (yp, oa, od, gab)
        ks_out.append(kt.transpose(0, 3, 1, 2))
        vs_out.append(vt.transpose(0, 3, 1, 2))
        sf_out.append(sf)
        sb_out.append(sb)

        q, kr, v, cqkv, dz, ba, gab = _in_proj(ys, mod, lat_group(tm_in), lw, conv_w, l, tm_in, dec_seq,
                                               rope_tables)
        ck = cache_k[:, l].reshape(dec_batch * past, ATTN_KV_W)
        cv = cache_v[:, l].reshape(dec_batch * past, ATTN_KV_W)
        oa = _attention(q, [(ck, cv, past), (kr, v, dec_seq)], dec_seq, ATTN_Q_TILE)
        init = (_pack_states(state_fwd[:, l]), _pack_states(state_bwd[:, l]))
        od, _, _ = _delta(cqkv, dz, ba, lw, dec_seq, SEQS_PER_STEP, init)
        yp, ys = _post(ctx_parts, (ys, oa, od, gab), mod, dec_seq // tm_post, lw["norm2_g"],
                       [w_pa, w_pb, w_o, w_gate, w_up, w_down], l, tm_post)

    return (yp.reshape(batch, seq, D_MODEL), ys.reshape(dec_batch, dec_seq, D_MODEL),
            jnp.stack(ks_out, axis=1), jnp.stack(vs_out, axis=1),
            jnp.stack(sf_out, axis=1), jnp.stack(sb_out, axis=1))
```

```python
import functools
import math

import jax
import jax.numpy as jnp
from jax import lax
from jax.experimental import pallas as pl
from jax.experimental.pallas import tpu as pltpu

D_MODEL = 1024
GRID_W = 64
HEAD_DIM = 64
N_Q_HEADS = 8
N_KV_HEADS = 2
Q_GROUP = N_Q_HEADS // N_KV_HEADS
ATTN_Q_W = N_Q_HEADS * HEAD_DIM
ATTN_KV_W = N_KV_HEADS * HEAD_DIM
ROPE_THETA = 10000.0
DN_HEADS = 8
DN_DK = 64
DN_DV = 64
DN_W = DN_HEADS * DN_DK
DN_CONV = 5
CHUNK = 64
EPS = 1e-6
LOG2_E = math.log2(math.e)

LANES = 128
SUBLANES = 8
MXU_DIM = 256
VMEM_LIMIT_BYTES = 56 * 1024 * 1024

HEADS_PER_GROUP = MXU_DIM // DN_DK
N_GROUPS = DN_HEADS // HEADS_PER_GROUP
GW = HEADS_PER_GROUP * DN_DK
INVERSE_BASE = 8

IN_PROJ_TILE = 512
POST_TILE = 256
ATTN_Q_TILE = 256
SEQS_PER_STEP = 2
ATTN_SEQS_PER_STEP = 4
ADALN_TILE = 2048
CAST_ROWS = 256

_QKV_W = ATTN_Q_W + 2 * ATTN_KV_W
_DN_OFF = _QKV_W
_BA_OFF = _DN_OFF + 4 * DN_W
_GATE_OFF = _BA_OFF + 4 * DN_HEADS
_BA_END = _BA_OFF + LANES

F32 = jnp.float32
BF16 = jnp.bfloat16


def _dot(a, b):
    return jnp.dot(a, b, preferred_element_type=F32)


def _dot_nt(a, b):
    return lax.dot_general(a, b, (((1,), (1,)), ((), ())), preferred_element_type=F32)


def _split2(x):
    hi = x.astype(BF16)
    lo = (x - hi.astype(F32)).astype(BF16)
    return hi, lo


def _split3(x):
    hi = x.astype(BF16)
    r = x - hi.astype(F32)
    mid = r.astype(BF16)
    lo = (r - mid.astype(F32)).astype(BF16)
    return hi, mid, lo


def _dot_exact_lhs(a01, b):
    n = b.shape[1]
    b1, b2, b3 = _split3(b)
    r = _dot(a01, jnp.concatenate([b1, b2, b3], axis=1))
    return r[:, :n] + r[:, n:2 * n] + r[:, 2 * n:]


def _iota(shape, dim):
    return lax.broadcasted_iota(jnp.int32, shape, dim)


def _same_block(shape, width):
    return (_iota(shape, 0) // width) == (_iota(shape, 1) // width)


def _head_sumsq(x, width):
    m, n = x.shape
    slab = min(n, MXU_DIM)
    sel = _same_block((slab, slab), width).astype(BF16)
    outs = []
    for s in range(n // slab):
        xs = x[:, s * slab:(s + 1) * slab]
        outs.append(_dot((xs * xs).astype(BF16), sel))
    return outs[0] if len(outs) == 1 else jnp.concatenate(outs, axis=1)


def _sigmoid(x):
    return 1.0 / (1.0 + jnp.exp(-x))


def _silu(x):
    return x * _sigmoid(x)


def _softplus(x):
    return jnp.maximum(x, 0.0) + jnp.log(1.0 + jnp.exp(-jnp.abs(x)))


def _resident(shape):
    nd = len(shape)
    return pl.BlockSpec(shape, lambda *_: (0,) * nd, pipeline_mode=pl.Buffered(1))


def _params(n_axes=1):
    return pltpu.CompilerParams(dimension_semantics=("arbitrary",) * n_axes,
                                vmem_limit_bytes=VMEM_LIMIT_BYTES)


def _mod_kernel(c_ref, w_ref, b_ref, o_ref):
    m = c_ref.shape[0]
    hi, lo = _split2(_silu(c_ref[...]))
    r = _dot(jnp.concatenate([hi, lo], axis=0), w_ref[...].astype(BF16))
    o_ref[...] = r[:m] + r[m:] + b_ref[...]


def _modulation(cvecs, w_ada, b_ada):
    n = w_ada.shape[1]
    tn = ADALN_TILE
    return pl.pallas_call(
        _mod_kernel,
        grid=(n // tn,),
        in_specs=[pl.BlockSpec((SUBLANES, D_MODEL), lambda j: (0, 0)),
                  pl.BlockSpec((D_MODEL, tn), lambda j: (0, j)),
                  pl.BlockSpec((1, tn), lambda j: (0, j))],
        out_specs=pl.BlockSpec((SUBLANES, tn), lambda j: (0, j)),
        out_shape=jax.ShapeDtypeStruct((SUBLANES, n), F32),
        compiler_params=_params(),
        name="adaln_modulation",
    )(cvecs, w_ada, b_ada.reshape(1, n))


def _mod_parts(mod_ref, first):
    m = mod_ref[...]
    base = 0 if first else 3 * D_MODEL
    return (m[:, base:base + D_MODEL], m[:, base + D_MODEL:base + 2 * D_MODEL],
            m[:, base + 2 * D_MODEL:base + 3 * D_MODEL])


def _rms_mod(x, g, shift, scale):
    ms = jnp.mean(x * x, axis=-1, keepdims=True)
    return (x * lax.rsqrt(ms + EPS) * g) * (1.0 + scale) + shift


def _round_w_in_kernel(w_ref, main_ref, gab_ref):
    w = w_ref[...]
    main_ref[...] = w[:, :_BA_END].astype(BF16)
    gab_ref[...] = w[:, _GATE_OFF:].astype(BF16)


def _round_w_in(w_in, layer):
    rows, cols = w_in.shape[1:]
    n_gate = cols - _GATE_OFF
    return pl.pallas_call(
        _round_w_in_kernel,
        grid=(rows // CAST_ROWS,),
        in_specs=[pl.BlockSpec((None, CAST_ROWS, cols), lambda i: (layer, i, 0))],
        out_specs=[pl.BlockSpec((CAST_ROWS, _BA_END), lambda i: (i, 0)),
                   pl.BlockSpec((CAST_ROWS, n_gate), lambda i: (i, 0))],
        out_shape=[jax.ShapeDtypeStruct((rows, _BA_END), BF16), jax.ShapeDtypeStruct((rows, n_gate), BF16)],
        compiler_params=_params(),
        name="round_w_in",
    )(w_in)

def _rope(x, cos, sin_signed):
    outs = []
    half = HEAD_DIM // 4
    first_half = (_iota((1, LANES), 1) % (2 * half)) < half
    for s in range(x.shape[1] // LANES):
        xs = x[:, s * LANES:(s + 1) * LANES]
        partner = jnp.where(first_half, pltpu.roll(xs, LANES - half, axis=1), pltpu.roll(xs, half, axis=1))
        outs.append(xs * cos + partner * sin_signed)
    return outs[0] if len(outs) == 1 else jnp.concatenate(outs, axis=1)


def _conv_silu(prev, x, nxt, taps):
    rows = x.shape[0]
    xe = jnp.concatenate([prev, x, nxt], axis=0)
    ne = rows + 2 * SUBLANES
    half = (DN_CONV - 1) // 2
    y = jnp.zeros(x.shape, F32)
    for tap in range(DN_CONV):
        d = tap - half
        sh = xe if d == 0 else pltpu.roll(xe, (ne - d) % ne, axis=0)
        y = y + sh[SUBLANES:SUBLANES + rows] * taps[tap:tap + 1]
    return _silu(y)


def _in_proj_kernel(*refs, rope, halo, seq):
    refs = list(refs)
    x_ref = refs.pop(0)
    xp_ref, xn_ref = (refs.pop(0), refs.pop(0)) if halo else (None, None)
    mod_ref, g_ref, w_ref, wgab_ref, gq_ref, gk_ref, cw_ref = (refs.pop(0) for _ in range(7))
    cos_ref, sin_ref = (refs.pop(0), refs.pop(0)) if rope else (None, None)
    q_ref, k_ref, v_ref, cqkv_ref, dz_ref, ba_ref, gab_ref = refs
    tm = x_ref.shape[0]
    shift, scale, _ = _mod_parts(mod_ref, True)
    h = _rms_mod(x_ref[...], g_ref[...], shift, scale).astype(BF16)

    qkv = _dot(h, w_ref[:, 0:_QKV_W])
    aq = qkv[:, :ATTN_Q_W]
    ak = qkv[:, ATTN_Q_W:ATTN_Q_W + ATTN_KV_W]
    av = qkv[:, ATTN_Q_W + ATTN_KV_W:]
    qn = aq * lax.rsqrt(_head_sumsq(aq, HEAD_DIM) * (1.0 / HEAD_DIM) + EPS) * gq_ref[...]
    kn = ak * lax.rsqrt(_head_sumsq(ak, HEAD_DIM) * (1.0 / HEAD_DIM) + EPS) * gk_ref[...]
    if rope:
        cos, sin = cos_ref[...], sin_ref[...]
        qn = _rope(qn, cos, sin)
        k_ref[...] = _rope(kn, cos, sin)
        v_ref[...] = av
    else:
        kt = kn.T
        vt = av.T
        for s in range(tm // seq):
            for hd in range(N_KV_HEADS):
                k_ref[s, hd] = kt[hd * HEAD_DIM:(hd + 1) * HEAD_DIM, s * seq:(s + 1) * seq]
                v_ref[s, hd] = vt[hd * HEAD_DIM:(hd + 1) * HEAD_DIM, s * seq:(s + 1) * seq]
    q_ref[...] = (qn * (HEAD_DIM ** -0.5 * LOG2_E)).astype(BF16)

    dn = _dot(h, w_ref[:, _DN_OFF:_BA_OFF])
    dz_ref[...] = dn[:, 3 * DN_W:]
    ba_ref[...] = _dot(h, w_ref[:, _BA_OFF:_BA_END])
    gab_ref[...] = _dot(h, wgab_ref[...]).astype(BF16)

    zeros = jnp.zeros((SUBLANES, 3 * DN_W), F32)
    if halo:
        tiles_per_seq = seq // tm
        pos = pl.program_id(0) % tiles_per_seq
        xh = jnp.concatenate([xp_ref[...], xn_ref[...]], axis=0)
        hh = _rms_mod(xh, g_ref[...], shift, scale).astype(BF16)
        dh = _dot(hh, w_ref[:, _DN_OFF:_DN_OFF + 3 * DN_W])
        edges = [(jnp.where(pos > 0, dh[:SUBLANES], 0.0), jnp.where(pos < tiles_per_seq - 1, dh[SUBLANES:], 0.0))]
        sub = tm
    else:
        sub = seq
        edges = [(zeros, zeros)] * (tm // seq)
    for s, (prev, nxt) in enumerate(edges):
        rows = slice(s * sub, (s + 1) * sub)
        for part in range(3):
            cols = slice(part * DN_W, (part + 1) * DN_W)
            y = _conv_silu(prev[:, cols], dn[rows, cols], nxt[:, cols], cw_ref[:, cols])
            if part == 0:
                y = y * lax.rsqrt(_head_sumsq(y, DN_DK) + EPS) * (DN_DK ** -0.5)
            elif part == 1:
                y = y * lax.rsqrt(_head_sumsq(y, DN_DK) + EPS)
            cqkv_ref[rows, cols] = y


def _in_proj(x, mod, group_of_tile, lw, conv_w, layer, tm, seq, rope_tables=None):
    t = x.shape[0]
    rope = rope_tables is not None
    halo = tm < seq
    assert tm % seq == 0 or seq % tm == 0
    row = lambda w: pl.BlockSpec((tm, w), lambda i: (i, 0))
    in_specs = [row(D_MODEL)]
    args = [x]
    if halo:
        blocks_per_tile = tm // SUBLANES
        last_block = t // SUBLANES - 1
        in_specs += [pl.BlockSpec((SUBLANES, D_MODEL), lambda i: (jnp.maximum(i * blocks_per_tile - 1, 0), 0)),
                     pl.BlockSpec((SUBLANES, D_MODEL),
                                  lambda i: (jnp.minimum((i + 1) * blocks_per_tile, last_block), 0))]
        args += [x, x]
    in_specs += [pl.BlockSpec((None, 1, 6 * D_MODEL), lambda i: (group_of_tile(i), 0, 0)),
                 _resident((1, D_MODEL)),
                 _resident(lw["w_in"].shape), _resident(lw["w_gab"].shape),
                 _resident((1, ATTN_Q_W)), _resident((1, ATTN_KV_W)),
                 pl.BlockSpec((None, DN_CONV, 3 * DN_W), lambda i: (layer, 0, 0), pipeline_mode=pl.Buffered(1))]
    args += [mod, lw["norm1_g"], lw["w_in"], lw["w_gab"], lw["gq"], lw["gk"], conv_w]
    if rope:
        cos, sin = rope_tables
        tiles_per_seq = cos.shape[0] // tm
        in_specs += [pl.BlockSpec((tm, LANES), lambda i: (i % tiles_per_seq, 0))] * 2
        args += [cos, sin]
        kv_spec = row(ATTN_KV_W)
        kv_shape = jax.ShapeDtypeStruct((t, ATTN_KV_W), F32)
    else:
        kv_spec = pl.BlockSpec((tm // seq, N_KV_HEADS, HEAD_DIM, seq), lambda i: (i, 0, 0, 0))
        kv_shape = jax.ShapeDtypeStruct((t // seq, N_KV_HEADS, HEAD_DIM, seq), F32)
    widths = (3 * DN_W, DN_W, LANES)
    return pl.pallas_call(
        functools.partial(_in_proj_kernel, rope=rope, halo=halo, seq=seq),
        grid=(t // tm,),
        in_specs=in_specs,
        out_specs=[row(ATTN_Q_W), kv_spec, kv_spec] + [row(w) for w in widths] + [row(2 * D_MODEL)],
        out_shape=[jax.ShapeDtypeStruct((t, ATTN_Q_W), BF16), kv_shape, kv_shape]
                  + [jax.ShapeDtypeStruct((t, w), F32) for w in widths]
                  + [jax.ShapeDtypeStruct((t, 2 * D_MODEL), BF16)],
        compiler_params=_params(),
        name="in_proj_rope" if rope else "in_proj",
    )(*args)


def _attn_kernel(*refs, transposed, n_seq):
    n_parts = len(transposed)
    q_ref = refs[0]
    kv_refs = refs[1:1 + 2 * n_parts]
    o_ref = refs[1 + 2 * n_parts]
    tq = q_ref.shape[0] // n_seq
    qt = q_ref[...].astype(F32).T.astype(BF16)
    chains = [(s, kvh) for s in range(n_seq) for kvh in range(N_KV_HEADS)]
    qgs = [jnp.concatenate([qt[j * HEAD_DIM:(j + 1) * HEAD_DIM, s * tq:(s + 1) * tq]
                            for j in range(kvh * Q_GROUP, (kvh + 1) * Q_GROUP)], axis=1) for s, kvh in chains]

    def keys(p, s, kvh):
        k_ref = kv_refs[2 * p]
        k = k_ref[s, kvh].T if transposed[p] else k_ref[:, kvh * HEAD_DIM:(kvh + 1) * HEAD_DIM]
        return k.astype(BF16)

    def values_t(p, s, kvh):
        v_ref = kv_refs[2 * p + 1]
        v = v_ref[s, kvh] if transposed[p] else v_ref[:, kvh * HEAD_DIM:(kvh + 1) * HEAD_DIM].T
        return v.astype(BF16)

    ss = [[_dot(keys(p, s, kvh), qg) for p in range(n_parts)] for (s, kvh), qg in zip(chains, qgs)]
    ms = [functools.reduce(jnp.maximum, [jnp.max(sc, axis=0, keepdims=True) for sc in sp]) for sp in ss]
    ps = [[jnp.exp2(sc - m) for sc in sp] for sp, m in zip(ss, ms)]
    dens = [functools.reduce(jnp.add, [jnp.sum(p, axis=0, keepdims=True) for p in pp]) for pp in ps]
    accs = [functools.reduce(jnp.add, [_dot(values_t(p, s, kvh), pr.astype(BF16)) for p, pr in enumerate(pp)])
            for (s, kvh), pp in zip(chains, ps)]
    for s in range(n_seq):
        outs = []
        for kvh in range(N_KV_HEADS):
            o = accs[s * N_KV_HEADS + kvh] / dens[s * N_KV_HEADS + kvh]
            outs += [o[:, g * tq:(g + 1) * tq] for g in range(Q_GROUP)]
        o_ref[s * tq:(s + 1) * tq, :] = jnp.concatenate(outs, axis=0).T.astype(BF16)


def _attention(q, parts, seq_q, tq, n_seq=1):
    t = q.shape[0]
    nq = seq_q // tq
    assert n_seq == 1 or (nq == 1 and all(k.ndim == 4 for k, _, _ in parts))
    in_specs = [pl.BlockSpec((n_seq * tq, ATTN_Q_W), lambda b, i: (b * nq + i, 0))]
    args = [q]
    transposed = []
    for k, v, seq_k in parts:
        transposed.append(k.ndim == 4)
        if k.ndim == 4:
            in_specs += [pl.BlockSpec((n_seq, N_KV_HEADS, HEAD_DIM, seq_k), lambda b, i: (b, 0, 0, 0))] * 2
        else:
            in_specs += [pl.BlockSpec((seq_k, ATTN_KV_W), lambda b, i: (b, 0))] * 2
        args += [k, v]
    return pl.pallas_call(
        functools.partial(_attn_kernel, transposed=tuple(transposed), n_seq=n_seq),
        grid=(t // (seq_q * n_seq), nq),
        in_specs=in_specs,
        out_specs=pl.BlockSpec((n_seq * tq, ATTN_Q_W), lambda b, i: (b * nq + i, 0)),
        out_shape=jax.ShapeDtypeStruct((t, ATTN_Q_W), BF16),
        compiler_params=_params(2),
        name="attention_%dparts" % len(parts),
    )(*args)


def _block_diag(x, mask):
    reps = LANES // DN_DK
    return [jnp.where(mask, jnp.concatenate([x[:, c * LANES:(c + 1) * LANES]] * reps, axis=0),
                      jnp.zeros((), x.dtype)) for c in range(x.shape[1] // LANES)]


def _bdot(x, tiles, nt=False):
    op = _dot_nt if nt else _dot
    return jnp.concatenate([op(x[:, c * LANES:(c + 1) * LANES], t) for c, t in enumerate(tiles)], axis=1)


def _heads_transposed(x):
    xt = x.T
    return jnp.concatenate([xt[hb * DN_DK:(hb + 1) * DN_DK] for hb in range(HEADS_PER_GROUP)], axis=1)


def _packed_unit_inverse(lms, eye, row, col, mask):
    def mm1(x, y):
        return _bdot(x, _block_diag(y, mask))

    b = INVERSE_BASE
    ns = [-jnp.where((row // b) == (col // b), lm, 0.0) for lm in lms]
    ts = [eye + n for n in ns]
    ps = [mm1(nb, nb) for nb in (n.astype(BF16) for n in ns)]
    steps = int(math.log2(b)) - 1
    for s in range(steps):
        last = s == steps - 1
        pbs = [p.astype(BF16) for p in ps]
        prods = [mm1(t.astype(BF16) if last else jnp.concatenate([t.astype(BF16), pb], axis=0), pb)
                 for t, pb in zip(ts, pbs)]
        ts = [t + prod[:CHUNK] for t, prod in zip(ts, prods)]
        if not last:
            ps = [prod[CHUNK:] for prod in prods]
    while b < CHUNK:
        between = ((row // (2 * b)) == (col // (2 * b))) & ((row // b) != (col // b))
        tbs = [t.astype(BF16) for t in ts]
        ys = [mm1(tb, jnp.where(between, lm, 0.0).astype(BF16)) for lm, tb in zip(lms, tbs)]
        ts = [t - mm1(y.astype(BF16), tb) for t, tb, y in zip(ts, tbs, ys)]
        b *= 2
    return ts


def _delta_kernel(*refs, seq, n_seq, has_init):
    if has_init:
        (x_ref, z_ref, ba_ref, alog_ref, dtb_ref, gn_ref, s0f_ref, s0b_ref,
         o_ref, sf_ref, sb_ref, gate_s, o_s, st_s) = refs
    else:
        (x_ref, z_ref, ba_ref, alog_ref, dtb_ref, gn_ref,
         o_ref, sf_ref, sb_ref, gate_s, o_s, st_s) = refs
    n_chunks = seq // CHUNK
    rb = MXU_DIM
    n_gate = 4 * DN_HEADS

    exp_r = _iota((LANES, 4 * DN_W), 0)
    expand = ((exp_r < 3 * n_gate) & ((_iota((LANES, 4 * DN_W), 1) // DN_DK) == exp_r % n_gate)).astype(BF16)
    lane = _iota((1, LANES), 1)
    blk_r = _iota((rb, rb), 0)
    blk_c = _iota((rb, rb), 1)
    same_chunk = (blk_r // CHUNK) == (blk_c // CHUNK)
    cum_f = (same_chunk & (blk_c <= blk_r)).astype(BF16)
    cum_b = (same_chunk & (blk_c >= blk_r)).astype(BF16)
    for blk in range(n_seq * seq // rb):
        rows = slice(blk * rb, (blk + 1) * rb)
        ba = ba_ref[rows, :]
        decay = -jnp.exp(alog_ref[...]) * _softplus(ba + dtb_ref[...])
        vals = jnp.where(lane < 2 * DN_HEADS, _sigmoid(ba), jnp.where(lane < n_gate, decay, 0.0))
        narrow = jnp.where(lane < 2 * DN_HEADS, vals,
                           jnp.where(lane < 3 * DN_HEADS, _dot_exact_lhs(cum_f, vals),
                                     jnp.where(lane < n_gate, _dot_exact_lhs(cum_b, vals), 0.0)))
        t1, t2, t3 = (t.astype(F32) for t in _split3(narrow))
        stacked = t1 + pltpu.roll(t2, n_gate, axis=1) + pltpu.roll(t3, 2 * n_gate, axis=1)
        gate_s[rows, :] = _dot(stacked.astype(BF16), expand)

    bd_mask = _same_block((LANES, LANES), DN_DK)
    row = _iota((CHUNK, GW), 0)
    col = _iota((CHUNK, GW), 1) % CHUNK
    diag = row == col
    eye = diag.astype(F32)
    dirs = ((col <= row, col < row, CHUNK - 1), (col >= row, col > row, 0))
    chains = [(s, d, g) for s in range(n_seq) for d in range(2) for g in range(N_GROUPS)]
    for ci, (s, d, g) in enumerate(chains):
        if has_init:
            st_s[ci] = (s0f_ref, s0b_ref)[d][s, :, g * GW:(g + 1) * GW]
        else:
            st_s[ci] = jnp.zeros((DN_DK, GW), F32)

    def bd(x):
        return _block_diag(x.astype(BF16), bd_mask)

    def chunk_step(n, carry):
        where = []
        for s, d, g in chains:
            c = n if d == 0 else n_chunks - 1 - n
            where.append((pl.ds(pl.multiple_of(s * seq + c * CHUNK, CHUNK), CHUNK), d, g))

        def load(ci, what):
            rows, d, g = where[ci]
            off = {"q": g * GW, "k": DN_W + g * GW, "v": 2 * DN_W + g * GW}
            if what in off:
                return x_ref[rows, pl.ds(off[what], GW)]
            return gate_s[rows, pl.ds((0 if what == "beta" else 2 * DN_W) + d * DN_W + g * GW, GW)]

        n_ch = len(chains)
        grams, decs = [], []
        for ci, (rows, d, g) in enumerate(where):
            incl = dirs[d][0]
            gc = load(ci, "gc")
            gc_col = jnp.sum(jnp.where(diag, gc, 0.0), axis=0, keepdims=True)
            decs.append(jnp.where(incl, jnp.exp(jnp.minimum(gc - gc_col, 0.0)), 0.0))
            k = load(ci, "k")
            lhs = jnp.concatenate([k * load(ci, "beta"), load(ci, "q")], axis=0).astype(BF16)
            grams.append(_bdot(lhs, bd(k), nt=True))
        lms = [jnp.where(dirs[d][1], gm[:CHUNK] * dec, 0.0) for (_, d, _), gm, dec in zip(where, grams, decs)]
        attn = [(gm[CHUNK:] * dec).astype(BF16) for gm, dec in zip(grams, decs)]
        ts = [t.astype(BF16) for t in _packed_unit_inverse(lms, eye, row, col, bd_mask)]
        us, ws = [], []
        for ci, t in enumerate(ts):
            beta = load(ci, "beta")
            us.append(_bdot(t, bd(load(ci, "v") * beta)))
            ws.append(_bdot(t, bd(load(ci, "k") * beta * jnp.exp(load(ci, "gc")))))
        states = [st_s[ci] for ci in range(n_ch)]
        ws_qs = [_bdot(jnp.concatenate([w, load(ci, "q") * jnp.exp(load(ci, "gc"))], axis=0).astype(BF16), bd(st))
                 for ci, (w, st) in enumerate(zip(ws, states))]
        g_lasts, lhs2 = [], []
        for ci, (rows, d, g) in enumerate(where):
            gc = load(ci, "gc")
            last_row = dirs[d][2]
            g_last = gc[last_row:last_row + 1, :]
            g_lasts.append(g_last)
            k_dec = load(ci, "k") * jnp.exp(g_last - gc)
            lhs2.append(jnp.concatenate([attn[ci], _heads_transposed(k_dec).astype(BF16)], axis=0))
        v_bds = [bd(u - x[:CHUNK]) for u, x in zip(us, ws_qs)]
        avs = [_bdot(l2, vb) for l2, vb in zip(lhs2, v_bds)]
        for ci, (rows, d, g) in enumerate(where):
            st_s[ci] = states[ci] * jnp.exp(g_lasts[ci]) + avs[ci][CHUNK:]
            o_s[d, rows, g * GW:(g + 1) * GW] = ws_qs[ci][CHUNK:] + avs[ci][:CHUNK]
        return carry

    lax.fori_loop(0, n_chunks, chunk_step, 0)

    for blk in range(n_seq * seq // rb):
        rows = slice(blk * rb, (blk + 1) * rb)
        o = o_s[0, rows, :] + o_s[1, rows, :]
        o = o * lax.rsqrt(_head_sumsq(o, DN_DV) * (1.0 / DN_DV) + EPS) * gn_ref[...]
        o_ref[rows, :] = (o * _silu(z_ref[rows, :])).astype(BF16)
    for ci, (s, d, g) in enumerate(chains):
        st = st_s[ci]
        for hb in range(HEADS_PER_GROUP):
            (sf_ref, sb_ref)[d][s, g * HEADS_PER_GROUP + hb] = st[:, hb * DN_DV:(hb + 1) * DN_DV]


def _delta(cqkv, dz, ba, lw, seq, n_seq, init=None):
    t = cqkv.shape[0]
    nb = t // seq
    rows = n_seq * seq
    has_init = init is not None
    seq_block = lambda w: pl.BlockSpec((rows, w), lambda b: (b, 0))
    state_block = pl.BlockSpec((n_seq, DN_DK, DN_W), lambda b: (b, 0, 0))
    in_specs = [seq_block(3 * DN_W), seq_block(DN_W), seq_block(LANES),
                _resident((1, LANES)), _resident((1, LANES)), _resident((1, DN_W))]
    args = [cqkv, dz, ba, lw["a_log"], lw["dt_bias"], lw["gn"]]
    if has_init:
        in_specs += [state_block, state_block]
        args += list(init)
    final_block = pl.BlockSpec((n_seq, DN_HEADS, DN_DK, DN_DV), lambda b: (b, 0, 0, 0))
    final_shape = jax.ShapeDtypeStruct((nb, DN_HEADS, DN_DK, DN_DV), F32)
    return pl.pallas_call(
        functools.partial(_delta_kernel, seq=seq, n_seq=n_seq, has_init=has_init),
        grid=(nb // n_seq,),
        in_specs=in_specs,
        out_specs=[seq_block(DN_W), final_block, final_block],
        out_shape=[jax.ShapeDtypeStruct((t, DN_W), BF16), final_shape, final_shape],
        scratch_shapes=[pltpu.VMEM((rows, 4 * DN_W), F32),
                        pltpu.VMEM((2, rows, DN_W), F32),
                        pltpu.VMEM((n_seq * 2 * N_GROUPS, DN_DK, GW), F32)],
        compiler_params=_params(),
        name="delta_rule_init" if has_init else "delta_rule",
    )(*args)


def _round_weights_once(layer, srcs, dsts, stage, sems):
    chunks = []
    for k, dst in enumerate(dsts):
        rows, width = dst.shape
        chunks += [(k, r0, min(CAST_ROWS, rows - r0), width) for r0 in range(0, rows, CAST_ROWS)]

    def copy_in(j):
        k, r0, n, width = chunks[j]
        return pltpu.make_async_copy(srcs[k].at[layer, pl.ds(r0, n), :],
                                     stage.at[j % 2, pl.ds(0, n), pl.ds(0, width)], sems.at[j % 2])

    copy_in(0).start()
    for j, (k, r0, n, width) in enumerate(chunks):
        if j + 1 < len(chunks):
            copy_in(j + 1).start()
        copy_in(j).wait()
        dsts[k][r0:r0 + n, :] = stage[j % 2, 0:n, 0:width].astype(BF16)


def _post_kernel(*refs, layer, n_ctx_tiles):
    (xp_ref, xs_ref, oap_ref, oas_ref, odp_ref, ods_ref, gabp_ref, gabs_ref, mod_ref, g2_ref) = refs[:10]
    n_w = 6
    srcs = refs[10:10 + n_w]
    yp_ref, ys_ref = refs[10 + n_w:12 + n_w]
    weights = refs[12 + n_w:12 + 2 * n_w]
    stage, sems = refs[12 + 2 * n_w:]
    step = pl.program_id(0)

    @pl.when(step == 0)
    def _():
        _round_weights_once(layer, srcs, weights, stage, sems)

    wpa_ref, wpb_ref, wo_ref, wg_ref, wu_ref, wd_ref = weights

    def block(x_ref, oa_ref, od_ref, gab_ref, y_ref):
        _, _, gate1 = _mod_parts(mod_ref, True)
        shift2, scale2, gate2 = _mod_parts(mod_ref, False)
        gab = gab_ref[...].astype(F32)
        merged = (_sigmoid(gab[:, :D_MODEL]) * _dot(oa_ref[...], wpa_ref[...])
                  + _sigmoid(gab[:, D_MODEL:]) * _dot(od_ref[...], wpb_ref[...]))
        x1 = x_ref[...] + gate1 * _dot(merged.astype(BF16), wo_ref[...])
        h2 = _rms_mod(x1, g2_ref[...], shift2, scale2).astype(BF16)
        act = _silu(_dot(h2, wg_ref[...])) * _dot(h2, wu_ref[...])
        y_ref[...] = x1 + gate2 * _dot(act.astype(BF16), wd_ref[...])

    @pl.when(step < n_ctx_tiles)
    def _():
        block(xp_ref, oap_ref, odp_ref, gabp_ref, yp_ref)

    @pl.when(step >= n_ctx_tiles)
    def _():
        block(xs_ref, oas_ref, ods_ref, gabs_ref, ys_ref)


def _post(ctx, lat, mod, lat_tiles_per_group, norm2_g, weights, layer, tm):
    n_ctx = ctx[0].shape[0] // tm
    n_lat = lat[0].shape[0] // tm
    widths = (D_MODEL, ATTN_Q_W, DN_W, 2 * D_MODEL)
    ctx_block = lambda w: pl.BlockSpec((tm, w), lambda i: (jnp.minimum(i, n_ctx - 1), 0))
    lat_block = lambda w: pl.BlockSpec((tm, w), lambda i: (jnp.maximum(i - n_ctx, 0), 0))
    group = lambda i: jnp.where(i < n_ctx, 0, 1 + jnp.maximum(i - n_ctx, 0) // lat_tiles_per_group)
    in_specs, args = [], []
    for w, a, b in zip(widths, ctx, lat):
        in_specs += [ctx_block(w), lat_block(w)]
        args += [a, b]
    in_specs += [pl.BlockSpec((None, 1, 6 * D_MODEL), lambda i: (group(i), 0, 0)), _resident((1, D_MODEL))]
    in_specs += [pl.BlockSpec(memory_space=pl.ANY)] * len(weights)
    shapes = [w.shape[1:] for w in weights]
    return pl.pallas_call(
        functools.partial(_post_kernel, layer=layer, n_ctx_tiles=n_ctx),
        grid=(n_ctx + n_lat,),
        in_specs=in_specs,
        out_specs=[ctx_block(D_MODEL), lat_block(D_MODEL)],
        out_shape=[jax.ShapeDtypeStruct(ctx[0].shape, F32), jax.ShapeDtypeStruct(lat[0].shape, F32)],
        scratch_shapes=([pltpu.VMEM(sh, BF16) for sh in shapes]
                        + [pltpu.VMEM((2, CAST_ROWS, max(sh[1] for sh in shapes)), F32),
                           pltpu.SemaphoreType.DMA((2,))]),
        compiler_params=_params(),
        name="post_block",
    )(*args, mod, norm2_g, *weights)


def _rope_tables(n_tokens):
    quarter = HEAD_DIM // 4
    lane = jnp.arange(LANES)
    d = lane % HEAD_DIM
    inv = ROPE_THETA ** (-(d % quarter).astype(F32) / quarter)
    t = jnp.arange(n_tokens)
    pos = jnp.where(d[None, :] < HEAD_DIM // 2, (t // GRID_W)[:, None], (t % GRID_W)[:, None]).astype(F32)
    ang = pos * inv[None, :]
    sign = jnp.where((d % (2 * quarter)) < quarter, -1.0, 1.0)
    return jnp.cos(ang), jnp.sin(ang) * sign[None, :]


def _pack_states(s):
    b = s.shape[0]
    return s.transpose(0, 2, 1, 3).reshape(b, DN_DK, DN_W)


def _layer_weights(l, w_main, w_gab, norm1_g, q_norm_g, k_norm_g, a_log, dt_bias, dn_norm_g, norm2_g):
    pad_small = lambda a: jnp.pad(a.reshape(1, -1), ((0, 0), (0, LANES - a.size)))
    return dict(
        w_in=w_main,
        w_gab=w_gab,
        norm1_g=norm1_g[l].reshape(1, D_MODEL),
        norm2_g=norm2_g[l].reshape(1, D_MODEL),
        gq=jnp.tile(q_norm_g[l], N_Q_HEADS).reshape(1, ATTN_Q_W),
        gk=jnp.tile(k_norm_g[l], N_KV_HEADS).reshape(1, ATTN_KV_W),
        a_log=pad_small(jnp.concatenate([jnp.zeros((2 * DN_HEADS,), F32), a_log[l].reshape(-1)])),
        dt_bias=pad_small(jnp.concatenate([jnp.zeros((2 * DN_HEADS,), F32), dt_bias[l].reshape(-1)])),
        gn=jnp.tile(dn_norm_g[l], DN_HEADS).reshape(1, DN_W),
    )


def kernel(x_prompt, x_sample, cache_k, cache_v, state_fwd, state_bwd, c, c_ctx, w_ada, b_ada, norm1_g, w_in,
           q_norm_g, k_norm_g, conv_w, a_log, dt_bias, dn_norm_g, w_pa, w_pb, w_o, norm2_g, w_gate, w_up, w_down):
    batch, seq, _ = x_prompt.shape
    dec_batch, dec_seq, _ = x_sample.shape
    depth = w_in.shape[0]
    past = cache_k.shape[2]
    assert dec_batch + 1 <= SUBLANES and seq % MXU_DIM == 0 and dec_seq % MXU_DIM == 0
    assert batch % SEQS_PER_STEP == 0 and dec_batch % SEQS_PER_STEP == 0 and batch % ATTN_SEQS_PER_STEP == 0

    cvecs = jnp.zeros((SUBLANES, D_MODEL), F32).at[0].set(c_ctx).at[1:1 + dec_batch].set(c)
    rope_tables = _rope_tables(dec_seq)
    tm_in = IN_PROJ_TILE
    tm_post = POST_TILE
    ctx_group = lambda i: 0

    def lat_group(tm):
        return lambda i: 1 + i // (dec_seq // tm)

    yp = x_prompt.reshape(batch * seq, D_MODEL)
    ys = x_sample.reshape(dec_batch * dec_seq, D_MODEL)
    ks_out, vs_out, sf_out, sb_out = [], [], [], []
    for l in range(depth):
        lw = _layer_weights(l, *_round_w_in(w_in, l), norm1_g, q_norm_g, k_norm_g, a_log, dt_bias, dn_norm_g, norm2_g)
        mod = _modulation(cvecs, w_ada[l], b_ada[l])[:1 + dec_batch].reshape(1 + dec_batch, 1, 6 * D_MODEL)

        q, kt, vt, cqkv, dz, ba, gab = _in_proj(yp, mod, ctx_group, lw, conv_w, l, tm_in, seq)
        oa = _attention(q, [(kt, vt, seq)], seq, seq, ATTN_SEQS_PER_STEP)
        od, sf, sb = _delta(cqkv, dz, ba, lw, seq, SEQS_PER_STEP)
        ctx_parts = (yp, oa, od, gab)
        ks_out.append(kt.transpose(0, 3, 1, 2))
        vs_out.append(vt.transpose(0, 3, 1, 2))
        sf_out.append(sf)
        sb_out.append(sb)

        q, kr, v, cqkv, dz, ba, gab = _in_proj(ys, mod, lat_group(tm_in), lw, conv_w, l, tm_in, dec_seq,
                                               rope_tables)
        ck = cache_k[:, l].reshape(dec_batch * past, ATTN_KV_W)
        cv = cache_v[:, l].reshape(dec_batch * past, ATTN_KV_W)
        oa = _attention(q, [(ck, cv, past), (kr, v, dec_seq)], dec_seq, ATTN_Q_TILE)
        init = (_pack_states(state_fwd[:, l]), _pack_states(state_bwd[:, l]))
        od, _, _ = _delta(cqkv, dz, ba, lw, dec_seq, SEQS_PER_STEP, init)
        yp, ys = _post(ctx_parts, (ys, oa, od, gab), mod, dec_seq // tm_post, lw["norm2_g"],
                       [w_pa, w_pb, w_o, w_gate, w_up, w_down], l, tm_post)

    return (yp.reshape(batch, seq, D_MODEL), ys.reshape(dec_batch, dec_seq, D_MODEL),
            jnp.stack(ks_out, axis=1), jnp.stack(vs_out, axis=1),
            jnp.stack(sf_out, axis=1), jnp.stack(sb_out, axis=1))
```
